```python
import jax, jax.numpy as jnp
from jax import lax
import numpy as np

D_MODEL = 2048
BATCH = 2
SEQ = 4096
DEPTH = 1
DEC_BATCH = 16
DEC_SEQ = 32
PAST_LEN = 1024

CHUNK = 64
Q_BLOCK = 128
N_META = 16
H_GLA = 4
GLA_DK_HEAD = D_MODEL // 2 // H_GLA
GLA_DV_HEAD = D_MODEL // H_GLA
GLA_RANK = 16
GLA_TAU = 16.0
H_SB = 16
SB_HEAD_DIM = D_MODEL // H_SB
D_FF = 4 * D_MODEL
EPS = 1e-5

GLA_QK = H_GLA * GLA_DK_HEAD
GLA_V = H_GLA * GLA_DV_HEAD
SB_W = H_SB * SB_HEAD_DIM
OFF_GK = GLA_QK
OFF_GV = 2 * GLA_QK
OFF_ALOW = 2 * GLA_QK + GLA_V
OFF_SQ = OFF_ALOW + GLA_RANK
OFF_SK = OFF_SQ + SB_W
OFF_SV = OFF_SK + SB_W
OFF_GATE_A = OFF_SV + SB_W
OFF_GATE_B = OFF_GATE_A + GLA_V
D_IN = OFF_GATE_B + SB_W
IN_SPLITS = (OFF_GK, OFF_GV, OFF_ALOW, OFF_SQ, OFF_SK, OFF_SV, OFF_GATE_A, OFF_GATE_B)

kernel_name = 'gla_stickbreak_streaming_encoder'


def rms_norm(x, g):
    xf = x.astype(jnp.float32)
    y = xf * lax.rsqrt(jnp.mean(xf * xf, axis=-1, keepdims=True) + EPS)
    return (y * g.astype(jnp.float32)).astype(x.dtype)


def mixer_inputs(x, norm1_g, w_in, w_alpha_up, b_alpha):
    b, t, _ = x.shape
    h = rms_norm(x, norm1_g)
    p = h @ w_in
    gq, gk, gv, a_low, sq, sk, sv, gate_a, gate_b = jnp.split(p, IN_SPLITS, axis=-1)
    log_alpha = jax.nn.log_sigmoid((a_low @ w_alpha_up + b_alpha).astype(jnp.float32)) / GLA_TAU
    gla = tuple(a.reshape(b, t, H_GLA, -1) for a in (gq, gk, gv, log_alpha))
    sb = tuple(a.reshape(b, t, H_SB, SB_HEAD_DIM) for a in (sq, sk, sv))
    return gla, sb, gate_a, gate_b


def gla_chunked(q, k, v, log_alpha, s0):
    b, t, h, dk = q.shape
    dv = v.shape[-1]
    pad = (-t) % CHUNK
    n = (t + pad) // CHUNK

    def to_chunks(a):
        a = jnp.pad(a.astype(jnp.float32), ((0, 0), (0, pad), (0, 0), (0, 0)))
        return a.reshape(b, n, CHUNK, h, a.shape[-1]).transpose(1, 0, 3, 2, 4)

    qc = to_chunks(q) * (dk ** -0.5)
    kc = to_chunks(k)
    vc = to_chunks(v)
    gc = to_chunks(log_alpha)
    causal = jnp.tril(jnp.ones((CHUNK, CHUNK), dtype=bool))

    def step(s, inp):
        qi, ki, vi, gi = inp
        cum = jnp.cumsum(gi, axis=-2)
        q_dec = qi * jnp.exp(cum)
        k_dec = ki * jnp.exp(-cum)
        att = jnp.where(causal, jnp.einsum('bhtk,bhsk->bhts', q_dec, k_dec), 0.0)
        o = jnp.einsum('bhts,bhsv->bhtv', att, vi) + jnp.einsum('bhtk,bhkv->bhtv', q_dec, s)
        last = cum[:, :, -1, :]
        s_new = jnp.exp(last)[..., None] * s + jnp.einsum(
            'bhsk,bhsv->bhkv', ki * jnp.exp(last[:, :, None, :] - cum), vi)
        return s_new, o

    s_t, o = lax.scan(step, s0.astype(jnp.float32), (qc, kc, vc, gc))
    o = o.transpose(1, 0, 3, 2, 4).reshape(b, n * CHUNK, h, dv)[:, :t]
    return o, s_t


def stick_breaking(q, k, v, q_pos, k_pos):
    z = jnp.einsum('bqhd,bkhd->bhqk', q.astype(jnp.float32), k.astype(jnp.float32)) * (q.shape[-1] ** -0.5)
    mask = k_pos[None, :] < q_pos[:, None]
    log_1m = jnp.where(mask, jax.nn.log_sigmoid(-z), 0.0)
    after = lax.cumsum(log_1m, axis=3, reverse=True) - log_1m
    w = jnp.where(mask, jnp.exp(jax.nn.log_sigmoid(z) + after), 0.0)
    return jnp.einsum('bhqk,bkhd->bqhd', w, v.astype(jnp.float32))


def sb_prompt(q, k, v):
    b, t, h, d = q.shape
    pad = (-t) % Q_BLOCK
    nb = (t + pad) // Q_BLOCK
    qp, kp, vp = (jnp.pad(a, ((0, 0), (0, pad), (0, 0), (0, 0))) for a in (q, k, v))
    pos = jnp.arange(t + pad)
    qb = qp.reshape(b, nb, Q_BLOCK, h, d).swapaxes(0, 1)
    pb = pos.reshape(nb, Q_BLOCK)
    ob = lax.map(lambda a: stick_breaking(a[0], kp, vp, a[1], pos), (qb, pb))
    return ob.swapaxes(0, 1).reshape(b, t + pad, h, d)[:, :t]


def merge_and_ffn(x, o_gla, o_sb, gate_a, gate_b, gla_norm_g, w_out, norm2_g, w_up, w_down):
    b, t, _ = x.shape
    of = o_gla.astype(jnp.float32)
    of = of * lax.rsqrt(jnp.mean(of * of, axis=-1, keepdims=True) + EPS) * gla_norm_g.astype(jnp.float32)
    mix = (jax.nn.sigmoid(gate_a.astype(jnp.float32)) * of.reshape(b, t, -1)
           + jax.nn.sigmoid(gate_b.astype(jnp.float32)) * o_sb.astype(jnp.float32).reshape(b, t, -1))
    x = x + mix.astype(x.dtype) @ w_out
    h = rms_norm(x, norm2_g)
    x = x + jnp.square(jax.nn.relu(h @ w_up)) @ w_down
    return x


def setup_inputs(seed: int = 0) -> dict:
    key = jax.random.key(seed)
    ks = jax.random.split(key, 20)
    f32 = jnp.float32
    nrm = lambda k, shape, s: jax.random.normal(k, shape, f32) * s
    return {
        'x_prompt': nrm(ks[0], (BATCH, SEQ, D_MODEL), 1.0),
        'x_sample': nrm(ks[1], (DEC_BATCH, DEC_SEQ, D_MODEL), 1.0),
        'cache_sb_k': nrm(ks[2], (DEPTH, DEC_BATCH, PAST_LEN, H_SB, SB_HEAD_DIM), 1.0),
        'cache_sb_v': nrm(ks[3], (DEPTH, DEC_BATCH, PAST_LEN, H_SB, SB_HEAD_DIM), 1.0),
        'state_gla': nrm(ks[4], (DEPTH, DEC_BATCH, H_GLA, GLA_DK_HEAD, GLA_DV_HEAD), 1.0),
        'meta_tokens': nrm(ks[5], (N_META, D_MODEL), 1.0),
        'norm1_g': 1.0 + nrm(ks[6], (DEPTH, D_MODEL), 0.02),
        'w_in': nrm(ks[7], (DEPTH, D_MODEL, D_IN), D_MODEL ** -0.5),
        'w_alpha_up': nrm(ks[8], (DEPTH, GLA_RANK, GLA_QK), GLA_RANK ** -0.5),
        'b_alpha': nrm(ks[9], (DEPTH, GLA_QK), 0.1),
        'gla_norm_g': 1.0 + nrm(ks[10], (DEPTH, H_GLA, GLA_DV_HEAD), 0.02),
        'w_out': nrm(ks[11], (DEPTH, D_MODEL, D_MODEL), D_MODEL ** -0.5),
        'norm2_g': 1.0 + nrm(ks[12], (DEPTH, D_MODEL), 0.02),
        'w_up': nrm(ks[13], (DEPTH, D_MODEL, D_FF), D_MODEL ** -0.5),
        'w_down': nrm(ks[14], (DEPTH, D_FF, D_MODEL), D_FF ** -0.5),
        'norm_f_g': 1.0 + nrm(ks[15], (D_MODEL,), 0.02),
    }


def reference(x_prompt, x_sample, cache_sb_k, cache_sb_v, state_gla, meta_tokens, norm1_g, w_in,
              w_alpha_up, b_alpha, gla_norm_g, w_out, norm2_g, w_up, w_down, norm_f_g):
    b = x_prompt.shape[0]
    xp = jnp.concatenate(
        [jnp.broadcast_to(meta_tokens.astype(x_prompt.dtype)[None], (b, N_META, D_MODEL)), x_prompt], axis=1)
    xs = x_sample
    t_new = xs.shape[1]
    past = cache_sb_k.shape[2]
    gla_p, k_p, v_p, gla_s, k_s, v_s = [], [], [], [], [], []
    for l in range(DEPTH):
        (gq, gk, gv, ga), (sq, sk, sv), gate_a, gate_b = mixer_inputs(
            xp, norm1_g[l], w_in[l], w_alpha_up[l], b_alpha[l])
        s0 = jnp.zeros((b, H_GLA, GLA_DK_HEAD, GLA_DV_HEAD), jnp.float32)
        o_gla, s_t = gla_chunked(gq, gk, gv, ga, s0)
        o_sb = sb_prompt(sq, sk, sv)
        xp = merge_and_ffn(xp, o_gla, o_sb, gate_a, gate_b, gla_norm_g[l], w_out[l], norm2_g[l], w_up[l], w_down[l])
        gla_p.append(s_t.astype(x_prompt.dtype))
        k_p.append(sk)
        v_p.append(sv)
        (gq, gk, gv, ga), (sq, sk, sv), gate_a, gate_b = mixer_inputs(
            xs, norm1_g[l], w_in[l], w_alpha_up[l], b_alpha[l])
        o_gla, s_new = gla_chunked(gq, gk, gv, ga, state_gla[l])
        k_all = jnp.concatenate([cache_sb_k[l].astype(sk.dtype), sk], axis=1)
        v_all = jnp.concatenate([cache_sb_v[l].astype(sv.dtype), sv], axis=1)
        o_sb = stick_breaking(sq, k_all, v_all, past + jnp.arange(t_new), jnp.arange(past + t_new))
        xs = merge_and_ffn(xs, o_gla, o_sb, gate_a, gate_b, gla_norm_g[l], w_out[l], norm2_g[l], w_up[l], w_down[l])
        gla_s.append(s_new.astype(x_sample.dtype))
        k_s.append(sk)
        v_s.append(sv)
    y_prompt = rms_norm(xp, norm_f_g)[:, N_META:]
    y_sample = rms_norm(xs, norm_f_g)
    return (y_prompt, y_sample, jnp.stack(gla_p), jnp.stack(k_p), jnp.stack(v_p),
            jnp.stack(gla_s), jnp.stack(k_s), jnp.stack(v_s))
```

```python
import functools

import jax
import jax.numpy as jnp
from jax import lax
from jax.experimental import pallas as pl
from jax.experimental.pallas import tpu as pltpu

F32 = jnp.float32
BF16 = jnp.bfloat16

D_MODEL = 2048
N_META = 16
H_GLA = 4
GLA_DK = 256
GLA_DV = 512
GLA_RANK = 16
GLA_TAU = 16.0
GLA_CHUNK = 64
H_SB = 16
SB_HD = 128
D_FF = 4 * D_MODEL
EPS = 1e-5
GLA_QK = H_GLA * GLA_DK
GLA_V = H_GLA * GLA_DV
SB_W = H_SB * SB_HD

_R_GK = GLA_QK
_R_GV = 2 * GLA_QK
_R_ALOW = _R_GV + GLA_V
_R_SQ = _R_ALOW + GLA_RANK
_R_SK = _R_SQ + SB_W
_R_SV = _R_SK + SB_W
_R_GA = _R_SV + SB_W
_R_GB = _R_GA + GLA_V
_R_END = _R_GB + SB_W

P_GQ = 0
P_GK = P_GQ + GLA_QK
P_GV = P_GK + GLA_QK
P_SQ = P_GV + GLA_V
P_SK = P_SQ + SB_W
P_SV = P_SK + SB_W
P_GA = P_SV + SB_W
P_GB = P_GA + GLA_V
P_W = P_GB + SB_W

LANE = 128
SUBLANE = 8
TM = 512
TN = 1024
TF = 512
GLA_ROWS = 256
SB_TILE = 256
SB_MINI = 32
SB_STRIDE = SB_MINI // SUBLANE
SB_CUTOFF = -104.0
VMEM_LIMIT = 56 * 1024 * 1024


def _dot(a, b):
    return jnp.dot(a, b, preferred_element_type=F32)


def _dot_nt(a, b):
    return lax.dot_general(a, b, (((1,), (1,)), ((), ())), preferred_element_type=F32)


def _dot_tn(a, b):
    return lax.dot_general(a, b, (((0,), (0,)), ((), ())), preferred_element_type=F32)


def _split2(x):
    hi = x.astype(BF16)
    lo = (x - hi.astype(F32)).astype(BF16)
    return hi, lo


def _rms(x, g):
    ms = jnp.mean(x * x, axis=-1, keepdims=True)
    return x * lax.rsqrt(ms + EPS) * g


def _log_sigmoid(x):
    return jnp.minimum(x, 0.0) - jnp.log(1.0 + jnp.exp(-jnp.abs(x)))


_N_SK0 = P_SK // TN
_N_SV0 = P_SV // TN
_N_GATE0 = P_GA // TN


def _inproj_kernel(n_prompt_tiles, xp_ref, xs_ref, meta_ref, g1_ref, w_ref, wal_ref,
                   p_ref, pm_ref, k32_ref, v32_ref, km_ref, vm_ref, al_ref, alm_ref, h_ref):
    m = pl.program_id(0)
    n = pl.program_id(1)

    @pl.when(n == 0)
    def _():
        g = g1_ref[...]

        @pl.when(m < n_prompt_tiles)
        def _():
            h_ref[0:TM, :] = _rms(xp_ref[...], g).astype(BF16)

        @pl.when(m >= n_prompt_tiles)
        def _():
            h_ref[0:TM, :] = _rms(xs_ref[...], g).astype(BF16)

        h_ref[TM:TM + N_META, :] = _rms(meta_ref[...], g).astype(BF16)
        al = _dot(h_ref[...], wal_ref[...])
        al_ref[...] = al[:TM]
        alm_ref[...] = al[TM:]

    acc = _dot(h_ref[...], w_ref[...])
    main = acc[:TM]
    met = acc[TM:]
    is_gate = n >= _N_GATE0
    first = m == 0

    @pl.when(jnp.logical_not(is_gate))
    def _():
        p_ref[...] = main.astype(BF16)

    @pl.when(is_gate)
    def _():
        p_ref[...] = jax.nn.sigmoid(main).astype(BF16)

    @pl.when(first)
    def _():
        pm_ref[...] = met.astype(BF16)

    is_k = (n >= _N_SK0) & (n < _N_SV0)
    is_v = (n >= _N_SV0) & (n < _N_GATE0)

    @pl.when(is_k)
    def _():
        k32_ref[...] = main

    @pl.when(is_v)
    def _():
        v32_ref[...] = main

    @pl.when(is_k & first)
    def _():
        km_ref[...] = met

    @pl.when(is_v & first)
    def _():
        vm_ref[...] = met


def _inproj(xp, xs, meta, g1, w_main, w_alow):
    npr, nsr = xp.shape[0], xs.shape[0]
    npt, nst = npr // TM, nsr // TM
    nt = npt + nst
    rows = npr + nsr
    nn = P_W // TN
    kcol = lambda n: jnp.clip(n - _N_SK0, 0, SB_W // TN - 1)
    vcol = lambda n: jnp.clip(n - _N_SV0, 0, SB_W // TN - 1)
    once = lambda m, col, last: jnp.where(m == 0, col, last)
    return pl.pallas_call(
        functools.partial(_inproj_kernel, npt),
        grid=(nt, nn),
        in_specs=[
            pl.BlockSpec((TM, D_MODEL), lambda m, n: (jnp.minimum(m, npt - 1), 0)),
            pl.BlockSpec((TM, D_MODEL), lambda m, n: (jnp.maximum(m - npt, 0), 0)),
            pl.BlockSpec((N_META, D_MODEL), lambda m, n: (0, 0)),
            pl.BlockSpec((1, D_MODEL), lambda m, n: (0, 0)),
            pl.BlockSpec((D_MODEL, TN), lambda m, n: (0, n)),
            pl.BlockSpec((D_MODEL, LANE), lambda m, n: (0, 0)),
        ],
        out_specs=[
            pl.BlockSpec((TM, TN), lambda m, n: (m, n)),
            pl.BlockSpec((N_META, TN), lambda m, n: (0, once(m, n, nn - 1))),
            pl.BlockSpec((TM, TN), lambda m, n: (m, kcol(n))),
            pl.BlockSpec((TM, TN), lambda m, n: (m, vcol(n))),
            pl.BlockSpec((N_META, TN), lambda m, n: (0, once(m, kcol(n), SB_W // TN - 1))),
            pl.BlockSpec((N_META, TN), lambda m, n: (0, once(m, vcol(n), SB_W // TN - 1))),
            pl.BlockSpec((TM, LANE), lambda m, n: (m, 0)),
            pl.BlockSpec((N_META, LANE), lambda m, n: (0, 0)),
        ],
        out_shape=[
            jax.ShapeDtypeStruct((rows, P_W), BF16),
            jax.ShapeDtypeStruct((N_META, P_W), BF16),
            jax.ShapeDtypeStruct((rows, SB_W), F32),
            jax.ShapeDtypeStruct((rows, SB_W), F32),
            jax.ShapeDtypeStruct((N_META, SB_W), F32),
            jax.ShapeDtypeStruct((N_META, SB_W), F32),
            jax.ShapeDtypeStruct((rows, LANE), F32),
            jax.ShapeDtypeStruct((N_META, LANE), F32),
        ],
        scratch_shapes=[pltpu.VMEM((TM + N_META, D_MODEL), BF16)],
        compiler_params=pltpu.CompilerParams(
            dimension_semantics=("arbitrary", "arbitrary"), vmem_limit_bytes=VMEM_LIMIT),
        name="inproj",
    )(xp, xs, meta, g1, w_main, w_alow)


def _gla_chunk(c, q, k, v, al, wup_hi, wup_lo, bias, st_ref):
    al_hi, al_lo = _split2(al)
    x = _dot(al_hi, wup_hi) + _dot(al_hi, wup_lo) + _dot(al_lo, wup_hi) + bias
    g = _log_sigmoid(x) * (1.0 / GLA_TAU)
    g_hi, g_lo = _split2(g)
    row = lax.broadcasted_iota(jnp.int32, (c, c), 0)
    col = lax.broadcasted_iota(jnp.int32, (c, c), 1)
    causal = col <= row
    tri = causal.astype(BF16)
    cum = _dot(tri, g_hi) + _dot(tri, g_lo)
    outs = []
    for h in range(H_GLA):
        ch = cum[:, h * GLA_DK:(h + 1) * GLA_DK]
        last = ch[c - 1:c, :]
        qh = q[:, h * GLA_DK:(h + 1) * GLA_DK].astype(F32)
        kh = k[:, h * GLA_DK:(h + 1) * GLA_DK].astype(F32)
        vh = v[:, h * GLA_DV:(h + 1) * GLA_DV]
        qd = (qh * jnp.exp(ch) * (GLA_DK ** -0.5)).astype(BF16)
        kd = (kh * jnp.exp(-ch)).astype(BF16)
        kr = (kh * jnp.exp(last - ch)).astype(BF16)
        att = jnp.where(causal, _dot_nt(qd, kd), 0.0).astype(BF16)
        st = st_ref[h]
        outs.append(_dot(att, vh) + _dot_nt(qd, st.astype(BF16)))
        st_ref[h] = st * jnp.exp(last) + _dot_tn(vh, kr)
    return outs


def _gla_finish(o, gn, sa):
    ms = jnp.mean(o * o, axis=-1, keepdims=True)
    return (o * lax.rsqrt(ms + EPS) * gn * sa.astype(F32)).astype(BF16)


def _gla_prompt_kernel(q_ref, k_ref, v_ref, sa_ref, al_ref, km_ref, vm_ref, alm_ref,
                       wh_ref, wl_ref, b_ref, gn_ref, og_ref, so_ref, st_ref):
    c_idx = pl.program_id(1)
    wup_hi = wh_ref[...]
    wup_lo = wl_ref[...]
    bias = b_ref[...]

    @pl.when(c_idx == 0)
    def _():
        st_ref[...] = jnp.zeros_like(st_ref)
        zq = jnp.zeros((N_META, GLA_QK), BF16)
        _gla_chunk(N_META, zq, km_ref[...], vm_ref[...], alm_ref[...], wup_hi, wup_lo, bias, st_ref)

    def body(i, carry):
        r0 = pl.multiple_of(i * GLA_CHUNK, GLA_CHUNK)
        rows = pl.ds(r0, GLA_CHUNK)
        outs = _gla_chunk(GLA_CHUNK, q_ref[rows, :], k_ref[rows, :], v_ref[rows, :], al_ref[rows, :],
                          wup_hi, wup_lo, bias, st_ref)
        for h in range(H_GLA):
            cols = slice(h * GLA_DV, (h + 1) * GLA_DV)
            og_ref[rows, cols] = _gla_finish(outs[h], gn_ref[:, cols], sa_ref[rows, cols])
        return carry

    lax.fori_loop(0, GLA_ROWS // GLA_CHUNK, body, 0)

    @pl.when(c_idx == pl.num_programs(1) - 1)
    def _():
        for h in range(H_GLA):
            so_ref[0, h] = st_ref[h].T


def _gla_prompt(p_main, p_meta, alow, alow_meta, wup_hi, wup_lo, bias, gn, batch, seq):
    nc = seq // GLA_ROWS
    rb = lambda b, c: b * nc + c
    const = lambda b, c: (0, 0)
    return pl.pallas_call(
        _gla_prompt_kernel,
        grid=(batch, nc),
        in_specs=[
            pl.BlockSpec((GLA_ROWS, GLA_QK), lambda b, c: (rb(b, c), P_GQ // GLA_QK)),
            pl.BlockSpec((GLA_ROWS, GLA_QK), lambda b, c: (rb(b, c), P_GK // GLA_QK)),
            pl.BlockSpec((GLA_ROWS, GLA_V), lambda b, c: (rb(b, c), P_GV // GLA_V)),
            pl.BlockSpec((GLA_ROWS, GLA_V), lambda b, c: (rb(b, c), P_GA // GLA_V)),
            pl.BlockSpec((GLA_ROWS, LANE), lambda b, c: (rb(b, c), 0)),
            pl.BlockSpec((N_META, GLA_QK), lambda b, c: (0, P_GK // GLA_QK)),
            pl.BlockSpec((N_META, GLA_V), lambda b, c: (0, P_GV // GLA_V)),
            pl.BlockSpec((N_META, LANE), const),
            pl.BlockSpec((LANE, GLA_QK), const),
            pl.BlockSpec((LANE, GLA_QK), const),
            pl.BlockSpec((1, GLA_QK), const),
            pl.BlockSpec((1, GLA_V), const),
        ],
        out_specs=[
            pl.BlockSpec((GLA_ROWS, GLA_V), lambda b, c: (rb(b, c), 0)),
            pl.BlockSpec((1, H_GLA, GLA_DK, GLA_DV), lambda b, c: (b, 0, 0, 0)),
        ],
        out_shape=[
            jax.ShapeDtypeStruct((batch * seq, GLA_V), BF16),
            jax.ShapeDtypeStruct((batch, H_GLA, GLA_DK, GLA_DV), F32),
        ],
        scratch_shapes=[pltpu.VMEM((H_GLA, GLA_DV, GLA_DK), F32)],
        compiler_params=pltpu.CompilerParams(
            dimension_semantics=("arbitrary", "arbitrary"), vmem_limit_bytes=VMEM_LIMIT),
        name="gla_prompt",
    )(p_main, p_main, p_main, p_main, alow, p_meta, p_meta, alow_meta, wup_hi, wup_lo, bias, gn)


def _gla_sample_kernel(t_new, q_ref, k_ref, v_ref, sa_ref, al_ref, s0_ref,
                       wh_ref, wl_ref, b_ref, gn_ref, og_ref, so_ref, st_ref):
    for h in range(H_GLA):
        st_ref[h] = s0_ref[0, h].T
    outs = _gla_chunk(t_new, q_ref[...], k_ref[...], v_ref[...], al_ref[...],
                      wh_ref[...], wl_ref[...], b_ref[...], st_ref)
    for h in range(H_GLA):
        cols = slice(h * GLA_DV, (h + 1) * GLA_DV)
        og_ref[:, cols] = _gla_finish(outs[h], gn_ref[:, cols], sa_ref[:, cols])
        so_ref[0, h] = st_ref[h].T


def _gla_sample(p_main, alow, state, wup_hi, wup_lo, bias, gn, row0, dec_batch, t_new):
    rb0 = row0 // t_new
    const = lambda b: (0, 0)
    return pl.pallas_call(
        functools.partial(_gla_sample_kernel, t_new),
        grid=(dec_batch,),
        in_specs=[
            pl.BlockSpec((t_new, GLA_QK), lambda b: (rb0 + b, P_GQ // GLA_QK)),
            pl.BlockSpec((t_new, GLA_QK), lambda b: (rb0 + b, P_GK // GLA_QK)),
            pl.BlockSpec((t_new, GLA_V), lambda b: (rb0 + b, P_GV // GLA_V)),
            pl.BlockSpec((t_new, GLA_V), lambda b: (rb0 + b, P_GA // GLA_V)),
            pl.BlockSpec((t_new, LANE), lambda b: (rb0 + b, 0)),
            pl.BlockSpec((1, H_GLA, GLA_DK, GLA_DV), lambda b: (b, 0, 0, 0)),
            pl.BlockSpec((LANE, GLA_QK), const),
            pl.BlockSpec((LANE, GLA_QK), const),
            pl.BlockSpec((1, GLA_QK), const),
            pl.BlockSpec((1, GLA_V), const),
        ],
        out_specs=[
            pl.BlockSpec((t_new, GLA_V), lambda b: (b, 0)),
            pl.BlockSpec((1, H_GLA, GLA_DK, GLA_DV), lambda b: (b, 0, 0, 0)),
        ],
        out_shape=[
            jax.ShapeDtypeStruct((dec_batch * t_new, GLA_V), BF16),
            jax.ShapeDtypeStruct((dec_batch, H_GLA, GLA_DK, GLA_DV), F32),
        ],
        scratch_shapes=[pltpu.VMEM((H_GLA, GLA_DV, GLA_DK), F32)],
        compiler_params=pltpu.CompilerParams(
            dimension_semantics=("arbitrary",), vmem_limit_bytes=VMEM_LIMIT),
        name="gla_sample",
    )(p_main, p_main, p_main, p_main, alow, state, wup_hi, wup_lo, bias, gn)


def _load_perm(ref, base, nk):
    parts = []
    for j in range(nk // SB_MINI):
        for c in range(SB_STRIDE):
            parts.append(ref[pl.ds(base + SB_MINI * j + c, SUBLANE, stride=SB_STRIDE), :])
    return jnp.concatenate(parts, axis=0)


def _sb_scan(zt, carry, limit):
    nk, nq = zt.shape
    rowid = lax.broadcasted_iota(jnp.int32, (SUBLANE, nq), 0)
    w_parts = [None] * (nk // SUBLANE)
    for j in reversed(range(nk // SB_MINI)):
        ls, ss, vis = [], [], []
        for c in range(SB_STRIDE):
            i = j * SB_STRIDE + c
            z = zt[i * SUBLANE:(i + 1) * SUBLANE]
            nz = -z
            t = jnp.log(1.0 + jnp.exp(jnp.minimum(z, nz)))
            l = jnp.minimum(nz, 0.0) - t
            ss.append(jnp.minimum(z, 0.0) - t)
            if limit is not None:
                v = rowid * SB_STRIDE + (SB_MINI * j + c) < limit
                l = jnp.where(v, l, 0.0)
                vis.append(v)
            ls.append(l)
        later = [None] * SB_STRIDE
        run = ls[SB_STRIDE - 1]
        for c in range(SB_STRIDE - 2, -1, -1):
            later[c] = run
            run = run + ls[c]
        incl = run
        for sh in (1, 2, 4):
            incl = incl + jnp.where(rowid < SUBLANE - sh, pltpu.roll(incl, SUBLANE - sh, axis=0), 0.0)
        off = carry + (incl - run)
        for c in range(SB_STRIDE):
            after = off if later[c] is None else off + later[c]
            w = jnp.exp(ss[c] + after)
            if limit is not None:
                w = jnp.where(vis[c], w, 0.0)
            w_parts[j * SB_STRIDE + c] = w
        carry = carry + incl[0:1, :]
    return jnp.concatenate(w_parts, axis=0), carry


def _sb_prompt_kernel(seq, q_ref, k_ref, v_ref, km_ref, vm_ref, sb_ref, o_ref, kms_ref, vms_ref):
    kms_ref[...] = jnp.zeros_like(kms_ref)
    vms_ref[...] = jnp.zeros_like(vms_ref)
    kms_ref[0:N_META, :] = km_ref[...]
    vms_ref[0:N_META, :] = vm_ref[...]
    scale = SB_HD ** -0.5
    diag_limit = lax.broadcasted_iota(jnp.int32, (SUBLANE, SB_TILE), 1)

    def tile(kref, vref, base, nk, q, carry, acc, limit):
        kp = _load_perm(kref, base, nk).astype(BF16)
        vp = _load_perm(vref, base, nk).astype(BF16)
        zt = _dot_nt(kp, q) * scale
        wt, carry = _sb_scan(zt, carry, limit)
        return carry, acc + _dot_tn(vp, wt.astype(BF16))

    def qblock(m, _):
        row0 = pl.multiple_of(m * SB_TILE, SB_TILE)
        q = q_ref[pl.ds(row0, SB_TILE), :]
        carry0 = jnp.zeros((1, SB_TILE), F32)
        acc0 = jnp.zeros((SB_HD, SB_TILE), F32)
        carry, acc = tile(k_ref, v_ref, row0, SB_TILE, q, carry0, acc0, diag_limit)

        def cond(st):
            t, carry, _ = st
            return (t >= -1) & (jnp.max(carry) > SB_CUTOFF)

        def body(st):
            t, carry, acc = st
            carry, acc = lax.cond(
                t >= 0,
                lambda: tile(k_ref, v_ref, pl.multiple_of(t * SB_TILE, SB_TILE), SB_TILE, q, carry, acc, None),
                lambda: tile(kms_ref, vms_ref, 0, SB_MINI, q, carry, acc, N_META))
            return t - 1, carry, acc

        _, _, acc = lax.while_loop(cond, body, (m - 1, carry, acc))
        o = acc.T * sb_ref[pl.ds(row0, SB_TILE), :].astype(F32)
        o_ref[pl.ds(row0, SB_TILE), :] = o.astype(BF16)
        return 0

    lax.fori_loop(0, seq // SB_TILE, qblock, 0)


def _sb_prompt(p_main, k32, v32, km32, vm32, batch, seq):
    const_h = lambda b, h: (0, h)
    return pl.pallas_call(
        functools.partial(_sb_prompt_kernel, seq),
        grid=(batch, H_SB),
        in_specs=[
            pl.BlockSpec((seq, SB_HD), lambda b, h: (b, P_SQ // SB_HD + h)),
            pl.BlockSpec((seq, SB_HD), lambda b, h: (b, h)),
            pl.BlockSpec((seq, SB_HD), lambda b, h: (b, h)),
            pl.BlockSpec((N_META, SB_HD), const_h),
            pl.BlockSpec((N_META, SB_HD), const_h),
            pl.BlockSpec((seq, SB_HD), lambda b, h: (b, P_GB // SB_HD + h)),
        ],
        out_specs=pl.BlockSpec((seq, SB_HD), lambda b, h: (b, h)),
        out_shape=jax.ShapeDtypeStruct((batch * seq, SB_W), BF16),
        scratch_shapes=[pltpu.VMEM((SB_MINI, SB_HD), F32), pltpu.VMEM((SB_MINI, SB_HD), F32)],
        compiler_params=pltpu.CompilerParams(
            dimension_semantics=("arbitrary", "arbitrary"), vmem_limit_bytes=VMEM_LIMIT),
        name="sb_prompt",
    )(p_main, k32, v32, km32, vm32, p_main)


SB_GROUP = 4


def _sb_sample_kernel(t_new, past, q_ref, *refs):
    kn = refs[0:SB_GROUP]
    vn = refs[SB_GROUP:2 * SB_GROUP]
    kc = refs[2 * SB_GROUP:3 * SB_GROUP]
    vc = refs[3 * SB_GROUP:4 * SB_GROUP]
    sb_ref, o_ref = refs[4 * SB_GROUP], refs[4 * SB_GROUP + 1]
    gw = SB_GROUP * SB_HD
    nq = SB_GROUP * t_new
    scale = SB_HD ** -0.5
    q = q_ref[...].astype(F32)
    qt = jnp.concatenate([q] * SB_GROUP, axis=0).T
    rh = lax.broadcasted_iota(jnp.int32, (gw, nq), 0) // SB_HD
    ch = lax.broadcasted_iota(jnp.int32, (gw, nq), 1) // t_new
    qbd = jnp.where(rh == ch, qt, 0.0).astype(BF16)

    def tile(krefs, vrefs, base, nk, carry, acc, limit):
        kp = jnp.concatenate([_load_perm(r, base, nk) for r in krefs], axis=1).astype(BF16)
        vp = jnp.concatenate([_load_perm(r, base, nk) for r in vrefs], axis=1).astype(BF16)
        zt = _dot(kp, qbd) * scale
        wt, carry = _sb_scan(zt, carry, limit)
        return carry, acc + _dot_tn(vp, wt.astype(BF16))

    new_limit = lax.broadcasted_iota(jnp.int32, (SUBLANE, nq), 1) % t_new
    carry0 = jnp.zeros((1, nq), F32)
    acc0 = jnp.zeros((gw, nq), F32)
    carry, acc = tile(kn, vn, 0, t_new, carry0, acc0, new_limit)

    def cond(st):
        t, carry, _ = st
        return (t >= 0) & (jnp.max(carry) > SB_CUTOFF)

    def body(st):
        t, carry, acc = st
        carry, acc = tile(kc, vc, pl.multiple_of(t * SB_TILE, SB_TILE), SB_TILE, carry, acc, None)
        return t - 1, carry, acc

    _, _, acc = lax.while_loop(cond, body, (past // SB_TILE - 1, carry, acc))
    at = acc.T
    for h in range(SB_GROUP):
        cols = slice(h * SB_HD, (h + 1) * SB_HD)
        o = at[h * t_new:(h + 1) * t_new, cols] * sb_ref[:, cols].astype(F32)
        o_ref[:, cols] = o.astype(BF16)


def _sb_sample(p_main, k32, v32, cache_k, cache_v, row0, dec_batch, t_new, past):
    rb0 = row0 // t_new
    gw = SB_GROUP * SB_HD
    ngroups = H_SB // SB_GROUP
    new_spec = [pl.BlockSpec((t_new, SB_HD), functools.partial(lambda b, g, i: (rb0 + b, SB_GROUP * g + i), i=i))
                for i in range(SB_GROUP)]
    cache_spec = [pl.BlockSpec((None, past, SB_HD), functools.partial(lambda b, g, i: (b, 0, SB_GROUP * g + i), i=i))
                  for i in range(SB_GROUP)]
    return pl.pallas_call(
        functools.partial(_sb_sample_kernel, t_new, past),
        grid=(dec_batch, ngroups),
        in_specs=[pl.BlockSpec((t_new, gw), lambda b, g: (rb0 + b, P_SQ // gw + g))]
        + new_spec + new_spec + cache_spec + cache_spec
        + [pl.BlockSpec((t_new, gw), lambda b, g: (rb0 + b, P_GB // gw + g))],
        out_specs=pl.BlockSpec((t_new, gw), lambda b, g: (b, g)),
        out_shape=jax.ShapeDtypeStruct((dec_batch * t_new, SB_W), BF16),
        compiler_params=pltpu.CompilerParams(
            dimension_semantics=("arbitrary", "arbitrary"), vmem_limit_bytes=VMEM_LIMIT),
        name="sb_sample",
    )(p_main, *([k32] * SB_GROUP), *([v32] * SB_GROUP), *([cache_k] * SB_GROUP), *([cache_v] * SB_GROUP), p_main)


def _merge_kernel(n_prompt_tiles, ogp_ref, ogs_ref, obp_ref, obs_ref, xp_ref, xs_ref, w_ref, o_ref):
    m = pl.program_id(0)

    def run(og_ref, ob_ref, x_ref):
        mix = (og_ref[...].astype(F32) + ob_ref[...].astype(F32)).astype(BF16)
        o_ref[...] = x_ref[...] + _dot(mix, w_ref[...])

    @pl.when(m < n_prompt_tiles)
    def _():
        run(ogp_ref, obp_ref, xp_ref)

    @pl.when(m >= n_prompt_tiles)
    def _():
        run(ogs_ref, obs_ref, xs_ref)


def _merge(og_p, og_s, ob_p, ob_s, xp, xs, w_out):
    npt, nst = xp.shape[0] // TM, xs.shape[0] // TM
    pidx = lambda m: (jnp.minimum(m, npt - 1), 0)
    sidx = lambda m: (jnp.maximum(m - npt, 0), 0)
    row = lambda idx: pl.BlockSpec((TM, D_MODEL), idx)
    return pl.pallas_call(
        functools.partial(_merge_kernel, npt),
        grid=(npt + nst,),
        in_specs=[row(pidx), row(sidx), row(pidx), row(sidx), row(pidx), row(sidx),
                  pl.BlockSpec((D_MODEL, D_MODEL), lambda m: (0, 0))],
        out_specs=pl.BlockSpec((TM, D_MODEL), lambda m: (m, 0)),
        out_shape=jax.ShapeDtypeStruct((xp.shape[0] + xs.shape[0], D_MODEL), F32),
        compiler_params=pltpu.CompilerParams(
            dimension_semantics=("arbitrary",), vmem_limit_bytes=VMEM_LIMIT),
        name="merge_outproj",
    )(og_p, og_s, ob_p, ob_s, xp, xs, w_out)


def _ffn_kernel(n_prompt_tiles, x_ref, g2_ref, wu_ref, wd_ref, gf_ref, yp_ref, ys_ref, h_ref, acc_ref):
    m = pl.program_id(0)
    f = pl.program_id(1)

    @pl.when(f == 0)
    def _():
        h_ref[...] = _rms(x_ref[...], g2_ref[...]).astype(BF16)
        acc_ref[...] = jnp.zeros_like(acc_ref)

    u = jnp.maximum(_dot(h_ref[...], wu_ref[...]), 0.0)
    acc_ref[...] += _dot((u * u).astype(BF16), wd_ref[...])

    @pl.when(f == pl.num_programs(1) - 1)
    def _():
        y = _rms(x_ref[...] + acc_ref[...], gf_ref[...])

        @pl.when(m < n_prompt_tiles)
        def _():
            yp_ref[...] = y

        @pl.when(m >= n_prompt_tiles)
        def _():
            ys_ref[...] = y


def _ffn(x1, g2, w_up, w_down, gf, npr, nsr):
    npt, nst = npr // TM, nsr // TM
    return pl.pallas_call(
        functools.partial(_ffn_kernel, npt),
        grid=(npt + nst, D_FF // TF),
        in_specs=[
            pl.BlockSpec((TM, D_MODEL), lambda m, f: (m, 0)),
            pl.BlockSpec((1, D_MODEL), lambda m, f: (0, 0)),
            pl.BlockSpec((D_MODEL, TF), lambda m, f: (0, f)),
            pl.BlockSpec((TF, D_MODEL), lambda m, f: (f, 0)),
            pl.BlockSpec((1, D_MODEL), lambda m, f: (0, 0)),
        ],
        out_specs=[
            pl.BlockSpec((TM, D_MODEL), lambda m, f: (jnp.minimum(m, npt - 1), 0)),
            pl.BlockSpec((TM, D_MODEL), lambda m, f: (jnp.maximum(m - npt, 0), 0)),
        ],
        out_shape=[
            jax.ShapeDtypeStruct((npr, D_MODEL), F32),
            jax.ShapeDtypeStruct((nsr, D_MODEL), F32),
        ],
        scratch_shapes=[pltpu.VMEM((TM, D_MODEL), BF16), pltpu.VMEM((TM, D_MODEL), F32)],
        compiler_params=pltpu.CompilerParams(
            dimension_semantics=("arbitrary", "arbitrary"), vmem_limit_bytes=VMEM_LIMIT),
        name="ffn_final",
    )(x1, g2, w_up, w_down, gf)


def kernel(x_prompt, x_sample, cache_sb_k, cache_sb_v, state_gla, meta_tokens, norm1_g, w_in,
           w_alpha_up, b_alpha, gla_norm_g, w_out, norm2_g, w_up, w_down, norm_f_g):
    batch, seq, _ = x_prompt.shape
    dec_batch, t_new, _ = x_sample.shape
    depth, _, past = cache_sb_k.shape[:3]
    assert depth == 1 and w_in.shape[2] == _R_END
    assert seq % TM == 0 and (dec_batch * t_new) % TM == 0 and past % SB_TILE == 0
    assert t_new == SB_MINI and t_new % 16 == 0
    npr, nsr = batch * seq, dec_batch * t_new

    w = w_in[0]
    w_main = jnp.concatenate([w[:, :_R_ALOW], w[:, _R_SQ:]], axis=1).astype(BF16)
    w_alow = jnp.pad(w[:, _R_ALOW:_R_SQ], ((0, 0), (0, LANE - GLA_RANK))).astype(BF16)
    wup = jnp.pad(w_alpha_up[0], ((0, LANE - GLA_RANK), (0, 0)))
    wup_hi = wup.astype(BF16)
    wup_lo = (wup - wup_hi.astype(F32)).astype(BF16)
    bias = b_alpha[0].reshape(1, GLA_QK)
    gn = gla_norm_g[0].reshape(1, GLA_V)
    g1 = norm1_g[0].reshape(1, D_MODEL)
    g2 = norm2_g[0].reshape(1, D_MODEL)
    gf = norm_f_g.reshape(1, D_MODEL)
    w_out_b = w_out[0].astype(BF16)
    w_up_b = w_up[0].astype(BF16)
    w_down_b = w_down[0].astype(BF16)

    xp = x_prompt.reshape(npr, D_MODEL)
    xs = x_sample.reshape(nsr, D_MODEL)
    meta = meta_tokens.astype(x_prompt.dtype)

    p_main, p_meta, k32, v32, km32, vm32, alow, alow_meta = _inproj(xp, xs, meta, g1, w_main, w_alow)

    og_p, st_p = _gla_prompt(p_main, p_meta, alow, alow_meta, wup_hi, wup_lo, bias, gn, batch, seq)
    og_s, st_s = _gla_sample(p_main, alow, state_gla[0], wup_hi, wup_lo, bias, gn, npr, dec_batch, t_new)
    ob_p = _sb_prompt(p_main, k32, v32, km32, vm32, batch, seq)
    ck = cache_sb_k[0].reshape(dec_batch, past, SB_W)
    cv = cache_sb_v[0].reshape(dec_batch, past, SB_W)
    ob_s = _sb_sample(p_main, k32, v32, ck, cv, npr, dec_batch, t_new, past)

    x1 = _merge(og_p, og_s, ob_p, ob_s, xp, xs, w_out_b)
    y_p, y_s = _ffn(x1, g2, w_up_b, w_down_b, gf, npr, nsr)

    def with_meta(main, met):
        full = jnp.concatenate(
            [jnp.broadcast_to(met[None], (batch, N_META, SB_W)), main[:npr].reshape(batch, seq, SB_W)], axis=1)
        return full.reshape(1, batch, N_META + seq, H_SB, SB_HD)

    return (
        y_p.reshape(batch, seq, D_MODEL),
        y_s.reshape(dec_batch, t_new, D_MODEL),
        st_p[None],
        with_meta(k32, km32),
        with_meta(v32, vm32),
        st_s[None],
        k32[npr:].reshape(1, dec_batch, t_new, H_SB, SB_HD),
        v32[npr:].reshape(1, dec_batch, t_new, H_SB, SB_HD),
    )
```

```python
import functools

import jax
import jax.numpy as jnp
from jax import lax
from jax.experimental import pallas as pl
from jax.experimental.pallas import tpu as pltpu

F32 = jnp.float32
BF16 = jnp.bfloat16

D_MODEL = 2048
N_META = 16
H_GLA = 4
GLA_DK = 256
GLA_DV = 512
GLA_RANK = 16
GLA_TAU = 16.0
GLA_CHUNK = 64
H_SB = 16
SB_HD = 128
D_FF = 4 * D_MODEL
EPS = 1e-5
GLA_QK = H_GLA * GLA_DK
GLA_V = H_GLA * GLA_DV
SB_W = H_SB * SB_HD

_R_GK = GLA_QK
_R_GV = 2 * GLA_QK
_R_ALOW = _R_GV + GLA_V
_R_SQ = _R_ALOW + GLA_RANK
_R_SK = _R_SQ + SB_W
_R_SV = _R_SK + SB_W
_R_GA = _R_SV + SB_W
_R_GB = _R_GA + GLA_V
_R_END = _R_GB + SB_W

P_GQ = 0
P_GK = P_GQ + GLA_QK
P_GV = P_GK + GLA_QK
P_SQ = P_GV + GLA_V
P_SK = P_SQ + SB_W
P_SV = P_SK + SB_W
P_GA = P_SV + SB_W
P_GB = P_GA + GLA_V
P_W = P_GB + SB_W

LANE = 128
SUBLANE = 8
TM = 512
TN = 1024
TF = 512
GLA_ROWS = 256
SB_TILE = 256
SB_MINI = 32
SB_STRIDE = SB_MINI // SUBLANE
SB_CUTOFF = -104.0
VMEM_LIMIT = 56 * 1024 * 1024


def _dot(a, b):
    return jnp.dot(a, b, preferred_element_type=F32)


def _dot_nt(a, b):
    return lax.dot_general(a, b, (((1,), (1,)), ((), ())), preferred_element_type=F32)


def _dot_tn(a, b):
    return lax.dot_general(a, b, (((0,), (0,)), ((), ())), preferred_element_type=F32)


def _split2(x):
    hi = x.astype(BF16)
    lo = (x - hi.astype(F32)).astype(BF16)
    return hi, lo


def _rms(x, g):
    ms = jnp.mean(x * x, axis=-1, keepdims=True)
    return x * lax.rsqrt(ms + EPS) * g


def _log_sigmoid(x):
    return jnp.minimum(x, 0.0) - jnp.log(1.0 + jnp.exp(-jnp.abs(x)))


_N_SK0 = P_SK // TN
_N_SV0 = P_SV // TN
_N_GATE0 = P_GA // TN


HEADS_PER_TN = TN // SB_HD


def _inproj_kernel(n_prompt_tiles, batch, seq, xp_ref, xs_ref, meta_ref, g1_ref, w_ref, wal_ref,
                   p_ref, pm_ref, k32_ref, v32_ref, km_ref, vm_ref, al_ref, alm_ref,
                   kp_ref, vp_ref, ks_ref, vs_ref, h_ref, kstage_ref, vstage_ref, mstage_ref, sems, msem):
    m = pl.program_id(0)
    n = pl.program_id(1)
    n_tiles = pl.num_programs(0)
    tiles_per_batch = seq // TM
    n_head_groups = SB_W // TN

    def native_copy(stage_ref, sem, dst_ref, token):
        return pltpu.make_async_copy(stage_ref, dst_ref.at[pl.ds(token * H_SB, TM * H_SB), :], sem)

    def stage_heads(stage_ref, rows, vals, group):
        for h in range(HEADS_PER_TN):
            stage_ref[pl.ds(group * HEADS_PER_TN + h, rows, stride=H_SB), :] = vals[:, h * SB_HD:(h + 1) * SB_HD]

    def write_native(main, met, group, stage_ref, sem, dst_p, dst_s):
        if group == 0:
            @pl.when(m > 0)
            def _():
                native_copy(stage_ref, sem, dst_p, 0).wait()

        stage_heads(stage_ref, TM, main, group)

        @pl.when(m == 0)
        def _():
            stage_heads(mstage_ref, N_META, met, group)

        if group == n_head_groups - 1:
            @pl.when(m < n_prompt_tiles)
            def _():
                b = m // tiles_per_batch
                token = b * (seq + N_META) + N_META + (m - b * tiles_per_batch) * TM
                native_copy(stage_ref, sem, dst_p, token).start()

            @pl.when(m >= n_prompt_tiles)
            def _():
                native_copy(stage_ref, sem, dst_s, (m - n_prompt_tiles) * TM).start()

            @pl.when(m == 0)
            def _():
                copies = [pltpu.make_async_copy(
                    mstage_ref, dst_p.at[pl.ds(b * (seq + N_META) * H_SB, N_META * H_SB), :], msem)
                    for b in range(batch)]
                for cp in copies:
                    cp.start()
                for cp in copies:
                    cp.wait()

    @pl.when(n == 0)
    def _():
        g = g1_ref[...]

        @pl.when(m < n_prompt_tiles)
        def _():
            h_ref[0:TM, :] = _rms(xp_ref[...], g).astype(BF16)

        @pl.when(m >= n_prompt_tiles)
        def _():
            h_ref[0:TM, :] = _rms(xs_ref[...], g).astype(BF16)

        h_ref[TM:TM + N_META, :] = _rms(meta_ref[...], g).astype(BF16)
        al = _dot(h_ref[...], wal_ref[...])
        al_ref[...] = al[:TM]
        alm_ref[...] = al[TM:]

    acc = _dot(h_ref[...], w_ref[...])
    main = acc[:TM]
    met = acc[TM:]
    is_gate = n >= _N_GATE0
    first = m == 0

    @pl.when(jnp.logical_not(is_gate))
    def _():
        p_ref[...] = main.astype(BF16)

    @pl.when(is_gate)
    def _():
        p_ref[...] = jax.nn.sigmoid(main).astype(BF16)

    @pl.when(first)
    def _():
        pm_ref[...] = met.astype(BF16)

    is_k = (n >= _N_SK0) & (n < _N_SV0)
    is_v = (n >= _N_SV0) & (n < _N_GATE0)

    @pl.when(is_k)
    def _():
        k32_ref[...] = main

    @pl.when(is_v)
    def _():
        v32_ref[...] = main

    @pl.when(is_k & first)
    def _():
        km_ref[...] = met

    @pl.when(is_v & first)
    def _():
        vm_ref[...] = met

    for group in range(n_head_groups):
        @pl.when(n == _N_SK0 + group)
        def _():
            write_native(main, met, group, kstage_ref, sems.at[0], kp_ref, ks_ref)

        @pl.when(n == _N_SV0 + group)
        def _():
            write_native(main, met, group, vstage_ref, sems.at[1], vp_ref, vs_ref)

    @pl.when((m == n_tiles - 1) & (n == pl.num_programs(1) - 1))
    def _():
        native_copy(kstage_ref, sems.at[0], kp_ref, 0).wait()
        native_copy(vstage_ref, sems.at[1], vp_ref, 0).wait()


def _inproj(xp, xs, meta, g1, w_main, w_alow, batch, seq):
    npr, nsr = xp.shape[0], xs.shape[0]
    npt, nst = npr // TM, nsr // TM
    nt = npt + nst
    rows = npr + nsr
    nn = P_W // TN
    assert SB_W // TN == 2 and _N_SV0 == _N_SK0 + 2
    kcol = lambda n: jnp.clip(n - _N_SK0, 0, SB_W // TN - 1)
    vcol = lambda n: jnp.clip(n - _N_SV0, 0, SB_W // TN - 1)
    once = lambda m, col, last: jnp.where(m == 0, col, last)
    any_spec = pl.BlockSpec(memory_space=pl.ANY)
    return pl.pallas_call(
        functools.partial(_inproj_kernel, npt, batch, seq),
        grid=(nt, nn),
        in_specs=[
            pl.BlockSpec((TM, D_MODEL), lambda m, n: (jnp.minimum(m, npt - 1), 0)),
            pl.BlockSpec((TM, D_MODEL), lambda m, n: (jnp.maximum(m - npt, 0), 0)),
            pl.BlockSpec((N_META, D_MODEL), lambda m, n: (0, 0)),
            pl.BlockSpec((1, D_MODEL), lambda m, n: (0, 0)),
            pl.BlockSpec((D_MODEL, TN), lambda m, n: (0, n)),
            pl.BlockSpec((D_MODEL, LANE), lambda m, n: (0, 0)),
        ],
        out_specs=[
            pl.BlockSpec((TM, TN), lambda m, n: (m, n)),
            pl.BlockSpec((N_META, TN), lambda m, n: (0, once(m, n, nn - 1))),
            pl.BlockSpec((TM, TN), lambda m, n: (m, kcol(n))),
            pl.BlockSpec((TM, TN), lambda m, n: (m, vcol(n))),
            pl.BlockSpec((N_META, TN), lambda m, n: (0, once(m, kcol(n), SB_W // TN - 1))),
            pl.BlockSpec((N_META, TN), lambda m, n: (0, once(m, vcol(n), SB_W // TN - 1))),
            pl.BlockSpec((TM, LANE), lambda m, n: (m, 0)),
            pl.BlockSpec((N_META, LANE), lambda m, n: (0, 0)),
            any_spec, any_spec, any_spec, any_spec,
        ],
        out_shape=[
            jax.ShapeDtypeStruct((rows, P_W), BF16),
            jax.ShapeDtypeStruct((N_META, P_W), BF16),
            jax.ShapeDtypeStruct((rows, SB_W), F32),
            jax.ShapeDtypeStruct((rows, SB_W), F32),
            jax.ShapeDtypeStruct((N_META, SB_W), F32),
            jax.ShapeDtypeStruct((N_META, SB_W), F32),
            jax.ShapeDtypeStruct((rows, LANE), F32),
            jax.ShapeDtypeStruct((N_META, LANE), F32),
            jax.ShapeDtypeStruct((batch * (seq + N_META) * H_SB, SB_HD), F32),
            jax.ShapeDtypeStruct((batch * (seq + N_META) * H_SB, SB_HD), F32),
            jax.ShapeDtypeStruct((nsr * H_SB, SB_HD), F32),
            jax.ShapeDtypeStruct((nsr * H_SB, SB_HD), F32),
        ],
        scratch_shapes=[
            pltpu.VMEM((TM + N_META, D_MODEL), BF16),
            pltpu.VMEM((TM * H_SB, SB_HD), F32),
            pltpu.VMEM((TM * H_SB, SB_HD), F32),
            pltpu.VMEM((N_META * H_SB, SB_HD), F32),
            pltpu.SemaphoreType.DMA((2,)),
            pltpu.SemaphoreType.DMA(()),
        ],
        compiler_params=pltpu.CompilerParams(
            dimension_semantics=("arbitrary", "arbitrary"), vmem_limit_bytes=VMEM_LIMIT),
        name="inproj",
    )(xp, xs, meta, g1, w_main, w_alow)


def _gla_chunk(c, q, k, v, al, wup_hi, wup_lo, bias, st_ref):
    al_hi, al_lo = _split2(al)
    x = _dot(al_hi, wup_hi) + _dot(al_hi, wup_lo) + _dot(al_lo, wup_hi) + bias
    g = _log_sigmoid(x) * (1.0 / GLA_TAU)
    g_hi, g_lo = _split2(g)
    row = lax.broadcasted_iota(jnp.int32, (c, c), 0)
    col = lax.broadcasted_iota(jnp.int32, (c, c), 1)
    causal = col <= row
    tri = causal.astype(BF16)
    cum = _dot(tri, g_hi) + _dot(tri, g_lo)
    outs = []
    for h in range(H_GLA):
        ch = cum[:, h * GLA_DK:(h + 1) * GLA_DK]
        last = ch[c - 1:c, :]
        qh = q[:, h * GLA_DK:(h + 1) * GLA_DK].astype(F32)
        kh = k[:, h * GLA_DK:(h + 1) * GLA_DK].astype(F32)
        vh = v[:, h * GLA_DV:(h + 1) * GLA_DV]
        qd = (qh * jnp.exp(ch) * (GLA_DK ** -0.5)).astype(BF16)
        kd = (kh * jnp.exp(-ch)).astype(BF16)
        kr = (kh * jnp.exp(last - ch)).astype(BF16)
        att = jnp.where(causal, _dot_nt(qd, kd), 0.0).astype(BF16)
        st = st_ref[h]
        outs.append(_dot(att, vh) + _dot_nt(qd, st.astype(BF16)))
        st_ref[h] = st * jnp.exp(last) + _dot_tn(vh, kr)
    return outs


def _gla_finish(o, gn, sa):
    ms = jnp.mean(o * o, axis=-1, keepdims=True)
    return (o * lax.rsqrt(ms + EPS) * gn * sa.astype(F32)).astype(BF16)


def _gla_prompt_kernel(q_ref, k_ref, v_ref, sa_ref, al_ref, km_ref, vm_ref, alm_ref,
                       wh_ref, wl_ref, b_ref, gn_ref, og_ref, so_ref, st_ref):
    c_idx = pl.program_id(1)
    wup_hi = wh_ref[...]
    wup_lo = wl_ref[...]
    bias = b_ref[...]

    @pl.when(c_idx == 0)
    def _():
        st_ref[...] = jnp.zeros_like(st_ref)
        zq = jnp.zeros((N_META, GLA_QK), BF16)
        _gla_chunk(N_META, zq, km_ref[...], vm_ref[...], alm_ref[...], wup_hi, wup_lo, bias, st_ref)

    def body(i, carry):
        r0 = pl.multiple_of(i * GLA_CHUNK, GLA_CHUNK)
        rows = pl.ds(r0, GLA_CHUNK)
        outs = _gla_chunk(GLA_CHUNK, q_ref[rows, :], k_ref[rows, :], v_ref[rows, :], al_ref[rows, :],
                          wup_hi, wup_lo, bias, st_ref)
        for h in range(H_GLA):
            cols = slice(h * GLA_DV, (h + 1) * GLA_DV)
            og_ref[rows, cols] = _gla_finish(outs[h], gn_ref[:, cols], sa_ref[rows, cols])
        return carry

    lax.fori_loop(0, GLA_ROWS // GLA_CHUNK, body, 0)

    @pl.when(c_idx == pl.num_programs(1) - 1)
    def _():
        for h in range(H_GLA):
            so_ref[0, h] = st_ref[h].T


def _gla_prompt(p_main, p_meta, alow, alow_meta, wup_hi, wup_lo, bias, gn, batch, seq):
    nc = seq // GLA_ROWS
    rb = lambda b, c: b * nc + c
    const = lambda b, c: (0, 0)
    return pl.pallas_call(
        _gla_prompt_kernel,
        grid=(batch, nc),
        in_specs=[
            pl.BlockSpec((GLA_ROWS, GLA_QK), lambda b, c: (rb(b, c), P_GQ // GLA_QK)),
            pl.BlockSpec((GLA_ROWS, GLA_QK), lambda b, c: (rb(b, c), P_GK // GLA_QK)),
            pl.BlockSpec((GLA_ROWS, GLA_V), lambda b, c: (rb(b, c), P_GV // GLA_V)),
            pl.BlockSpec((GLA_ROWS, GLA_V), lambda b, c: (rb(b, c), P_GA // GLA_V)),
            pl.BlockSpec((GLA_ROWS, LANE), lambda b, c: (rb(b, c), 0)),
            pl.BlockSpec((N_META, GLA_QK), lambda b, c: (0, P_GK // GLA_QK)),
            pl.BlockSpec((N_META, GLA_V), lambda b, c: (0, P_GV // GLA_V)),
            pl.BlockSpec((N_META, LANE), const),
            pl.BlockSpec((LANE, GLA_QK), const),
            pl.BlockSpec((LANE, GLA_QK), const),
            pl.BlockSpec((1, GLA_QK), const),
            pl.BlockSpec((1, GLA_V), const),
        ],
        out_specs=[
            pl.BlockSpec((GLA_ROWS, GLA_V), lambda b, c: (rb(b, c), 0)),
            pl.BlockSpec((1, H_GLA, GLA_DK, GLA_DV), lambda b, c: (b, 0, 0, 0)),
        ],
        out_shape=[
            jax.ShapeDtypeStruct((batch * seq, GLA_V), BF16),
            jax.ShapeDtypeStruct((batch, H_GLA, GLA_DK, GLA_DV), F32),
        ],
        scratch_shapes=[pltpu.VMEM((H_GLA, GLA_DV, GLA_DK), F32)],
        compiler_params=pltpu.CompilerParams(
            dimension_semantics=("arbitrary", "arbitrary"), vmem_limit_bytes=VMEM_LIMIT),
        name="gla_prompt",
    )(p_main, p_main, p_main, p_main, alow, p_meta, p_meta, alow_meta, wup_hi, wup_lo, bias, gn)


def _gla_sample_kernel(t_new, q_ref, k_ref, v_ref, sa_ref, al_ref, s0_ref,
                       wh_ref, wl_ref, b_ref, gn_ref, og_ref, so_ref, st_ref):
    for h in range(H_GLA):
        st_ref[h] = s0_ref[0, h].T
    outs = _gla_chunk(t_new, q_ref[...], k_ref[...], v_ref[...], al_ref[...],
                      wh_ref[...], wl_ref[...], b_ref[...], st_ref)
    for h in range(H_GLA):
        cols = slice(h * GLA_DV, (h + 1) * GLA_DV)
        og_ref[:, cols] = _gla_finish(outs[h], gn_ref[:, cols], sa_ref[:, cols])
        so_ref[0, h] = st_ref[h].T


def _gla_sample(p_main, alow, state, wup_hi, wup_lo, bias, gn, row0, dec_batch, t_new):
    rb0 = row0 // t_new
    const = lambda b: (0, 0)
    return pl.pallas_call(
        functools.partial(_gla_sample_kernel, t_new),
        grid=(dec_batch,),
        in_specs=[
            pl.BlockSpec((t_new, GLA_QK), lambda b: (rb0 + b, P_GQ // GLA_QK)),
            pl.BlockSpec((t_new, GLA_QK), lambda b: (rb0 + b, P_GK // GLA_QK)),
            pl.BlockSpec((t_new, GLA_V), lambda b: (rb0 + b, P_GV // GLA_V)),
            pl.BlockSpec((t_new, GLA_V), lambda b: (rb0 + b, P_GA // GLA_V)),
            pl.BlockSpec((t_new, LANE), lambda b: (rb0 + b, 0)),
            pl.BlockSpec((1, H_GLA, GLA_DK, GLA_DV), lambda b: (b, 0, 0, 0)),
            pl.BlockSpec((LANE, GLA_QK), const),
            pl.BlockSpec((LANE, GLA_QK), const),
            pl.BlockSpec((1, GLA_QK), const),
            pl.BlockSpec((1, GLA_V), const),
        ],
        out_specs=[
            pl.BlockSpec((t_new, GLA_V), lambda b: (b, 0)),
            pl.BlockSpec((1, H_GLA, GLA_DK, GLA_DV), lambda b: (b, 0, 0, 0)),
        ],
        out_shape=[
            jax.ShapeDtypeStruct((dec_batch * t_new, GLA_V), BF16),
            jax.ShapeDtypeStruct((dec_batch, H_GLA, GLA_DK, GLA_DV), F32),
        ],
        scratch_shapes=[pltpu.VMEM((H_GLA, GLA_DV, GLA_DK), F32)],
        compiler_params=pltpu.CompilerParams(
            dimension_semantics=("arbitrary",), vmem_limit_bytes=VMEM_LIMIT),
        name="gla_sample",
    )(p_main, p_main, p_main, p_main, alow, state, wup_hi, wup_lo, bias, gn)


def _load_perm(ref, base, nk):
    parts = []
    for j in range(nk // SB_MINI):
        for c in range(SB_STRIDE):
            parts.append(ref[pl.ds(base + SB_MINI * j + c, SUBLANE, stride=SB_STRIDE), :])
    return jnp.concatenate(parts, axis=0)


def _sb_scan(zt, carry, limit):
    nk, nq = zt.shape
    rowid = lax.broadcasted_iota(jnp.int32, (SUBLANE, nq), 0)
    w_parts = [None] * (nk // SUBLANE)
    for j in reversed(range(nk // SB_MINI)):
        ls, ss, vis = [], [], []
        for c in range(SB_STRIDE):
            i = j * SB_STRIDE + c
            z = zt[i * SUBLANE:(i + 1) * SUBLANE]
            nz = -z
            t = jnp.log(1.0 + jnp.exp(jnp.minimum(z, nz)))
            l = jnp.minimum(nz, 0.0) - t
            ss.append(jnp.minimum(z, 0.0) - t)
            if limit is not None:
                v = rowid * SB_STRIDE + (SB_MINI * j + c) < limit
                l = jnp.where(v, l, 0.0)
                vis.append(v)
            ls.append(l)
        later = [None] * SB_STRIDE
        run = ls[SB_STRIDE - 1]
        for c in range(SB_STRIDE - 2, -1, -1):
            later[c] = run
            run = run + ls[c]
        incl = run
        for sh in (1, 2, 4):
            incl = incl + jnp.where(rowid < SUBLANE - sh, pltpu.roll(incl, SUBLANE - sh, axis=0), 0.0)
        off = carry + (incl - run)
        for c in range(SB_STRIDE):
            after = off if later[c] is None else off + later[c]
            w = jnp.exp(ss[c] + after)
            if limit is not None:
                w = jnp.where(vis[c], w, 0.0)
            w_parts[j * SB_STRIDE + c] = w
        carry = carry + incl[0:1, :]
    return jnp.concatenate(w_parts, axis=0), carry


SB_PAR = 2


def _sb_prompt_kernel(seq, q_ref, *refs):
    k_refs = refs[0:SB_PAR]
    v_refs = refs[SB_PAR:2 * SB_PAR]
    km_ref, vm_ref, sb_ref, o_ref, kms_ref, vms_ref = refs[2 * SB_PAR:]
    kms_ref[...] = jnp.zeros_like(kms_ref)
    vms_ref[...] = jnp.zeros_like(vms_ref)
    for i in range(SB_PAR):
        kms_ref[i, 0:N_META, :] = km_ref[:, i * SB_HD:(i + 1) * SB_HD]
        vms_ref[i, 0:N_META, :] = vm_ref[:, i * SB_HD:(i + 1) * SB_HD]
    scale = SB_HD ** -0.5
    heads = range(SB_PAR)
    diag_limit = lax.broadcasted_iota(jnp.int32, (SUBLANE, SB_TILE), 1)

    def tile(kref, vref, base, nk, q, carry, acc, limit):
        kp = _load_perm(kref, base, nk).astype(BF16)
        vp = _load_perm(vref, base, nk).astype(BF16)
        zt = _dot_nt(kp, q) * scale
        wt, carry = _sb_scan(zt, carry, limit)
        return carry, acc + _dot_tn(vp, wt.astype(BF16))

    def plain_tiles(t, qs, carries, accs):
        base = pl.multiple_of(t * SB_TILE, SB_TILE)
        res = [tile(k_refs[i], v_refs[i], base, SB_TILE, qs[i], carries[i], accs[i], None) for i in heads]
        return [r[0] for r in res], [r[1] for r in res]

    def meta_tiles(qs, carries, accs):
        res = [tile(kms_ref.at[i], vms_ref.at[i], 0, SB_MINI, qs[i], carries[i], accs[i], N_META) for i in heads]
        return [r[0] for r in res], [r[1] for r in res]

    def qblock(m, is_first):
        row0 = pl.multiple_of(m * SB_TILE, SB_TILE)
        rows = pl.ds(row0, SB_TILE)
        qs = [q_ref[rows, i * SB_HD:(i + 1) * SB_HD] for i in heads]
        carry0 = jnp.zeros((1, SB_TILE), F32)
        acc0 = jnp.zeros((SB_HD, SB_TILE), F32)
        res = [tile(k_refs[i], v_refs[i], row0, SB_TILE, qs[i], carry0, acc0, diag_limit) for i in heads]
        carries, accs = [r[0] for r in res], [r[1] for r in res]
        if is_first:
            _, accs = meta_tiles(qs, carries, accs)
        else:
            carries, accs = plain_tiles(m - 1, qs, carries, accs)

            def cond(st):
                t, carries, _ = st
                worst = functools.reduce(jnp.maximum, [jnp.max(c) for c in carries])
                return (t >= -1) & (worst > SB_CUTOFF)

            def body(st):
                t, carries, accs = st
                carries, accs = lax.cond(t >= 0,
                                         lambda: plain_tiles(t, qs, carries, accs),
                                         lambda: meta_tiles(qs, carries, accs))
                return t - 1, carries, accs

            _, _, accs = lax.while_loop(cond, body, (m - 2, carries, accs))
        for i in heads:
            cols = slice(i * SB_HD, (i + 1) * SB_HD)
            o = accs[i].T * sb_ref[rows, cols].astype(F32)
            o_ref[rows, cols] = o.astype(BF16)

    qblock(0, True)

    def loop_body(m, carry):
        qblock(m, False)
        return carry

    lax.fori_loop(1, seq // SB_TILE, loop_body, 0)


def _sb_prompt(p_main, k32, v32, km32, vm32, batch, seq):
    gw = SB_PAR * SB_HD
    head_spec = [pl.BlockSpec((seq, SB_HD), functools.partial(lambda b, g, i: (b, SB_PAR * g + i), i=i))
                 for i in range(SB_PAR)]
    return pl.pallas_call(
        functools.partial(_sb_prompt_kernel, seq),
        grid=(batch, H_SB // SB_PAR),
        in_specs=[pl.BlockSpec((seq, gw), lambda b, g: (b, P_SQ // gw + g))]
        + head_spec + head_spec
        + [pl.BlockSpec((N_META, gw), lambda b, g: (0, g)),
           pl.BlockSpec((N_META, gw), lambda b, g: (0, g)),
           pl.BlockSpec((seq, gw), lambda b, g: (b, P_GB // gw + g))],
        out_specs=pl.BlockSpec((seq, gw), lambda b, g: (b, g)),
        out_shape=jax.ShapeDtypeStruct((batch * seq, SB_W), BF16),
        scratch_shapes=[pltpu.VMEM((SB_PAR, SB_MINI, SB_HD), F32), pltpu.VMEM((SB_PAR, SB_MINI, SB_HD), F32)],
        compiler_params=pltpu.CompilerParams(
            dimension_semantics=("arbitrary", "arbitrary"), vmem_limit_bytes=VMEM_LIMIT),
        name="sb_prompt",
    )(p_main, *([k32] * SB_PAR), *([v32] * SB_PAR), km32, vm32, p_main)


SB_GROUP = 4


def _load_perm_native(ref, key0, nk, head):
    parts = []
    for j in range(nk // SB_MINI):
        for c in range(SB_STRIDE):
            start = (key0 + SB_MINI * j + c) * H_SB + head
            parts.append(ref[pl.ds(start, SUBLANE, stride=SB_STRIDE * H_SB), :])
    return jnp.concatenate(parts, axis=0)


def _sb_sample_kernel(t_new, past, q_ref, kn_ref, vn_ref, ck_ref, cv_ref, sb_ref, o_ref,
                      kbuf, vbuf, kfar, vfar, sems, far_sems):
    b = pl.program_id(0)
    nb = pl.num_programs(0)
    gw = SB_GROUP * SB_HD
    nq = SB_GROUP * t_new
    ngroups = H_SB // SB_GROUP
    n_tiles = past // SB_TILE
    tile_rows = SB_TILE * H_SB
    scale = SB_HD ** -0.5

    def near_copies(bb, slot):
        rows = pl.ds((n_tiles - 1) * tile_rows, tile_rows)
        return (pltpu.make_async_copy(ck_ref.at[bb, rows, :], kbuf.at[slot], sems.at[slot, 0]),
                pltpu.make_async_copy(cv_ref.at[bb, rows, :], vbuf.at[slot], sems.at[slot, 1]))

    @pl.when(b == 0)
    def _():
        for cp in near_copies(0, 0):
            cp.start()

    slot = b % 2

    @pl.when(b + 1 < nb)
    def _():
        for cp in near_copies(b + 1, 1 - slot):
            cp.start()

    for cp in near_copies(b, slot):
        cp.wait()

    rh = lax.broadcasted_iota(jnp.int32, (gw, nq), 0) // SB_HD
    ch = lax.broadcasted_iota(jnp.int32, (gw, nq), 1) // t_new
    head_match = rh == ch
    new_limit = lax.broadcasted_iota(jnp.int32, (SUBLANE, nq), 1) % t_new

    def tile(kref, vref, nk, g, qbd, carry, acc, limit):
        hs = range(SB_GROUP * g, SB_GROUP * (g + 1))
        kp = jnp.concatenate([_load_perm_native(kref, 0, nk, h) for h in hs], axis=1).astype(BF16)
        vp = jnp.concatenate([_load_perm_native(vref, 0, nk, h) for h in hs], axis=1).astype(BF16)
        zt = _dot(kp, qbd) * scale
        wt, carry = _sb_scan(zt, carry, limit)
        return carry, acc + _dot_tn(vp, wt.astype(BF16))

    qbds, carries, accs = [], [], []
    for g in range(ngroups):
        q = q_ref[:, g * gw:(g + 1) * gw].astype(F32)
        qt = jnp.concatenate([q] * SB_GROUP, axis=0).T
        qbd = jnp.where(head_match, qt, 0.0).astype(BF16)
        carry = jnp.zeros((1, nq), F32)
        acc = jnp.zeros((gw, nq), F32)
        carry, acc = tile(kn_ref, vn_ref, t_new, g, qbd, carry, acc, new_limit)
        carry, acc = tile(kbuf.at[slot], vbuf.at[slot], SB_TILE, g, qbd, carry, acc, None)
        qbds.append(qbd)
        carries.append(carry)
        accs.append(acc)

    def cond(st):
        t, carries, _ = st
        worst = functools.reduce(jnp.maximum, [jnp.max(c) for c in carries])
        return (t >= 0) & (worst > SB_CUTOFF)

    def body(st):
        t, carries, accs = st
        rows = pl.ds(pl.multiple_of(t * tile_rows, tile_rows), tile_rows)
        far = (pltpu.make_async_copy(ck_ref.at[b, rows, :], kfar, far_sems.at[0]),
               pltpu.make_async_copy(cv_ref.at[b, rows, :], vfar, far_sems.at[1]))
        for cp in far:
            cp.start()
        for cp in far:
            cp.wait()
        res = [tile(kfar, vfar, SB_TILE, g, qbds[g], carries[g], accs[g], None) for g in range(ngroups)]
        return t - 1, [r[0] for r in res], [r[1] for r in res]

    _, _, accs = lax.while_loop(cond, body, (n_tiles - 2, carries, accs))
    for g in range(ngroups):
        at = accs[g].T
        for h in range(SB_GROUP):
            cols = slice(g * gw + h * SB_HD, g * gw + (h + 1) * SB_HD)
            o = at[h * t_new:(h + 1) * t_new, h * SB_HD:(h + 1) * SB_HD] * sb_ref[:, cols].astype(F32)
            o_ref[:, cols] = o.astype(BF16)


def _sb_sample(p_main, k5s, v5s, cache_k, cache_v, row0, dec_batch, t_new, past):
    rb0 = row0 // t_new
    tile_rows = SB_TILE * H_SB
    any_spec = pl.BlockSpec(memory_space=pl.ANY)
    new_spec = pl.BlockSpec((t_new * H_SB, SB_HD), lambda b: (b, 0))
    return pl.pallas_call(
        functools.partial(_sb_sample_kernel, t_new, past),
        grid=(dec_batch,),
        in_specs=[pl.BlockSpec((t_new, SB_W), lambda b: (rb0 + b, P_SQ // SB_W)),
                  new_spec, new_spec, any_spec, any_spec,
                  pl.BlockSpec((t_new, SB_W), lambda b: (rb0 + b, P_GB // SB_W))],
        out_specs=pl.BlockSpec((t_new, SB_W), lambda b: (b, 0)),
        out_shape=jax.ShapeDtypeStruct((dec_batch * t_new, SB_W), BF16),
        scratch_shapes=[
            pltpu.VMEM((2, tile_rows, SB_HD), F32),
            pltpu.VMEM((2, tile_rows, SB_HD), F32),
            pltpu.VMEM((tile_rows, SB_HD), F32),
            pltpu.VMEM((tile_rows, SB_HD), F32),
            pltpu.SemaphoreType.DMA((2, 2)),
            pltpu.SemaphoreType.DMA((2,)),
        ],
        compiler_params=pltpu.CompilerParams(
            dimension_semantics=("arbitrary",), vmem_limit_bytes=VMEM_LIMIT),
        name="sb_sample",
    )(p_main, k5s, v5s, cache_k, cache_v, p_main)


def _merge_kernel(n_prompt_tiles, ogp_ref, ogs_ref, obp_ref, obs_ref, xp_ref, xs_ref, w_ref, o_ref):
    m = pl.program_id(0)

    def run(og_ref, ob_ref, x_ref):
        mix = (og_ref[...].astype(F32) + ob_ref[...].astype(F32)).astype(BF16)
        o_ref[...] = x_ref[...] + _dot(mix, w_ref[...])

    @pl.when(m < n_prompt_tiles)
    def _():
        run(ogp_ref, obp_ref, xp_ref)

    @pl.when(m >= n_prompt_tiles)
    def _():
        run(ogs_ref, obs_ref, xs_ref)


def _merge(og_p, og_s, ob_p, ob_s, xp, xs, w_out):
    npt, nst = xp.shape[0] // TM, xs.shape[0] // TM
    pidx = lambda m: (jnp.minimum(m, npt - 1), 0)
    sidx = lambda m: (jnp.maximum(m - npt, 0), 0)
    row = lambda idx: pl.BlockSpec((TM, D_MODEL), idx)
    return pl.pallas_call(
        functools.partial(_merge_kernel, npt),
        grid=(npt + nst,),
        in_specs=[row(pidx), row(sidx), row(pidx), row(sidx), row(pidx), row(sidx),
                  pl.BlockSpec((D_MODEL, D_MODEL), lambda m: (0, 0))],
        out_specs=pl.BlockSpec((TM, D_MODEL), lambda m: (m, 0)),
        out_shape=jax.ShapeDtypeStruct((xp.shape[0] + xs.shape[0], D_MODEL), F32),
        compiler_params=pltpu.CompilerParams(
            dimension_semantics=("arbitrary",), vmem_limit_bytes=VMEM_LIMIT),
        name="merge_outproj",
    )(og_p, og_s, ob_p, ob_s, xp, xs, w_out)


def _ffn_kernel(n_prompt_tiles, x_ref, g2_ref, wu_ref, wd_ref, gf_ref, yp_ref, ys_ref, h_ref, acc_ref):
    m = pl.program_id(0)
    f = pl.program_id(1)

    @pl.when(f == 0)
    def _():
        h_ref[...] = _rms(x_ref[...], g2_ref[...]).astype(BF16)
        acc_ref[...] = jnp.zeros_like(acc_ref)

    u = jnp.maximum(_dot(h_ref[...], wu_ref[...]), 0.0)
    acc_ref[...] += _dot((u * u).astype(BF16), wd_ref[...])

    @pl.when(f == pl.num_programs(1) - 1)
    def _():
        y = _rms(x_ref[...] + acc_ref[...], gf_ref[...])

        @pl.when(m < n_prompt_tiles)
        def _():
            yp_ref[...] = y

        @pl.when(m >= n_prompt_tiles)
        def _():
            ys_ref[...] = y


def _ffn(x1, g2, w_up, w_down, gf, npr, nsr):
    npt, nst = npr // TM, nsr // TM
    return pl.pallas_call(
        functools.partial(_ffn_kernel, npt),
        grid=(npt + nst, D_FF // TF),
        in_specs=[
            pl.BlockSpec((TM, D_MODEL), lambda m, f: (m, 0)),
            pl.BlockSpec((1, D_MODEL), lambda m, f: (0, 0)),
            pl.BlockSpec((D_MODEL, TF), lambda m, f: (0, f)),
            pl.BlockSpec((TF, D_MODEL), lambda m, f: (f, 0)),
            pl.BlockSpec((1, D_MODEL), lambda m, f: (0, 0)),
        ],
        out_specs=[
            pl.BlockSpec((TM, D_MODEL), lambda m, f: (jnp.minimum(m, npt - 1), 0)),
            pl.BlockSpec((TM, D_MODEL), lambda m, f: (jnp.maximum(m - npt, 0), 0)),
        ],
        out_shape=[
            jax.ShapeDtypeStruct((npr, D_MODEL), F32),
            jax.ShapeDtypeStruct((nsr, D_MODEL), F32),
        ],
        scratch_shapes=[pltpu.VMEM((TM, D_MODEL), BF16), pltpu.VMEM((TM, D_MODEL), F32)],
        compiler_params=pltpu.CompilerParams(
            dimension_semantics=("arbitrary", "arbitrary"), vmem_limit_bytes=VMEM_LIMIT),
        name="ffn_final",
    )(x1, g2, w_up, w_down, gf)


def kernel(x_prompt, x_sample, cache_sb_k, cache_sb_v, state_gla, meta_tokens, norm1_g, w_in,
           w_alpha_up, b_alpha, gla_norm_g, w_out, norm2_g, w_up, w_down, norm_f_g):
    batch, seq, _ = x_prompt.shape
    dec_batch, t_new, _ = x_sample.shape
    depth, _, past = cache_sb_k.shape[:3]
    assert depth == 1 and w_in.shape[2] == _R_END
    assert seq % TM == 0 and (dec_batch * t_new) % TM == 0 and past % SB_TILE == 0
    assert t_new == SB_MINI and t_new % 16 == 0
    npr, nsr = batch * seq, dec_batch * t_new

    w = w_in[0]
    w_main = jnp.concatenate([w[:, :_R_ALOW], w[:, _R_SQ:]], axis=1).astype(BF16)
    w_alow = jnp.pad(w[:, _R_ALOW:_R_SQ], ((0, 0), (0, LANE - GLA_RANK))).astype(BF16)
    wup = jnp.pad(w_alpha_up[0], ((0, LANE - GLA_RANK), (0, 0)))
    wup_hi = wup.astype(BF16)
    wup_lo = (wup - wup_hi.astype(F32)).astype(BF16)
    bias = b_alpha[0].reshape(1, GLA_QK)
    gn = gla_norm_g[0].reshape(1, GLA_V)
    g1 = norm1_g[0].reshape(1, D_MODEL)
    g2 = norm2_g[0].reshape(1, D_MODEL)
    gf = norm_f_g.reshape(1, D_MODEL)
    w_out_b = w_out[0].astype(BF16)
    w_up_b = w_up[0].astype(BF16)
    w_down_b = w_down[0].astype(BF16)

    xp = x_prompt.reshape(npr, D_MODEL)
    xs = x_sample.reshape(nsr, D_MODEL)
    meta = meta_tokens.astype(x_prompt.dtype)

    (p_main, p_meta, k32, v32, km32, vm32, alow, alow_meta,
     k5p, v5p, k5s, v5s) = _inproj(xp, xs, meta, g1, w_main, w_alow, batch, seq)

    og_p, st_p = _gla_prompt(p_main, p_meta, alow, alow_meta, wup_hi, wup_lo, bias, gn, batch, seq)
    og_s, st_s = _gla_sample(p_main, alow, state_gla[0], wup_hi, wup_lo, bias, gn, npr, dec_batch, t_new)
    ob_p = _sb_prompt(p_main, k32, v32, km32, vm32, batch, seq)
    ck = cache_sb_k[0].reshape(dec_batch, past * H_SB, SB_HD)
    cv = cache_sb_v[0].reshape(dec_batch, past * H_SB, SB_HD)
    ob_s = _sb_sample(p_main, k5s, v5s, ck, cv, npr, dec_batch, t_new, past)

    x1 = _merge(og_p, og_s, ob_p, ob_s, xp, xs, w_out_b)
    y_p, y_s = _ffn(x1, g2, w_up_b, w_down_b, gf, npr, nsr)

    return (
        y_p.reshape(batch, seq, D_MODEL),
        y_s.reshape(dec_batch, t_new, D_MODEL),
        st_p[None],
        k5p.reshape(1, batch, N_META + seq, H_SB, SB_HD),
        v5p.reshape(1, batch, N_META + seq, H_SB, SB_HD),
        st_s[None],
        k5s.reshape(1, dec_batch, t_new, H_SB, SB_HD),
        v5s.reshape(1, dec_batch, t_new, H_SB, SB_HD),
    )
```

```python
import functools

import jax
import jax.numpy as jnp
from jax import lax
from jax.experimental import pallas as pl
from jax.experimental.pallas import tpu as pltpu

F32 = jnp.float32
BF16 = jnp.bfloat16

D_MODEL = 2048
N_META = 16
H_GLA = 4
GLA_DK = 256
GLA_DV = 512
GLA_RANK = 16
GLA_TAU = 16.0
GLA_CHUNK = 64
H_SB = 16
SB_HD = 128
D_FF = 4 * D_MODEL
EPS = 1e-5
GLA_QK = H_GLA * GLA_DK
GLA_V = H_GLA * GLA_DV
SB_W = H_SB * SB_HD

_R_GK = GLA_QK
_R_GV = 2 * GLA_QK
_R_ALOW = _R_GV + GLA_V
_R_SQ = _R_ALOW + GLA_RANK
_R_SK = _R_SQ + SB_W
_R_SV = _R_SK + SB_W
_R_GA = _R_SV + SB_W
_R_GB = _R_GA + GLA_V
_R_END = _R_GB + SB_W

P_GQ = 0
P_GK = P_GQ + GLA_QK
P_GV = P_GK + GLA_QK
P_SQ = P_GV + GLA_V
P_SK = P_SQ + SB_W
P_SV = P_SK + SB_W
P_GA = P_SV + SB_W
P_GB = P_GA + GLA_V
P_W = P_GB + SB_W

LANE = 128
SUBLANE = 8
TM = 512
TN = 2048
TF = 1024
GLA_ROWS = 256
SB_TILE = 256
SB_MINI = 32
SB_STRIDE = SB_MINI // SUBLANE
SB_CUTOFF = -104.0
LOG2_E = 1.4426950408889634
SB_SCALE_LOG2 = SB_HD ** -0.5 * LOG2_E
SB_CUTOFF_LOG2 = SB_CUTOFF * LOG2_E
VMEM_LIMIT = 56 * 1024 * 1024


def _dot(a, b):
    return jnp.dot(a, b, preferred_element_type=F32)


def _dot_nt(a, b):
    return lax.dot_general(a, b, (((1,), (1,)), ((), ())), preferred_element_type=F32)


def _dot_tn(a, b):
    return lax.dot_general(a, b, (((0,), (0,)), ((), ())), preferred_element_type=F32)


def _split2(x):
    hi = x.astype(BF16)
    lo = (x - hi.astype(F32)).astype(BF16)
    return hi, lo


def _rms(x, g):
    ms = jnp.mean(x * x, axis=-1, keepdims=True)
    return x * lax.rsqrt(ms + EPS) * g


def _log_sigmoid(x):
    return jnp.minimum(x, 0.0) - jnp.log(1.0 + jnp.exp(-jnp.abs(x)))


_PREP_ROWS = 512
_PREP_COLS = 1024


def _prep_w_kernel(n_aligned, w_ref, edge_ref, o_ref):
    n = pl.program_id(1)

    @pl.when(n < n_aligned)
    def _():
        o_ref[...] = w_ref[...].astype(BF16)

    @pl.when(n >= n_aligned)
    def _():
        wide = jnp.concatenate([w_ref[...], edge_ref[...]], axis=1)
        width = _PREP_COLS + LANE
        o_ref[...] = pltpu.roll(wide, width - GLA_RANK, axis=1)[:, :_PREP_COLS].astype(BF16)


def _prep_w_in(w):
    n_aligned = _R_ALOW // _PREP_COLS
    lane_blocks = _PREP_COLS // LANE
    return pl.pallas_call(
        functools.partial(_prep_w_kernel, n_aligned),
        grid=(D_MODEL // _PREP_ROWS, P_W // _PREP_COLS),
        in_specs=[
            pl.BlockSpec((_PREP_ROWS, _PREP_COLS), lambda r, n: (r, n)),
            pl.BlockSpec((_PREP_ROWS, LANE), lambda r, n: (r, jnp.maximum(n, n_aligned) * lane_blocks + lane_blocks)),
        ],
        out_specs=pl.BlockSpec((_PREP_ROWS, _PREP_COLS), lambda r, n: (r, n)),
        out_shape=jax.ShapeDtypeStruct((D_MODEL, P_W), BF16),
        compiler_params=pltpu.CompilerParams(
            dimension_semantics=("arbitrary", "arbitrary"), vmem_limit_bytes=VMEM_LIMIT),
        name="prep_w_in",
    )(w, w)


_N_SK0 = P_SK // TN
_N_SV0 = P_SV // TN
_N_GATE0 = P_GA // TN


assert TN == SB_W


def _permute_rows(vals, perm_ref):
    rows = vals.shape[0]
    cols = []
    for cb in range(TN // LANE):
        slab = perm_ref.at[cb % 2]
        slab[...] = vals[:, cb * LANE:(cb + 1) * LANE]
        parts = [slab[pl.ds(SB_MINI * j + c, SUBLANE, stride=SB_STRIDE), :]
                 for j in range(rows // SB_MINI) for c in range(SB_STRIDE)]
        cols.append(jnp.concatenate(parts, axis=0).astype(BF16))
    return jnp.concatenate(cols, axis=1)


def _inproj_kernel(n_prompt_tiles, batch, seq, xp_ref, xs_ref, meta_ref, g1_ref, w_ref, wal_ref,
                   p_ref, pm_ref, km_ref, vm_ref, al_ref, alm_ref,
                   kp_ref, vp_ref, ks_ref, vs_ref,
                   h_ref, perm_ref, stage_ref, mstage_ref, sem, msem):
    m = pl.program_id(0)
    n = pl.program_id(1)
    n_tiles = pl.num_programs(0)
    tiles_per_batch = seq // TM

    def native_copy(dst_ref, token):
        return pltpu.make_async_copy(stage_ref, dst_ref.at[pl.ds(token * H_SB, TM * H_SB), :], sem)

    def stage_heads(ref, rows, vals):
        for h in range(H_SB):
            ref[pl.ds(h, rows, stride=H_SB), :] = vals[:, h * SB_HD:(h + 1) * SB_HD]

    def write_native(main, met, dst_p, dst_s, pending):
        @pl.when(pending)
        def _():
            native_copy(dst_p, 0).wait()

        stage_heads(stage_ref, TM, main)

        @pl.when(m < n_prompt_tiles)
        def _():
            b = m // tiles_per_batch
            token = b * (seq + N_META) + N_META + (m - b * tiles_per_batch) * TM
            native_copy(dst_p, token).start()

        @pl.when(m >= n_prompt_tiles)
        def _():
            native_copy(dst_s, (m - n_prompt_tiles) * TM).start()

        @pl.when(m == 0)
        def _():
            stage_heads(mstage_ref, N_META, met)
            copies = [pltpu.make_async_copy(
                mstage_ref, dst_p.at[pl.ds(b * (seq + N_META) * H_SB, N_META * H_SB), :], msem)
                for b in range(batch)]
            for cp in copies:
                cp.start()
            for cp in copies:
                cp.wait()

    @pl.when(n == 0)
    def _():
        g = g1_ref[...]

        @pl.when(m < n_prompt_tiles)
        def _():
            h_ref[0:TM, :] = _rms(xp_ref[...], g).astype(BF16)

        @pl.when(m >= n_prompt_tiles)
        def _():
            h_ref[0:TM, :] = _rms(xs_ref[...], g).astype(BF16)

        h_ref[TM:TM + N_META, :] = _rms(meta_ref[...], g).astype(BF16)
        al = _dot(h_ref[...], wal_ref[...])
        al_ref[...] = al[:TM]
        alm_ref[...] = al[TM:]

    acc = _dot(h_ref[...], w_ref[...])
    main = acc[:TM]
    met = acc[TM:]
    is_gate = n >= _N_GATE0
    first = m == 0
    is_k = (n >= _N_SK0) & (n < _N_SV0)
    is_v = (n >= _N_SV0) & (n < _N_GATE0)
    is_kv = is_k | is_v

    @pl.when(jnp.logical_not(is_gate | is_kv))
    def _():
        p_ref[...] = main.astype(BF16)

    @pl.when(is_kv)
    def _():
        p_ref[...] = _permute_rows(main, perm_ref)

    @pl.when(is_gate)
    def _():
        p_ref[...] = jax.nn.sigmoid(main).astype(BF16)

    @pl.when(first)
    def _():
        pm_ref[...] = met.astype(BF16)

    @pl.when(is_k & first)
    def _():
        km_ref[...] = met

    @pl.when(is_v & first)
    def _():
        vm_ref[...] = met

    @pl.when(is_k)
    def _():
        write_native(main, met, kp_ref, ks_ref, m > 0)

    @pl.when(is_v)
    def _():
        write_native(main, met, vp_ref, vs_ref, True)

    @pl.when((m == n_tiles - 1) & (n == pl.num_programs(1) - 1))
    def _():
        native_copy(vp_ref, 0).wait()


def _inproj(xp, xs, meta, g1, w_main, w_alow, batch, seq):
    npr, nsr = xp.shape[0], xs.shape[0]
    npt, nst = npr // TM, nsr // TM
    nt = npt + nst
    rows = npr + nsr
    nn = P_W // TN
    kcol = lambda n: jnp.clip(n - _N_SK0, 0, SB_W // TN - 1)
    vcol = lambda n: jnp.clip(n - _N_SV0, 0, SB_W // TN - 1)
    once = lambda m, col, last: jnp.where(m == 0, col, last)
    any_spec = pl.BlockSpec(memory_space=pl.ANY)
    return pl.pallas_call(
        functools.partial(_inproj_kernel, npt, batch, seq),
        grid=(nt, nn),
        in_specs=[
            pl.BlockSpec((TM, D_MODEL), lambda m, n: (jnp.minimum(m, npt - 1), 0)),
            pl.BlockSpec((TM, D_MODEL), lambda m, n: (jnp.maximum(m - npt, 0), 0),
                         pipeline_mode=pl.Buffered(1)),
            pl.BlockSpec((N_META, D_MODEL), lambda m, n: (0, 0)),
            pl.BlockSpec((1, D_MODEL), lambda m, n: (0, 0)),
            pl.BlockSpec((D_MODEL, TN), lambda m, n: (0, n)),
            pl.BlockSpec((D_MODEL, LANE), lambda m, n: (0, 0)),
        ],
        out_specs=[
            pl.BlockSpec((TM, TN), lambda m, n: (m, n)),
            pl.BlockSpec((N_META, TN), lambda m, n: (0, once(m, n, nn - 1))),
            pl.BlockSpec((N_META, TN), lambda m, n: (0, once(m, kcol(n), SB_W // TN - 1))),
            pl.BlockSpec((N_META, TN), lambda m, n: (0, once(m, vcol(n), SB_W // TN - 1))),
            pl.BlockSpec((TM, LANE), lambda m, n: (m, 0)),
            pl.BlockSpec((N_META, LANE), lambda m, n: (0, 0)),
            any_spec, any_spec, any_spec, any_spec,
        ],
        out_shape=[
            jax.ShapeDtypeStruct((rows, P_W), BF16),
            jax.ShapeDtypeStruct((N_META, P_W), BF16),
            jax.ShapeDtypeStruct((N_META, SB_W), F32),
            jax.ShapeDtypeStruct((N_META, SB_W), F32),
            jax.ShapeDtypeStruct((rows, LANE), F32),
            jax.ShapeDtypeStruct((N_META, LANE), F32),
            jax.ShapeDtypeStruct((batch * (seq + N_META) * H_SB, SB_HD), F32),
            jax.ShapeDtypeStruct((batch * (seq + N_META) * H_SB, SB_HD), F32),
            jax.ShapeDtypeStruct((nsr * H_SB, SB_HD), F32),
            jax.ShapeDtypeStruct((nsr * H_SB, SB_HD), F32),
        ],
        scratch_shapes=[
            pltpu.VMEM((TM + N_META, D_MODEL), BF16),
            pltpu.VMEM((2, TM, LANE), F32),
            pltpu.VMEM((TM * H_SB, SB_HD), F32),
            pltpu.VMEM((N_META * H_SB, SB_HD), F32),
            pltpu.SemaphoreType.DMA(()),
            pltpu.SemaphoreType.DMA(()),
        ],
        compiler_params=pltpu.CompilerParams(
            dimension_semantics=("arbitrary", "arbitrary"), vmem_limit_bytes=VMEM_LIMIT),
        name="inproj",
    )(xp, xs, meta, g1, w_main, w_alow)


def _gla_chunk(c, q, k, v, al, wup_hi, wup_lo, bias, st_ref):
    al_hi, al_lo = _split2(al)
    x = _dot(al_hi, wup_hi) + _dot(al_hi, wup_lo) + _dot(al_lo, wup_hi) + bias
    g = _log_sigmoid(x) * (1.0 / GLA_TAU)
    g_hi, g_lo = _split2(g)
    row = lax.broadcasted_iota(jnp.int32, (c, c), 0)
    col = lax.broadcasted_iota(jnp.int32, (c, c), 1)
    causal = col <= row
    tri = causal.astype(BF16)
    cum = _dot(tri, g_hi) + _dot(tri, g_lo)
    outs = []
    for h in range(H_GLA):
        ch = cum[:, h * GLA_DK:(h + 1) * GLA_DK]
        last = ch[c - 1:c, :]
        qh = q[:, h * GLA_DK:(h + 1) * GLA_DK].astype(F32)
        kh = k[:, h * GLA_DK:(h + 1) * GLA_DK].astype(F32)
        vh = v[:, h * GLA_DV:(h + 1) * GLA_DV]
        qd = (qh * jnp.exp(ch) * (GLA_DK ** -0.5)).astype(BF16)
        kd = (kh * jnp.exp(-ch)).astype(BF16)
        kr = (kh * jnp.exp(last - ch)).astype(BF16)
        att = jnp.where(causal, _dot_nt(qd, kd), 0.0).astype(BF16)
        st = st_ref[h]
        outs.append(_dot(att, vh) + _dot_nt(qd, st.astype(BF16)))
        st_ref[h] = st * jnp.exp(last) + _dot_tn(vh, kr)
    return outs


def _gla_finish(o, gn, sa):
    ms = jnp.mean(o * o, axis=-1, keepdims=True)
    return (o * lax.rsqrt(ms + EPS) * gn * sa.astype(F32)).astype(BF16)


def _gla_prompt_kernel(q_ref, k_ref, v_ref, sa_ref, al_ref, km_ref, vm_ref, alm_ref,
                       wh_ref, wl_ref, b_ref, gn_ref, og_ref, so_ref, st_ref):
    c_idx = pl.program_id(1)
    wup_hi = wh_ref[...]
    wup_lo = wl_ref[...]
    bias = b_ref[...]

    @pl.when(c_idx == 0)
    def _():
        st_ref[...] = jnp.zeros_like(st_ref)
        zq = jnp.zeros((N_META, GLA_QK), BF16)
        _gla_chunk(N_META, zq, km_ref[...], vm_ref[...], alm_ref[...], wup_hi, wup_lo, bias, st_ref)

    def body(i, carry):
        r0 = pl.multiple_of(i * GLA_CHUNK, GLA_CHUNK)
        rows = pl.ds(r0, GLA_CHUNK)
        outs = _gla_chunk(GLA_CHUNK, q_ref[rows, :], k_ref[rows, :], v_ref[rows, :], al_ref[rows, :],
                          wup_hi, wup_lo, bias, st_ref)
        for h in range(H_GLA):
            cols = slice(h * GLA_DV, (h + 1) * GLA_DV)
            og_ref[rows, cols] = _gla_finish(outs[h], gn_ref[:, cols], sa_ref[rows, cols])
        return carry

    lax.fori_loop(0, GLA_ROWS // GLA_CHUNK, body, 0)

    @pl.when(c_idx == pl.num_programs(1) - 1)
    def _():
        for h in range(H_GLA):
            so_ref[0, h] = st_ref[h].T


def _gla_prompt(p_main, p_meta, alow, alow_meta, wup_hi, wup_lo, bias, gn, batch, seq):
    nc = seq // GLA_ROWS
    rb = lambda b, c: b * nc + c
    const = lambda b, c: (0, 0)
    return pl.pallas_call(
        _gla_prompt_kernel,
        grid=(batch, nc),
        in_specs=[
            pl.BlockSpec((GLA_ROWS, GLA_QK), lambda b, c: (rb(b, c), P_GQ // GLA_QK)),
            pl.BlockSpec((GLA_ROWS, GLA_QK), lambda b, c: (rb(b, c), P_GK // GLA_QK)),
            pl.BlockSpec((GLA_ROWS, GLA_V), lambda b, c: (rb(b, c), P_GV // GLA_V)),
            pl.BlockSpec((GLA_ROWS, GLA_V), lambda b, c: (rb(b, c), P_GA // GLA_V)),
            pl.BlockSpec((GLA_ROWS, LANE), lambda b, c: (rb(b, c), 0)),
            pl.BlockSpec((N_META, GLA_QK), lambda b, c: (0, P_GK // GLA_QK)),
            pl.BlockSpec((N_META, GLA_V), lambda b, c: (0, P_GV // GLA_V)),
            pl.BlockSpec((N_META, LANE), const),
            pl.BlockSpec((LANE, GLA_QK), const),
            pl.BlockSpec((LANE, GLA_QK), const),
            pl.BlockSpec((1, GLA_QK), const),
            pl.BlockSpec((1, GLA_V), const),
        ],
        out_specs=[
            pl.BlockSpec((GLA_ROWS, GLA_V), lambda b, c: (rb(b, c), 0)),
            pl.BlockSpec((1, H_GLA, GLA_DK, GLA_DV), lambda b, c: (b, 0, 0, 0)),
        ],
        out_shape=[
            jax.ShapeDtypeStruct((batch * seq, GLA_V), BF16),
            jax.ShapeDtypeStruct((batch, H_GLA, GLA_DK, GLA_DV), F32),
        ],
        scratch_shapes=[pltpu.VMEM((H_GLA, GLA_DV, GLA_DK), F32)],
        compiler_params=pltpu.CompilerParams(
            dimension_semantics=("arbitrary", "arbitrary"), vmem_limit_bytes=VMEM_LIMIT),
        name="gla_prompt",
    )(p_main, p_main, p_main, p_main, alow, p_meta, p_meta, alow_meta, wup_hi, wup_lo, bias, gn)


def _gla_sample_kernel(t_new, q_ref, k_ref, v_ref, sa_ref, al_ref, s0_ref,
                       wh_ref, wl_ref, b_ref, gn_ref, og_ref, so_ref, st_ref):
    for h in range(H_GLA):
        st_ref[h] = s0_ref[0, h].T
    outs = _gla_chunk(t_new, q_ref[...], k_ref[...], v_ref[...], al_ref[...],
                      wh_ref[...], wl_ref[...], b_ref[...], st_ref)
    for h in range(H_GLA):
        cols = slice(h * GLA_DV, (h + 1) * GLA_DV)
        og_ref[:, cols] = _gla_finish(outs[h], gn_ref[:, cols], sa_ref[:, cols])
        so_ref[0, h] = st_ref[h].T


def _gla_sample(p_main, alow, state, wup_hi, wup_lo, bias, gn, row0, dec_batch, t_new):
    rb0 = row0 // t_new
    const = lambda b: (0, 0)
    return pl.pallas_call(
        functools.partial(_gla_sample_kernel, t_new),
        grid=(dec_batch,),
        in_specs=[
            pl.BlockSpec((t_new, GLA_QK), lambda b: (rb0 + b, P_GQ // GLA_QK)),
            pl.BlockSpec((t_new, GLA_QK), lambda b: (rb0 + b, P_GK // GLA_QK)),
            pl.BlockSpec((t_new, GLA_V), lambda b: (rb0 + b, P_GV // GLA_V)),
            pl.BlockSpec((t_new, GLA_V), lambda b: (rb0 + b, P_GA // GLA_V)),
            pl.BlockSpec((t_new, LANE), lambda b: (rb0 + b, 0)),
            pl.BlockSpec((1, H_GLA, GLA_DK, GLA_DV), lambda b: (b, 0, 0, 0)),
            pl.BlockSpec((LANE, GLA_QK), const),
            pl.BlockSpec((LANE, GLA_QK), const),
            pl.BlockSpec((1, GLA_QK), const),
            pl.BlockSpec((1, GLA_V), const),
        ],
        out_specs=[
            pl.BlockSpec((t_new, GLA_V), lambda b: (b, 0)),
            pl.BlockSpec((1, H_GLA, GLA_DK, GLA_DV), lambda b: (b, 0, 0, 0)),
        ],
        out_shape=[
            jax.ShapeDtypeStruct((dec_batch * t_new, GLA_V), BF16),
            jax.ShapeDtypeStruct((dec_batch, H_GLA, GLA_DK, GLA_DV), F32),
        ],
        scratch_shapes=[pltpu.VMEM((H_GLA, GLA_DV, GLA_DK), F32)],
        compiler_params=pltpu.CompilerParams(
            dimension_semantics=("arbitrary",), vmem_limit_bytes=VMEM_LIMIT),
        name="gla_sample",
    )(p_main, p_main, p_main, p_main, alow, state, wup_hi, wup_lo, bias, gn)


def _load_perm(ref, base, nk):
    parts = []
    for j in range(nk // SB_MINI):
        for c in range(SB_STRIDE):
            parts.append(ref[pl.ds(base + SB_MINI * j + c, SUBLANE, stride=SB_STRIDE), :])
    return jnp.concatenate(parts, axis=0)


def _sb_scan(zt, carry, limit):
    nk, nq = zt.shape
    rowid = lax.broadcasted_iota(jnp.int32, (SUBLANE, nq), 0)
    w_parts = [None] * (nk // SUBLANE)
    for j in reversed(range(nk // SB_MINI)):
        ls, ss, vis = [], [], []
        for c in range(SB_STRIDE):
            i = j * SB_STRIDE + c
            z = zt[i * SUBLANE:(i + 1) * SUBLANE]
            nz = -z
            t = jnp.log2(1.0 + jnp.exp2(jnp.minimum(z, nz)))
            l = jnp.minimum(nz, 0.0) - t
            ss.append(l + z)
            if limit is not None:
                v = rowid * SB_STRIDE + (SB_MINI * j + c) < limit
                l = jnp.where(v, l, 0.0)
                vis.append(v)
            ls.append(l)
        later = [None] * SB_STRIDE
        run = ls[SB_STRIDE - 1]
        for c in range(SB_STRIDE - 2, -1, -1):
            later[c] = run
            run = run + ls[c]
        incl = run
        for sh in (1, 2, 4):
            incl = incl + jnp.where(rowid < SUBLANE - sh, pltpu.roll(incl, SUBLANE - sh, axis=0), 0.0)
        off = carry + (incl - run)
        for c in range(SB_STRIDE):
            after = off if later[c] is None else off + later[c]
            w = jnp.exp2(ss[c] + after)
            if limit is not None:
                w = jnp.where(vis[c], w, 0.0)
            w_parts[j * SB_STRIDE + c] = w
        carry = carry + incl[0:1, :]
    return jnp.concatenate(w_parts, axis=0), carry


SB_PAR = 4


def _sb_prompt_kernel(seq, q_ref, k_ref, v_ref, km_ref, vm_ref, sb_ref, o_ref, kms_ref, vms_ref):
    kms_ref[...] = jnp.zeros_like(kms_ref)
    vms_ref[...] = jnp.zeros_like(vms_ref)
    for i in range(SB_PAR):
        kms_ref[i, 0:N_META, :] = km_ref[:, i * SB_HD:(i + 1) * SB_HD]
        vms_ref[i, 0:N_META, :] = vm_ref[:, i * SB_HD:(i + 1) * SB_HD]
    heads = range(SB_PAR)
    diag_limit = lax.broadcasted_iota(jnp.int32, (SUBLANE, SB_TILE), 1)

    def tile(kp, vp, q, carry, acc, limit):
        zt = _dot_nt(kp, q) * SB_SCALE_LOG2
        wt, carry = _sb_scan(zt, carry, limit)
        return carry, acc + _dot_tn(vp, wt.astype(BF16))

    def prompt_tiles(base, qs, carries, accs, limit):
        rows = pl.ds(base, SB_TILE)
        res = [tile(k_ref[rows, i * SB_HD:(i + 1) * SB_HD], v_ref[rows, i * SB_HD:(i + 1) * SB_HD],
                    qs[i], carries[i], accs[i], limit) for i in heads]
        return [r[0] for r in res], [r[1] for r in res]

    def plain_tiles(t, qs, carries, accs):
        return prompt_tiles(pl.multiple_of(t * SB_TILE, SB_TILE), qs, carries, accs, None)

    def meta_tiles(qs, carries, accs):
        res = [tile(_load_perm(kms_ref.at[i], 0, SB_MINI).astype(BF16),
                    _load_perm(vms_ref.at[i], 0, SB_MINI).astype(BF16),
                    qs[i], carries[i], accs[i], N_META) for i in heads]
        return [r[0] for r in res], [r[1] for r in res]

    def qblock(m, is_first):
        row0 = pl.multiple_of(m * SB_TILE, SB_TILE)
        rows = pl.ds(row0, SB_TILE)
        qs = [q_ref[rows, i * SB_HD:(i + 1) * SB_HD] for i in heads]
        carry0 = [jnp.zeros((1, SB_TILE), F32)] * SB_PAR
        acc0 = [jnp.zeros((SB_HD, SB_TILE), F32)] * SB_PAR
        carries, accs = prompt_tiles(row0, qs, carry0, acc0, diag_limit)
        if is_first:
            _, accs = meta_tiles(qs, carries, accs)
        else:
            carries, accs = plain_tiles(m - 1, qs, carries, accs)

            def cond(st):
                t, carries, _ = st
                worst = functools.reduce(jnp.maximum, [jnp.max(c) for c in carries])
                return (t >= -1) & (worst > SB_CUTOFF_LOG2)

            def body(st):
                t, carries, accs = st
                carries, accs = lax.cond(t >= 0,
                                         lambda: plain_tiles(t, qs, carries, accs),
                                         lambda: meta_tiles(qs, carries, accs))
                return t - 1, carries, accs

            _, _, accs = lax.while_loop(cond, body, (m - 2, carries, accs))
        for i in heads:
            cols = slice(i * SB_HD, (i + 1) * SB_HD)
            o = accs[i].T * sb_ref[rows, cols].astype(F32)
            o_ref[rows, cols] = o.astype(BF16)

    qblock(0, True)

    def loop_body(m, carry):
        qblock(m, False)
        return carry

    lax.fori_loop(1, seq // SB_TILE, loop_body, 0)


def _sb_prompt(p_main, km32, vm32, batch, seq):
    gw = SB_PAR * SB_HD
    col = lambda off: pl.BlockSpec((seq, gw), lambda b, g: (b, off // gw + g))
    return pl.pallas_call(
        functools.partial(_sb_prompt_kernel, seq),
        grid=(batch, H_SB // SB_PAR),
        in_specs=[col(P_SQ), col(P_SK), col(P_SV),
                  pl.BlockSpec((N_META, gw), lambda b, g: (0, g)),
                  pl.BlockSpec((N_META, gw), lambda b, g: (0, g)),
                  col(P_GB)],
        out_specs=pl.BlockSpec((seq, gw), lambda b, g: (b, g)),
        out_shape=jax.ShapeDtypeStruct((batch * seq, SB_W), BF16),
        scratch_shapes=[pltpu.VMEM((SB_PAR, SB_MINI, SB_HD), F32), pltpu.VMEM((SB_PAR, SB_MINI, SB_HD), F32)],
        compiler_params=pltpu.CompilerParams(
            dimension_semantics=("arbitrary", "arbitrary"), vmem_limit_bytes=VMEM_LIMIT),
        name="sb_prompt",
    )(p_main, p_main, p_main, km32, vm32, p_main)


SB_GROUP = 4


def _load_perm_native(ref, key0, nk, head):
    parts = []
    for j in range(nk // SB_MINI):
        for c in range(SB_STRIDE):
            start = (key0 + SB_MINI * j + c) * H_SB + head
            parts.append(ref[pl.ds(start, SUBLANE, stride=SB_STRIDE * H_SB), :])
    return jnp.concatenate(parts, axis=0)


def _sb_sample_kernel(t_new, past, q_ref, kn_ref, vn_ref, ck_ref, cv_ref, sb_ref, o_ref,
                      kbuf, vbuf, kfar, vfar, sems, far_sems):
    b = pl.program_id(0)
    nb = pl.num_programs(0)
    gw = SB_GROUP * SB_HD
    nq = SB_GROUP * t_new
    ngroups = H_SB // SB_GROUP
    n_tiles = past // SB_TILE
    tile_rows = SB_TILE * H_SB

    def near_copies(bb, slot):
        rows = pl.ds((n_tiles - 1) * tile_rows, tile_rows)
        return (pltpu.make_async_copy(ck_ref.at[bb, rows, :], kbuf.at[slot], sems.at[slot, 0]),
                pltpu.make_async_copy(cv_ref.at[bb, rows, :], vbuf.at[slot], sems.at[slot, 1]))

    @pl.when(b == 0)
    def _():
        for cp in near_copies(0, 0):
            cp.start()

    slot = b % 2

    @pl.when(b + 1 < nb)
    def _():
        for cp in near_copies(b + 1, 1 - slot):
            cp.start()

    for cp in near_copies(b, slot):
        cp.wait()

    rh = lax.broadcasted_iota(jnp.int32, (gw, nq), 0) // SB_HD
    ch = lax.broadcasted_iota(jnp.int32, (gw, nq), 1) // t_new
    head_match = rh == ch
    new_limit = lax.broadcasted_iota(jnp.int32, (SUBLANE, nq), 1) % t_new

    def tile(kref, vref, nk, g, qbd, carry, acc, limit):
        hs = range(SB_GROUP * g, SB_GROUP * (g + 1))
        kp = jnp.concatenate([_load_perm_native(kref, 0, nk, h) for h in hs], axis=1).astype(BF16)
        vp = jnp.concatenate([_load_perm_native(vref, 0, nk, h) for h in hs], axis=1).astype(BF16)
        zt = _dot(kp, qbd) * SB_SCALE_LOG2
        wt, carry = _sb_scan(zt, carry, limit)
        return carry, acc + _dot_tn(vp, wt.astype(BF16))

    qbds, carries, accs = [], [], []
    for g in range(ngroups):
        q = q_ref[:, g * gw:(g + 1) * gw].astype(F32)
        qt = jnp.concatenate([q] * SB_GROUP, axis=0).T
        qbd = jnp.where(head_match, qt, 0.0).astype(BF16)
        carry = jnp.zeros((1, nq), F32)
        acc = jnp.zeros((gw, nq), F32)
        carry, acc = tile(kn_ref, vn_ref, t_new, g, qbd, carry, acc, new_limit)
        carry, acc = tile(kbuf.at[slot], vbuf.at[slot], SB_TILE, g, qbd, carry, acc, None)
        qbds.append(qbd)
        carries.append(carry)
        accs.append(acc)

    def cond(st):
        t, carries, _ = st
        worst = functools.reduce(jnp.maximum, [jnp.max(c) for c in carries])
        return (t >= 0) & (worst > SB_CUTOFF_LOG2)

    def body(st):
        t, carries, accs = st
        rows = pl.ds(pl.multiple_of(t * tile_rows, tile_rows), tile_rows)
        far = (pltpu.make_async_copy(ck_ref.at[b, rows, :], kfar, far_sems.at[0]),
               pltpu.make_async_copy(cv_ref.at[b, rows, :], vfar, far_sems.at[1]))
        for cp in far:
            cp.start()
        for cp in far:
            cp.wait()
        res = [tile(kfar, vfar, SB_TILE, g, qbds[g], carries[g], accs[g], None) for g in range(ngroups)]
        return t - 1, [r[0] for r in res], [r[1] for r in res]

    _, _, accs = lax.while_loop(cond, body, (n_tiles - 2, carries, accs))
    for g in range(ngroups):
        at = accs[g].T
        for h in range(SB_GROUP):
            cols = slice(g * gw + h * SB_HD, g * gw + (h + 1) * SB_HD)
            o = at[h * t_new:(h + 1) * t_new, h * SB_HD:(h + 1) * SB_HD] * sb_ref[:, cols].astype(F32)
            o_ref[:, cols] = o.astype(BF16)


def _sb_sample(p_main, k5s, v5s, cache_k, cache_v, row0, dec_batch, t_new, past):
    rb0 = row0 // t_new
    tile_rows = SB_TILE * H_SB
    any_spec = pl.BlockSpec(memory_space=pl.ANY)
    new_spec = pl.BlockSpec((t_new * H_SB, SB_HD), lambda b: (b, 0))
    return pl.pallas_call(
        functools.partial(_sb_sample_kernel, t_new, past),
        grid=(dec_batch,),
        in_specs=[pl.BlockSpec((t_new, SB_W), lambda b: (rb0 + b, P_SQ // SB_W)),
                  new_spec, new_spec, any_spec, any_spec,
                  pl.BlockSpec((t_new, SB_W), lambda b: (rb0 + b, P_GB // SB_W))],
        out_specs=pl.BlockSpec((t_new, SB_W), lambda b: (b, 0)),
        out_shape=jax.ShapeDtypeStruct((dec_batch * t_new, SB_W), BF16),
        scratch_shapes=[
            pltpu.VMEM((2, tile_rows, SB_HD), F32),
            pltpu.VMEM((2, tile_rows, SB_HD), F32),
            pltpu.VMEM((tile_rows, SB_HD), F32),
            pltpu.VMEM((tile_rows, SB_HD), F32),
            pltpu.SemaphoreType.DMA((2, 2)),
            pltpu.SemaphoreType.DMA((2,)),
        ],
        compiler_params=pltpu.CompilerParams(
            dimension_semantics=("arbitrary",), vmem_limit_bytes=VMEM_LIMIT),
        name="sb_sample",
    )(p_main, k5s, v5s, cache_k, cache_v, p_main)


def _merge_kernel(n_prompt_tiles, ogp_ref, ogs_ref, obp_ref, obs_ref, xp_ref, xs_ref, w_ref, o_ref):
    m = pl.program_id(0)

    def run(og_ref, ob_ref, x_ref):
        mix = (og_ref[...].astype(F32) + ob_ref[...].astype(F32)).astype(BF16)
        o_ref[...] = x_ref[...] + _dot(mix, w_ref[...])

    @pl.when(m < n_prompt_tiles)
    def _():
        run(ogp_ref, obp_ref, xp_ref)

    @pl.when(m >= n_prompt_tiles)
    def _():
        run(ogs_ref, obs_ref, xs_ref)


def _merge(og_p, og_s, ob_p, ob_s, xp, xs, w_out):
    npt, nst = xp.shape[0] // TM, xs.shape[0] // TM
    pidx = lambda m: (jnp.minimum(m, npt - 1), 0)
    sidx = lambda m: (jnp.maximum(m - npt, 0), 0)
    row = lambda idx: pl.BlockSpec((TM, D_MODEL), idx)
    return pl.pallas_call(
        functools.partial(_merge_kernel, npt),
        grid=(npt + nst,),
        in_specs=[row(pidx), row(sidx), row(pidx), row(sidx), row(pidx), row(sidx),
                  pl.BlockSpec((D_MODEL, D_MODEL), lambda m: (0, 0))],
        out_specs=pl.BlockSpec((TM, D_MODEL), lambda m: (m, 0)),
        out_shape=jax.ShapeDtypeStruct((xp.shape[0] + xs.shape[0], D_MODEL), F32),
        compiler_params=pltpu.CompilerParams(
            dimension_semantics=("arbitrary",), vmem_limit_bytes=VMEM_LIMIT),
        name="merge_outproj",
    )(og_p, og_s, ob_p, ob_s, xp, xs, w_out)


def _ffn_kernel(n_prompt_tiles, x_ref, g2_ref, wu_ref, wd_ref, gf_ref, yp_ref, ys_ref, h_ref, acc_ref):
    m = pl.program_id(0)
    f = pl.program_id(1)

    @pl.when(f == 0)
    def _():
        h_ref[...] = _rms(x_ref[...], g2_ref[...]).astype(BF16)
        acc_ref[...] = jnp.zeros_like(acc_ref)

    u = jnp.maximum(_dot(h_ref[...], wu_ref[...]), 0.0)
    acc_ref[...] += _dot((u * u).astype(BF16), wd_ref[...])

    @pl.when(f == pl.num_programs(1) - 1)
    def _():
        y = _rms(x_ref[...] + acc_ref[...], gf_ref[...])

        @pl.when(m < n_prompt_tiles)
        def _():
            yp_ref[...] = y

        @pl.when(m >= n_prompt_tiles)
        def _():
            ys_ref[...] = y


def _ffn(x1, g2, w_up, w_down, gf, npr, nsr):
    npt, nst = npr // TM, nsr // TM
    return pl.pallas_call(
        functools.partial(_ffn_kernel, npt),
        grid=(npt + nst, D_FF // TF),
        in_specs=[
            pl.BlockSpec((TM, D_MODEL), lambda m, f: (m, 0)),
            pl.BlockSpec((1, D_MODEL), lambda m, f: (0, 0)),
            pl.BlockSpec((D_MODEL, TF), lambda m, f: (0, f)),
            pl.BlockSpec((TF, D_MODEL), lambda m, f: (f, 0)),
            pl.BlockSpec((1, D_MODEL), lambda m, f: (0, 0)),
        ],
        out_specs=[
            pl.BlockSpec((TM, D_MODEL), lambda m, f: (jnp.minimum(m, npt - 1), 0)),
            pl.BlockSpec((TM, D_MODEL), lambda m, f: (jnp.maximum(m - npt, 0), 0)),
        ],
        out_shape=[
            jax.ShapeDtypeStruct((npr, D_MODEL), F32),
            jax.ShapeDtypeStruct((nsr, D_MODEL), F32),
        ],
        scratch_shapes=[pltpu.VMEM((TM, D_MODEL), BF16), pltpu.VMEM((TM, D_MODEL), F32)],
        compiler_params=pltpu.CompilerParams(
            dimension_semantics=("arbitrary", "arbitrary"), vmem_limit_bytes=VMEM_LIMIT),
        name="ffn_final",
    )(x1, g2, w_up, w_down, gf)


def kernel(x_prompt, x_sample, cache_sb_k, cache_sb_v, state_gla, meta_tokens, norm1_g, w_in,
           w_alpha_up, b_alpha, gla_norm_g, w_out, norm2_g, w_up, w_down, norm_f_g):
    batch, seq, _ = x_prompt.shape
    dec_batch, t_new, _ = x_sample.shape
    depth, _, past = cache_sb_k.shape[:3]
    assert depth == 1 and w_in.shape[2] == _R_END
    assert seq % TM == 0 and (dec_batch * t_new) % TM == 0 and past % SB_TILE == 0
    assert t_new == SB_MINI and t_new % 16 == 0
    npr, nsr = batch * seq, dec_batch * t_new

    w = w_in[0]
    w_main = _prep_w_in(w)
    w_alow = jnp.pad(w[:, _R_ALOW:_R_SQ], ((0, 0), (0, LANE - GLA_RANK))).astype(BF16)
    wup = jnp.pad(w_alpha_up[0], ((0, LANE - GLA_RANK), (0, 0)))
    wup_hi = wup.astype(BF16)
    wup_lo = (wup - wup_hi.astype(F32)).astype(BF16)
    bias = b_alpha[0].reshape(1, GLA_QK)
    gn = gla_norm_g[0].reshape(1, GLA_V)
    g1 = norm1_g[0].reshape(1, D_MODEL)
    g2 = norm2_g[0].reshape(1, D_MODEL)
    gf = norm_f_g.reshape(1, D_MODEL)
    w_out_b = w_out[0].astype(BF16)
    w_up_b = w_up[0].astype(BF16)
    w_down_b = w_down[0].astype(BF16)

    xp = x_prompt.reshape(npr, D_MODEL)
    xs = x_sample.reshape(nsr, D_MODEL)
    meta = meta_tokens.astype(x_prompt.dtype)

    (p_main, p_meta, km32, vm32, alow, alow_meta,
     k5p, v5p, k5s, v5s) = _inproj(xp, xs, meta, g1, w_main, w_alow, batch, seq)

    og_p, st_p = _gla_prompt(p_main, p_meta, alow, alow_meta, wup_hi, wup_lo, bias, gn, batch, seq)
    og_s, st_s = _gla_sample(p_main, alow, state_gla[0], wup_hi, wup_lo, bias, gn, npr, dec_batch, t_new)
    ob_p = _sb_prompt(p_main, km32, vm32, batch, seq)
    ck = cache_sb_k[0].reshape(dec_batch, past * H_SB, SB_HD)
    cv = cache_sb_v[0].reshape(dec_batch, past * H_SB, SB_HD)
    ob_s = _sb_sample(p_main, k5s, v5s, ck, cv, npr, dec_batch, t_new, past)

    x1 = _merge(og_p, og_s, ob_p, ob_s, xp, xs, w_out_b)
    y_p, y_s = _ffn(x1, g2, w_up_b, w_down_b, gf, npr, nsr)

    return (
        y_p.reshape(batch, seq, D_MODEL),
        y_s.reshape(dec_batch, t_new, D_MODEL),
        st_p[None],
        k5p.reshape(1, batch, N_META + seq, H_SB, SB_HD),
        v5p.reshape(1, batch, N_META + seq, H_SB, SB_HD),
        st_s[None],
        k5s.reshape(1, dec_batch, t_new, H_SB, SB_HD),
        v5s.reshape(1, dec_batch, t_new, H_SB, SB_HD),
    )
```

```python
import functools

import jax
import jax.numpy as jnp
from jax import lax
from jax.experimental import pallas as pl
from jax.experimental.pallas import tpu as pltpu

F32 = jnp.float32
BF16 = jnp.bfloat16

D_MODEL = 2048
N_META = 16
H_GLA = 4
GLA_DK = 256
GLA_DV = 512
GLA_RANK = 16
GLA_TAU = 16.0
GLA_CHUNK = 64
H_SB = 16
SB_HD = 128
D_FF = 4 * D_MODEL
EPS = 1e-5
GLA_QK = H_GLA * GLA_DK
GLA_V = H_GLA * GLA_DV
SB_W = H_SB * SB_HD

_R_GK = GLA_QK
_R_GV = 2 * GLA_QK
_R_ALOW = _R_GV + GLA_V
_R_SQ = _R_ALOW + GLA_RANK
_R_SK = _R_SQ + SB_W
_R_SV = _R_SK + SB_W
_R_GA = _R_SV + SB_W
_R_GB = _R_GA + GLA_V
_R_END = _R_GB + SB_W

P_GQ = 0
P_GK = P_GQ + GLA_QK
P_GV = P_GK + GLA_QK
P_SQ = P_GV + GLA_V
P_SK = P_SQ + SB_W
P_SV = P_SK + SB_W
P_GA = P_SV + SB_W
P_GB = P_GA + GLA_V
P_W = P_GB + SB_W

LANE = 128
SUBLANE = 8
TM = 512
TN = 2048
TF = 1024
GLA_ROWS = 256
SB_TILE = 256
SB_MINI = 32
SB_STRIDE = SB_MINI // SUBLANE
SB_CUTOFF = -104.0
LOG2_E = 1.4426950408889634
SB_SCALE_LOG2 = SB_HD ** -0.5 * LOG2_E
SB_CUTOFF_LOG2 = SB_CUTOFF * LOG2_E
VMEM_LIMIT = 56 * 1024 * 1024


def _dot(a, b):
    return jnp.dot(a, b, preferred_element_type=F32)


def _dot_nt(a, b):
    return lax.dot_general(a, b, (((1,), (1,)), ((), ())), preferred_element_type=F32)


def _dot_tn(a, b):
    return lax.dot_general(a, b, (((0,), (0,)), ((), ())), preferred_element_type=F32)


def _split2(x):
    hi = x.astype(BF16)
    lo = (x - hi.astype(F32)).astype(BF16)
    return hi, lo


def _rms(x, g):
    ms = jnp.mean(x * x, axis=-1, keepdims=True)
    return x * lax.rsqrt(ms + EPS) * g


def _log_sigmoid(x):
    return jnp.minimum(x, 0.0) - jnp.log(1.0 + jnp.exp(-jnp.abs(x)))


_PREP_ROWS = 512
_PREP_COLS = 1024


def _prep_w_kernel(n_aligned, wt_ref, edge_ref, o_ref, al_ref):
    n = pl.program_id(1)

    @pl.when(n < n_aligned)
    def _():
        o_ref[...] = wt_ref[...].T.astype(BF16)

    @pl.when(n >= n_aligned)
    def _():
        shifted = jnp.concatenate([wt_ref[GLA_RANK:, :], edge_ref[...]], axis=0)
        o_ref[...] = shifted.T.astype(BF16)

    @pl.when(n == n_aligned)
    def _():
        pad = jnp.zeros((LANE - GLA_RANK, wt_ref.shape[1]), F32)
        al_ref[...] = jnp.concatenate([wt_ref[0:GLA_RANK, :], pad], axis=0).T.astype(BF16)


def _prep_w_in(wt):
    n_aligned = _R_ALOW // _PREP_COLS
    edge_blocks = _PREP_COLS // GLA_RANK
    return pl.pallas_call(
        functools.partial(_prep_w_kernel, n_aligned),
        grid=(D_MODEL // _PREP_ROWS, P_W // _PREP_COLS),
        in_specs=[
            pl.BlockSpec((_PREP_COLS, _PREP_ROWS), lambda r, n: (n, r)),
            pl.BlockSpec((GLA_RANK, _PREP_ROWS), lambda r, n: ((jnp.maximum(n, n_aligned) + 1) * edge_blocks, r)),
        ],
        out_specs=[pl.BlockSpec((_PREP_ROWS, _PREP_COLS), lambda r, n: (r, n)),
                   pl.BlockSpec((_PREP_ROWS, LANE), lambda r, n: (r, 0))],
        out_shape=[jax.ShapeDtypeStruct((D_MODEL, P_W), BF16),
                   jax.ShapeDtypeStruct((D_MODEL, LANE), BF16)],
        compiler_params=pltpu.CompilerParams(
            dimension_semantics=("arbitrary", "arbitrary"), vmem_limit_bytes=VMEM_LIMIT),
        name="prep_w_in",
    )(wt, wt)


_N_SK0 = P_SK // TN
_N_SV0 = P_SV // TN
_N_GATE0 = P_GA // TN


assert TN == SB_W


def _permute_rows(vals, perm_ref):
    rows = vals.shape[0]
    cols = []
    for cb in range(TN // LANE):
        slab = perm_ref.at[cb % 2]
        slab[...] = vals[:, cb * LANE:(cb + 1) * LANE]
        parts = [slab[pl.ds(SB_MINI * j + c, SUBLANE, stride=SB_STRIDE), :]
                 for j in range(rows // SB_MINI) for c in range(SB_STRIDE)]
        cols.append(jnp.concatenate(parts, axis=0).astype(BF16))
    return jnp.concatenate(cols, axis=1)


def _inproj_kernel(n_prompt_tiles, batch, seq, xp_ref, xs_ref, meta_ref, g1_ref, w_ref, wal_ref,
                   p_ref, pm_ref, km_ref, vm_ref, al_ref, alm_ref,
                   kp_ref, vp_ref, ks_ref, vs_ref,
                   h_ref, perm_ref, stage_ref, mstage_ref, sem, msem):
    m = pl.program_id(0)
    n = pl.program_id(1)
    n_tiles = pl.num_programs(0)
    tiles_per_batch = seq // TM

    def native_copy(dst_ref, token):
        return pltpu.make_async_copy(stage_ref, dst_ref.at[pl.ds(token * H_SB, TM * H_SB), :], sem)

    def stage_heads(ref, rows, vals):
        for h in range(H_SB):
            ref[pl.ds(h, rows, stride=H_SB), :] = vals[:, h * SB_HD:(h + 1) * SB_HD]

    def wait_native(pending):
        @pl.when(pending)
        def _():
            native_copy(kp_ref, 0).wait()

    def write_native(main, met, dst_p, dst_s):
        stage_heads(stage_ref, TM, main)

        @pl.when(m < n_prompt_tiles)
        def _():
            b = m // tiles_per_batch
            token = b * (seq + N_META) + N_META + (m - b * tiles_per_batch) * TM
            native_copy(dst_p, token).start()

        @pl.when(m >= n_prompt_tiles)
        def _():
            native_copy(dst_s, (m - n_prompt_tiles) * TM).start()

        @pl.when(m == 0)
        def _():
            stage_heads(mstage_ref, N_META, met)
            copies = [pltpu.make_async_copy(
                mstage_ref, dst_p.at[pl.ds(b * (seq + N_META) * H_SB, N_META * H_SB), :], msem)
                for b in range(batch)]
            for cp in copies:
                cp.start()
            for cp in copies:
                cp.wait()

    @pl.when(n == 0)
    def _():
        g = g1_ref[...]

        @pl.when(m < n_prompt_tiles)
        def _():
            h_ref[0:TM, :] = _rms(xp_ref[...], g).astype(BF16)

        @pl.when(m >= n_prompt_tiles)
        def _():
            h_ref[0:TM, :] = _rms(xs_ref[...], g).astype(BF16)

        h_ref[TM:TM + N_META, :] = _rms(meta_ref[...], g).astype(BF16)
        al = _dot(h_ref[...], wal_ref[...])
        al_ref[...] = al[:TM]
        alm_ref[...] = al[TM:]

    is_gate = n >= _N_GATE0
    first = m == 0
    is_k = (n >= _N_SK0) & (n < _N_SV0)
    is_v = (n >= _N_SV0) & (n < _N_GATE0)

    def project():
        acc = _dot(h_ref[...], w_ref[...])
        return acc[:TM], acc[TM:]

    def keep_meta(met, f32_ref):
        @pl.when(first)
        def _():
            pm_ref[...] = met.astype(BF16)
            if f32_ref is not None:
                f32_ref[...] = met

    @pl.when(jnp.logical_not(is_gate | is_k | is_v))
    def _():
        main, met = project()
        p_ref[...] = main.astype(BF16)
        keep_meta(met, None)

    @pl.when(is_gate)
    def _():
        main, met = project()
        p_ref[...] = jax.nn.sigmoid(main).astype(BF16)
        keep_meta(met, None)

    @pl.when(is_k)
    def _():
        wait_native(m > 0)
        main, met = project()
        p_ref[...] = _permute_rows(main, perm_ref)
        write_native(main, met, kp_ref, ks_ref)
        keep_meta(met, km_ref)

    @pl.when(is_v)
    def _():
        wait_native(True)
        main, met = project()
        p_ref[...] = _permute_rows(main, perm_ref)
        write_native(main, met, vp_ref, vs_ref)
        keep_meta(met, vm_ref)

    @pl.when((m == n_tiles - 1) & (n == pl.num_programs(1) - 1))
    def _():
        native_copy(vp_ref, 0).wait()


def _inproj(xp, xs, meta, g1, w_main, w_alow, batch, seq):
    npr, nsr = xp.shape[0], xs.shape[0]
    npt, nst = npr // TM, nsr // TM
    nt = npt + nst
    rows = npr + nsr
    nn = P_W // TN
    kcol = lambda n: jnp.clip(n - _N_SK0, 0, SB_W // TN - 1)
    vcol = lambda n: jnp.clip(n - _N_SV0, 0, SB_W // TN - 1)
    once = lambda m, col, last: jnp.where(m == 0, col, last)
    any_spec = pl.BlockSpec(memory_space=pl.ANY)
    return pl.pallas_call(
        functools.partial(_inproj_kernel, npt, batch, seq),
        grid=(nt, nn),
        in_specs=[
            pl.BlockSpec((TM, D_MODEL), lambda m, n: (jnp.minimum(m, npt - 1), 0)),
            pl.BlockSpec((TM, D_MODEL), lambda m, n: (jnp.maximum(m - npt, 0), 0),
                         pipeline_mode=pl.Buffered(1)),
            pl.BlockSpec((N_META, D_MODEL), lambda m, n: (0, 0)),
            pl.BlockSpec((1, D_MODEL), lambda m, n: (0, 0)),
            pl.BlockSpec((D_MODEL, TN), lambda m, n: (0, n)),
            pl.BlockSpec((D_MODEL, LANE), lambda m, n: (0, 0)),
        ],
        out_specs=[
            pl.BlockSpec((TM, TN), lambda m, n: (m, n)),
            pl.BlockSpec((N_META, TN), lambda m, n: (0, once(m, n, nn - 1))),
            pl.BlockSpec((N_META, TN), lambda m, n: (0, once(m, kcol(n), SB_W // TN - 1))),
            pl.BlockSpec((N_META, TN), lambda m, n: (0, once(m, vcol(n), SB_W // TN - 1))),
            pl.BlockSpec((TM, LANE), lambda m, n: (m, 0)),
            pl.BlockSpec((N_META, LANE), lambda m, n: (0, 0)),
            any_spec, any_spec, any_spec, any_spec,
        ],
        out_shape=[
            jax.ShapeDtypeStruct((rows, P_W), BF16),
            jax.ShapeDtypeStruct((N_META, P_W), BF16),
            jax.ShapeDtypeStruct((N_META, SB_W), F32),
            jax.ShapeDtypeStruct((N_META, SB_W), F32),
            jax.ShapeDtypeStruct((rows, LANE), F32),
            jax.ShapeDtypeStruct((N_META, LANE), F32),
            jax.ShapeDtypeStruct((batch * (seq + N_META) * H_SB, SB_HD), F32),
            jax.ShapeDtypeStruct((batch * (seq + N_META) * H_SB, SB_HD), F32),
            jax.ShapeDtypeStruct((nsr * H_SB, SB_HD), F32),
            jax.ShapeDtypeStruct((nsr * H_SB, SB_HD), F32),
        ],
        scratch_shapes=[
            pltpu.VMEM((TM + N_META, D_MODEL), BF16),
            pltpu.VMEM((2, TM, LANE), F32),
            pltpu.VMEM((TM * H_SB, SB_HD), F32),
            pltpu.VMEM((N_META * H_SB, SB_HD), F32),
            pltpu.SemaphoreType.DMA(()),
            pltpu.SemaphoreType.DMA(()),
        ],
        compiler_params=pltpu.CompilerParams(
            dimension_semantics=("arbitrary", "arbitrary"), vmem_limit_bytes=VMEM_LIMIT),
        name="inproj",
    )(xp, xs, meta, g1, w_main, w_alow)


def _gla_chunk(c, q, k, v, al, wup, bias, st_ref):
    x = _dot(al.astype(BF16), wup) + bias
    g = _log_sigmoid(x) * (1.0 / GLA_TAU)
    g_hi, g_lo = _split2(g)
    row = lax.broadcasted_iota(jnp.int32, (c, c), 0)
    col = lax.broadcasted_iota(jnp.int32, (c, c), 1)
    causal = col <= row
    tri = causal.astype(BF16)
    cum = _dot(tri, g_hi) + _dot(tri, g_lo)
    outs = []
    for h in range(H_GLA):
        ch = cum[:, h * GLA_DK:(h + 1) * GLA_DK]
        last = ch[c - 1:c, :]
        qh = q[:, h * GLA_DK:(h + 1) * GLA_DK].astype(F32)
        kh = k[:, h * GLA_DK:(h + 1) * GLA_DK].astype(F32)
        vh = v[:, h * GLA_DV:(h + 1) * GLA_DV]
        qd = (qh * jnp.exp(ch) * (GLA_DK ** -0.5)).astype(BF16)
        kd = (kh * jnp.exp(-ch)).astype(BF16)
        kr = (kh * jnp.exp(last - ch)).astype(BF16)
        att = jnp.where(causal, _dot_nt(qd, kd), 0.0).astype(BF16)
        st = st_ref[h]
        outs.append(_dot(att, vh) + _dot_nt(qd, st.astype(BF16)))
        st_ref[h] = st * jnp.exp(last) + _dot_tn(vh, kr)
    return outs


def _gla_finish(o, gn, sa):
    ms = jnp.mean(o * o, axis=-1, keepdims=True)
    return (o * lax.rsqrt(ms + EPS) * gn * sa.astype(F32)).astype(BF16)


def _gla_prompt_kernel(q_ref, k_ref, v_ref, sa_ref, al_ref, km_ref, vm_ref, alm_ref,
                       wup_ref, b_ref, gn_ref, og_ref, so_ref, st_ref):
    c_idx = pl.program_id(1)
    wup = wup_ref[...]
    bias = b_ref[...]

    @pl.when(c_idx == 0)
    def _():
        st_ref[...] = jnp.zeros_like(st_ref)
        zq = jnp.zeros((N_META, GLA_QK), BF16)
        _gla_chunk(N_META, zq, km_ref[...], vm_ref[...], alm_ref[...], wup, bias, st_ref)

    def body(i, carry):
        r0 = pl.multiple_of(i * GLA_CHUNK, GLA_CHUNK)
        rows = pl.ds(r0, GLA_CHUNK)
        outs = _gla_chunk(GLA_CHUNK, q_ref[rows, :], k_ref[rows, :], v_ref[rows, :], al_ref[rows, :],
                          wup, bias, st_ref)
        for h in range(H_GLA):
            cols = slice(h * GLA_DV, (h + 1) * GLA_DV)
            og_ref[rows, cols] = _gla_finish(outs[h], gn_ref[:, cols], sa_ref[rows, cols])
        return carry

    lax.fori_loop(0, GLA_ROWS // GLA_CHUNK, body, 0)

    @pl.when(c_idx == pl.num_programs(1) - 1)
    def _():
        for h in range(H_GLA):
            so_ref[0, h] = st_ref[h].T


def _gla_prompt(p_main, p_meta, alow, alow_meta, wup, bias, gn, batch, seq):
    nc = seq // GLA_ROWS
    rb = lambda b, c: b * nc + c
    const = lambda b, c: (0, 0)
    return pl.pallas_call(
        _gla_prompt_kernel,
        grid=(batch, nc),
        in_specs=[
            pl.BlockSpec((GLA_ROWS, GLA_QK), lambda b, c: (rb(b, c), P_GQ // GLA_QK)),
            pl.BlockSpec((GLA_ROWS, GLA_QK), lambda b, c: (rb(b, c), P_GK // GLA_QK)),
            pl.BlockSpec((GLA_ROWS, GLA_V), lambda b, c: (rb(b, c), P_GV // GLA_V)),
            pl.BlockSpec((GLA_ROWS, GLA_V), lambda b, c: (rb(b, c), P_GA // GLA_V)),
            pl.BlockSpec((GLA_ROWS, LANE), lambda b, c: (rb(b, c), 0)),
            pl.BlockSpec((N_META, GLA_QK), lambda b, c: (0, P_GK // GLA_QK)),
            pl.BlockSpec((N_META, GLA_V), lambda b, c: (0, P_GV // GLA_V)),
            pl.BlockSpec((N_META, LANE), const),
            pl.BlockSpec((LANE, GLA_QK), const),
            pl.BlockSpec((1, GLA_QK), const),
            pl.BlockSpec((1, GLA_V), const),
        ],
        out_specs=[
            pl.BlockSpec((GLA_ROWS, GLA_V), lambda b, c: (rb(b, c), 0)),
            pl.BlockSpec((1, H_GLA, GLA_DK, GLA_DV), lambda b, c: (b, 0, 0, 0)),
        ],
        out_shape=[
            jax.ShapeDtypeStruct((batch * seq, GLA_V), BF16),
            jax.ShapeDtypeStruct((batch, H_GLA, GLA_DK, GLA_DV), F32),
        ],
        scratch_shapes=[pltpu.VMEM((H_GLA, GLA_DV, GLA_DK), F32)],
        compiler_params=pltpu.CompilerParams(
            dimension_semantics=("arbitrary", "arbitrary"), vmem_limit_bytes=VMEM_LIMIT),
        name="gla_prompt",
    )(p_main, p_main, p_main, p_main, alow, p_meta, p_meta, alow_meta, wup, bias, gn)


def _gla_sample_kernel(t_new, q_ref, k_ref, v_ref, sa_ref, al_ref, s0_ref,
                       wup_ref, b_ref, gn_ref, og_ref, so_ref, st_ref):
    for h in range(H_GLA):
        st_ref[h] = s0_ref[0, h].T
    outs = _gla_chunk(t_new, q_ref[...], k_ref[...], v_ref[...], al_ref[...],
                      wup_ref[...], b_ref[...], st_ref)
    for h in range(H_GLA):
        cols = slice(h * GLA_DV, (h + 1) * GLA_DV)
        og_ref[:, cols] = _gla_finish(outs[h], gn_ref[:, cols], sa_ref[:, cols])
        so_ref[0, h] = st_ref[h].T


def _gla_sample(p_main, alow, state, wup, bias, gn, row0, dec_batch, t_new):
    rb0 = row0 // t_new
    const = lambda b: (0, 0)
    return pl.pallas_call(
        functools.partial(_gla_sample_kernel, t_new),
        grid=(dec_batch,),
        in_specs=[
            pl.BlockSpec((t_new, GLA_QK), lambda b: (rb0 + b, P_GQ // GLA_QK)),
            pl.BlockSpec((t_new, GLA_QK), lambda b: (rb0 + b, P_GK // GLA_QK)),
            pl.BlockSpec((t_new, GLA_V), lambda b: (rb0 + b, P_GV // GLA_V)),
            pl.BlockSpec((t_new, GLA_V), lambda b: (rb0 + b, P_GA // GLA_V)),
            pl.BlockSpec((t_new, LANE), lambda b: (rb0 + b, 0)),
            pl.BlockSpec((1, H_GLA, GLA_DK, GLA_DV), lambda b: (b, 0, 0, 0)),
            pl.BlockSpec((LANE, GLA_QK), const),
            pl.BlockSpec((1, GLA_QK), const),
            pl.BlockSpec((1, GLA_V), const),
        ],
        out_specs=[
            pl.BlockSpec((t_new, GLA_V), lambda b: (b, 0)),
            pl.BlockSpec((1, H_GLA, GLA_DK, GLA_DV), lambda b: (b, 0, 0, 0)),
        ],
        out_shape=[
            jax.ShapeDtypeStruct((dec_batch * t_new, GLA_V), BF16),
            jax.ShapeDtypeStruct((dec_batch, H_GLA, GLA_DK, GLA_DV), F32),
        ],
        scratch_shapes=[pltpu.VMEM((H_GLA, GLA_DV, GLA_DK), F32)],
        compiler_params=pltpu.CompilerParams(
            dimension_semantics=("arbitrary",), vmem_limit_bytes=VMEM_LIMIT),
        name="gla_sample",
    )(p_main, p_main, p_main, p_main, alow, state, wup, bias, gn)


def _load_perm(ref, base, nk):
    parts = []
    for j in range(nk // SB_MINI):
        for c in range(SB_STRIDE):
            parts.append(ref[pl.ds(base + SB_MINI * j + c, SUBLANE, stride=SB_STRIDE), :])
    return jnp.concatenate(parts, axis=0)


def _sb_scan(zt, carry, limit):
    nk, nq = zt.shape
    rowid = lax.broadcasted_iota(jnp.int32, (SUBLANE, nq), 0)
    w_parts = [None] * (nk // SUBLANE)
    for j in reversed(range(nk // SB_MINI)):
        ls, ss, vis = [], [], []
        for c in range(SB_STRIDE):
            i = j * SB_STRIDE + c
            z = zt[i * SUBLANE:(i + 1) * SUBLANE]
            nz = -z
            t = jnp.log2(1.0 + jnp.exp2(jnp.minimum(z, nz)))
            l = jnp.minimum(nz, 0.0) - t
            ss.append(l + z)
            if limit is not None:
                v = rowid * SB_STRIDE + (SB_MINI * j + c) < limit
                l = jnp.where(v, l, 0.0)
                vis.append(v)
            ls.append(l)
        later = [None] * SB_STRIDE
        run = ls[SB_STRIDE - 1]
        for c in range(SB_STRIDE - 2, -1, -1):
            later[c] = run
            run = run + ls[c]
        incl = run
        for sh in (1, 2, 4):
            incl = incl + jnp.where(rowid < SUBLANE - sh, pltpu.roll(incl, SUBLANE - sh, axis=0), 0.0)
        off = carry + (incl - run)
        for c in range(SB_STRIDE):
            after = off if later[c] is None else off + later[c]
            w = jnp.exp2(ss[c] + after)
            if limit is not None:
                w = jnp.where(vis[c], w, 0.0)
            w_parts[j * SB_STRIDE + c] = w
        carry = carry + incl[0:1, :]
    return jnp.concatenate(w_parts, axis=0), carry


SB_PAR = 4


def _sb_prompt_kernel(seq, q_ref, k_ref, v_ref, km_ref, vm_ref, sb_ref, o_ref, kms_ref, vms_ref):
    kms_ref[...] = jnp.zeros_like(kms_ref)
    vms_ref[...] = jnp.zeros_like(vms_ref)
    for i in range(SB_PAR):
        kms_ref[i, 0:N_META, :] = km_ref[:, i * SB_HD:(i + 1) * SB_HD]
        vms_ref[i, 0:N_META, :] = vm_ref[:, i * SB_HD:(i + 1) * SB_HD]
    heads = range(SB_PAR)
    diag_limit = lax.broadcasted_iota(jnp.int32, (SUBLANE, SB_TILE), 1)

    def tile(kp, vp, q, carry, acc, limit):
        zt = _dot_nt(kp, q) * SB_SCALE_LOG2
        wt, carry = _sb_scan(zt, carry, limit)
        return carry, acc + _dot_tn(vp, wt.astype(BF16))

    def prompt_tiles(base, qs, carries, accs, limit):
        rows = pl.ds(base, SB_TILE)
        res = [tile(k_ref[rows, i * SB_HD:(i + 1) * SB_HD], v_ref[rows, i * SB_HD:(i + 1) * SB_HD],
                    qs[i], carries[i], accs[i], limit) for i in heads]
        return [r[0] for r in res], [r[1] for r in res]

    def plain_tiles(t, qs, carries, accs):
        return prompt_tiles(pl.multiple_of(t * SB_TILE, SB_TILE), qs, carries, accs, None)

    def meta_tiles(qs, carries, accs):
        res = [tile(_load_perm(kms_ref.at[i], 0, SB_MINI).astype(BF16),
                    _load_perm(vms_ref.at[i], 0, SB_MINI).astype(BF16),
                    qs[i], carries[i], accs[i], N_META) for i in heads]
        return [r[0] for r in res], [r[1] for r in res]

    def qblock(m, is_first):
        row0 = pl.multiple_of(m * SB_TILE, SB_TILE)
        rows = pl.ds(row0, SB_TILE)
        qs = [q_ref[rows, i * SB_HD:(i + 1) * SB_HD] for i in heads]
        carry0 = [jnp.zeros((1, SB_TILE), F32)] * SB_PAR
        acc0 = [jnp.zeros((SB_HD, SB_TILE), F32)] * SB_PAR
        carries, accs = prompt_tiles(row0, qs, carry0, acc0, diag_limit)
        if is_first:
            _, accs = meta_tiles(qs, carries, accs)
        else:
            carries, accs = plain_tiles(m - 1, qs, carries, accs)

            def cond(st):
                t, carries, _ = st
                worst = functools.reduce(jnp.maximum, [jnp.max(c) for c in carries])
                return (t >= -1) & (worst > SB_CUTOFF_LOG2)

            def body(st):
                t, carries, accs = st
                carries, accs = lax.cond(t >= 0,
                                         lambda: plain_tiles(t, qs, carries, accs),
                                         lambda: meta_tiles(qs, carries, accs))
                return t - 1, carries, accs

            _, _, accs = lax.while_loop(cond, body, (m - 2, carries, accs))
        for i in heads:
            cols = slice(i * SB_HD, (i + 1) * SB_HD)
            o = accs[i].T * sb_ref[rows, cols].astype(F32)
            o_ref[rows, cols] = o.astype(BF16)

    qblock(0, True)

    def loop_body(m, carry):
        qblock(m, False)
        return carry

    lax.fori_loop(1, seq // SB_TILE, loop_body, 0)


def _sb_prompt(p_main, km32, vm32, batch, seq):
    gw = SB_PAR * SB_HD
    col = lambda off: pl.BlockSpec((seq, gw), lambda b, g: (b, off // gw + g))
    return pl.pallas_call(
        functools.partial(_sb_prompt_kernel, seq),
        grid=(batch, H_SB // SB_PAR),
        in_specs=[col(P_SQ), col(P_SK), col(P_SV),
                  pl.BlockSpec((N_META, gw), lambda b, g: (0, g)),
                  pl.BlockSpec((N_META, gw), lambda b, g: (0, g)),
                  col(P_GB)],
        out_specs=pl.BlockSpec((seq, gw), lambda b, g: (b, g)),
        out_shape=jax.ShapeDtypeStruct((batch * seq, SB_W), BF16),
        scratch_shapes=[pltpu.VMEM((SB_PAR, SB_MINI, SB_HD), F32), pltpu.VMEM((SB_PAR, SB_MINI, SB_HD), F32)],
        compiler_params=pltpu.CompilerParams(
            dimension_semantics=("arbitrary", "arbitrary"), vmem_limit_bytes=VMEM_LIMIT),
        name="sb_prompt",
    )(p_main, p_main, p_main, km32, vm32, p_main)


SB_GROUP = 4


def _load_perm_native(ref, key0, nk, head):
    parts = []
    for j in range(nk // SB_MINI):
        for c in range(SB_STRIDE):
            start = (key0 + SB_MINI * j + c) * H_SB + head
            parts.append(ref[pl.ds(start, SUBLANE, stride=SB_STRIDE * H_SB), :])
    return jnp.concatenate(parts, axis=0)


def _sb_sample_kernel(t_new, past, q_ref, kn_ref, vn_ref, ck_ref, cv_ref, sb_ref, o_ref,
                      kbuf, vbuf, kfar, vfar, sems, far_sems):
    b = pl.program_id(0)
    nb = pl.num_programs(0)
    gw = SB_GROUP * SB_HD
    nq = SB_GROUP * t_new
    ngroups = H_SB // SB_GROUP
    n_tiles = past // SB_TILE
    tile_rows = SB_TILE * H_SB

    def near_copies(bb, slot):
        rows = pl.ds((n_tiles - 1) * tile_rows, tile_rows)
        return (pltpu.make_async_copy(ck_ref.at[bb, rows, :], kbuf.at[slot], sems.at[slot, 0]),
                pltpu.make_async_copy(cv_ref.at[bb, rows, :], vbuf.at[slot], sems.at[slot, 1]))

    @pl.when(b == 0)
    def _():
        for cp in near_copies(0, 0):
            cp.start()

    slot = b % 2

    @pl.when(b + 1 < nb)
    def _():
        for cp in near_copies(b + 1, 1 - slot):
            cp.start()

    for cp in near_copies(b, slot):
        cp.wait()

    rh = lax.broadcasted_iota(jnp.int32, (gw, nq), 0) // SB_HD
    ch = lax.broadcasted_iota(jnp.int32, (gw, nq), 1) // t_new
    head_match = rh == ch
    new_limit = lax.broadcasted_iota(jnp.int32, (SUBLANE, nq), 1) % t_new

    def tile(kref, vref, nk, g, qbd, carry, acc, limit):
        hs = range(SB_GROUP * g, SB_GROUP * (g + 1))
        kp = jnp.concatenate([_load_perm_native(kref, 0, nk, h) for h in hs], axis=1).astype(BF16)
        vp = jnp.concatenate([_load_perm_native(vref, 0, nk, h) for h in hs], axis=1).astype(BF16)
        zt = _dot(kp, qbd) * SB_SCALE_LOG2
        wt, carry = _sb_scan(zt, carry, limit)
        return carry, acc + _dot_tn(vp, wt.astype(BF16))

    qbds, carries, accs = [], [], []
    for g in range(ngroups):
        q = q_ref[:, g * gw:(g + 1) * gw].astype(F32)
        qt = jnp.concatenate([q] * SB_GROUP, axis=0).T
        qbd = jnp.where(head_match, qt, 0.0).astype(BF16)
        carry = jnp.zeros((1, nq), F32)
        acc = jnp.zeros((gw, nq), F32)
        carry, acc = tile(kn_ref, vn_ref, t_new, g, qbd, carry, acc, new_limit)
        carry, acc = tile(kbuf.at[slot], vbuf.at[slot], SB_TILE, g, qbd, carry, acc, None)
        qbds.append(qbd)
        carries.append(carry)
        accs.append(acc)

    def cond(st):
        t, carries, _ = st
        worst = functools.reduce(jnp.maximum, [jnp.max(c) for c in carries])
        return (t >= 0) & (worst > SB_CUTOFF_LOG2)

    def body(st):
        t, carries, accs = st
        rows = pl.ds(pl.multiple_of(t * tile_rows, tile_rows), tile_rows)
        far = (pltpu.make_async_copy(ck_ref.at[b, rows, :], kfar, far_sems.at[0]),
               pltpu.make_async_copy(cv_ref.at[b, rows, :], vfar, far_sems.at[1]))
        for cp in far:
            cp.start()
        for cp in far:
            cp.wait()
        res = [tile(kfar, vfar, SB_TILE, g, qbds[g], carries[g], accs[g], None) for g in range(ngroups)]
        return t - 1, [r[0] for r in res], [r[1] for r in res]

    _, _, accs = lax.while_loop(cond, body, (n_tiles - 2, carries, accs))
    for g in range(ngroups):
        at = accs[g].T
        for h in range(SB_GROUP):
            cols = slice(g * gw + h * SB_HD, g * gw + (h + 1) * SB_HD)
            o = at[h * t_new:(h + 1) * t_new, h * SB_HD:(h + 1) * SB_HD] * sb_ref[:, cols].astype(F32)
            o_ref[:, cols] = o.astype(BF16)


def _sb_sample(p_main, k5s, v5s, cache_k, cache_v, row0, dec_batch, t_new, past):
    rb0 = row0 // t_new
    tile_rows = SB_TILE * H_SB
    any_spec = pl.BlockSpec(memory_space=pl.ANY)
    new_spec = pl.BlockSpec((t_new * H_SB, SB_HD), lambda b: (b, 0))
    return pl.pallas_call(
        functools.partial(_sb_sample_kernel, t_new, past),
        grid=(dec_batch,),
        in_specs=[pl.BlockSpec((t_new, SB_W), lambda b: (rb0 + b, P_SQ // SB_W)),
                  new_spec, new_spec, any_spec, any_spec,
                  pl.BlockSpec((t_new, SB_W), lambda b: (rb0 + b, P_GB // SB_W))],
        out_specs=pl.BlockSpec((t_new, SB_W), lambda b: (b, 0)),
        out_shape=jax.ShapeDtypeStruct((dec_batch * t_new, SB_W), BF16),
        scratch_shapes=[
            pltpu.VMEM((2, tile_rows, SB_HD), F32),
            pltpu.VMEM((2, tile_rows, SB_HD), F32),
            pltpu.VMEM((tile_rows, SB_HD), F32),
            pltpu.VMEM((tile_rows, SB_HD), F32),
            pltpu.SemaphoreType.DMA((2, 2)),
            pltpu.SemaphoreType.DMA((2,)),
        ],
        compiler_params=pltpu.CompilerParams(
            dimension_semantics=("arbitrary",), vmem_limit_bytes=VMEM_LIMIT),
        name="sb_sample",
    )(p_main, k5s, v5s, cache_k, cache_v, p_main)


def _merge_kernel(n_prompt_tiles, ogp_ref, ogs_ref, obp_ref, obs_ref, xp_ref, xs_ref, w_ref, o_ref):
    m = pl.program_id(0)

    def run(og_ref, ob_ref, x_ref):
        mix = (og_ref[...].astype(F32) + ob_ref[...].astype(F32)).astype(BF16)
        o_ref[...] = x_ref[...] + _dot(mix, w_ref[...])

    @pl.when(m < n_prompt_tiles)
    def _():
        run(ogp_ref, obp_ref, xp_ref)

    @pl.when(m >= n_prompt_tiles)
    def _():
        run(ogs_ref, obs_ref, xs_ref)


def _merge(og_p, og_s, ob_p, ob_s, xp, xs, w_out):
    npt, nst = xp.shape[0] // TM, xs.shape[0] // TM
    pidx = lambda m: (jnp.minimum(m, npt - 1), 0)
    sidx = lambda m: (jnp.maximum(m - npt, 0), 0)
    row = lambda idx: pl.BlockSpec((TM, D_MODEL), idx)
    return pl.pallas_call(
        functools.partial(_merge_kernel, npt),
        grid=(npt + nst,),
        in_specs=[row(pidx), row(sidx), row(pidx), row(sidx), row(pidx), row(sidx),
                  pl.BlockSpec((D_MODEL, D_MODEL), lambda m: (0, 0))],
        out_specs=pl.BlockSpec((TM, D_MODEL), lambda m: (m, 0)),
        out_shape=jax.ShapeDtypeStruct((xp.shape[0] + xs.shape[0], D_MODEL), F32),
        compiler_params=pltpu.CompilerParams(
            dimension_semantics=("arbitrary",), vmem_limit_bytes=VMEM_LIMIT),
        name="merge_outproj",
    )(og_p, og_s, ob_p, ob_s, xp, xs, w_out)


def _ffn_kernel(n_prompt_tiles, x_ref, g2_ref, wu_ref, wd_ref, gf_ref, yp_ref, ys_ref, h_ref, acc_ref):
    m = pl.program_id(0)
    f = pl.program_id(1)

    @pl.when(f == 0)
    def _():
        h_ref[...] = _rms(x_ref[...], g2_ref[...]).astype(BF16)
        acc_ref[...] = jnp.zeros_like(acc_ref)

    u = jnp.maximum(_dot(h_ref[...], wu_ref[...]), 0.0)
    acc_ref[...] += _dot((u * u).astype(BF16), wd_ref[...])

    @pl.when(f == pl.num_programs(1) - 1)
    def _():
        y = _rms(x_ref[...] + acc_ref[...], gf_ref[...])

        @pl.when(m < n_prompt_tiles)
        def _():
            yp_ref[...] = y

        @pl.when(m >= n_prompt_tiles)
        def _():
            ys_ref[...] = y


def _ffn(x1, g2, w_up, w_down, gf, npr, nsr):
    npt, nst = npr // TM, nsr // TM
    return pl.pallas_call(
        functools.partial(_ffn_kernel, npt),
        grid=(npt + nst, D_FF // TF),
        in_specs=[
            pl.BlockSpec((TM, D_MODEL), lambda m, f: (m, 0)),
            pl.BlockSpec((1, D_MODEL), lambda m, f: (0, 0)),
            pl.BlockSpec((D_MODEL, TF), lambda m, f: (0, f)),
            pl.BlockSpec((TF, D_MODEL), lambda m, f: (f, 0)),
            pl.BlockSpec((1, D_MODEL), lambda m, f: (0, 0)),
        ],
        out_specs=[
            pl.BlockSpec((TM, D_MODEL), lambda m, f: (jnp.minimum(m, npt - 1), 0)),
            pl.BlockSpec((TM, D_MODEL), lambda m, f: (jnp.maximum(m - npt, 0), 0)),
        ],
        out_shape=[
            jax.ShapeDtypeStruct((npr, D_MODEL), F32),
            jax.ShapeDtypeStruct((nsr, D_MODEL), F32),
        ],
        scratch_shapes=[pltpu.VMEM((TM, D_MODEL), BF16), pltpu.VMEM((TM, D_MODEL), F32)],
        compiler_params=pltpu.CompilerParams(
            dimension_semantics=("arbitrary", "arbitrary"), vmem_limit_bytes=VMEM_LIMIT),
        name="ffn_final",
    )(x1, g2, w_up, w_down, gf)


def kernel(x_prompt, x_sample, cache_sb_k, cache_sb_v, state_gla, meta_tokens, norm1_g, w_in,
           w_alpha_up, b_alpha, gla_norm_g, w_out, norm2_g, w_up, w_down, norm_f_g):
    batch, seq, _ = x_prompt.shape
    dec_batch, t_new, _ = x_sample.shape
    depth, _, past = cache_sb_k.shape[:3]
    assert depth == 1 and w_in.shape[2] == _R_END
    assert seq % TM == 0 and (dec_batch * t_new) % TM == 0 and past % SB_TILE == 0
    assert t_new == SB_MINI and t_new % 16 == 0
    npr, nsr = batch * seq, dec_batch * t_new

    w_main, w_alow = _prep_w_in(w_in[0].T)
    wup = jnp.pad(w_alpha_up[0], ((0, LANE - GLA_RANK), (0, 0))).astype(BF16)
    bias = b_alpha[0].reshape(1, GLA_QK)
    gn = gla_norm_g[0].reshape(1, GLA_V)
    g1 = norm1_g[0].reshape(1, D_MODEL)
    g2 = norm2_g[0].reshape(1, D_MODEL)
    gf = norm_f_g.reshape(1, D_MODEL)
    w_out_b = w_out[0].astype(BF16)
    w_up_b = w_up[0].astype(BF16)
    w_down_b = w_down[0].astype(BF16)

    xp = x_prompt.reshape(npr, D_MODEL)
    xs = x_sample.reshape(nsr, D_MODEL)
    meta = meta_tokens.astype(x_prompt.dtype)

    (p_main, p_meta, km32, vm32, alow, alow_meta,
     k5p, v5p, k5s, v5s) = _inproj(xp, xs, meta, g1, w_main, w_alow, batch, seq)

    og_p, st_p = _gla_prompt(p_main, p_meta, alow, alow_meta, wup, bias, gn, batch, seq)
    og_s, st_s = _gla_sample(p_main, alow, state_gla[0], wup, bias, gn, npr, dec_batch, t_new)
    ob_p = _sb_prompt(p_main, km32, vm32, batch, seq)
    ck = cache_sb_k[0].reshape(dec_batch, past * H_SB, SB_HD)
    cv = cache_sb_v[0].reshape(dec_batch, past * H_SB, SB_HD)
    ob_s = _sb_sample(p_main, k5s, v5s, ck, cv, npr, dec_batch, t_new, past)

    x1 = _merge(og_p, og_s, ob_p, ob_s, xp, xs, w_out_b)
    y_p, y_s = _ffn(x1, g2, w_up_b, w_down_b, gf, npr, nsr)

    return (
        y_p.reshape(batch, seq, D_MODEL),
        y_s.reshape(dec_batch, t_new, D_MODEL),
        st_p[None],
        k5p.reshape(1, batch, N_META + seq, H_SB, SB_HD),
        v5p.reshape(1, batch, N_META + seq, H_SB, SB_HD),
        st_s[None],
        k5s.reshape(1, dec_batch, t_new, H_SB, SB_HD),
        v5s.reshape(1, dec_batch, t_new, H_SB, SB_HD),
    )
```

```python
import functools

import jax
import jax.numpy as jnp
from jax import lax
from jax.experimental import pallas as pl
from jax.experimental.pallas import tpu as pltpu

F32 = jnp.float32
BF16 = jnp.bfloat16

D_MODEL = 2048
N_META = 16
H_GLA = 4
GLA_DK = 256
GLA_DV = 512
GLA_RANK = 16
GLA_TAU = 16.0
GLA_CHUNK = 64
H_SB = 16
SB_HD = 128
D_FF = 4 * D_MODEL
EPS = 1e-5
GLA_QK = H_GLA * GLA_DK
GLA_V = H_GLA * GLA_DV
SB_W = H_SB * SB_HD

_R_GK = GLA_QK
_R_GV = 2 * GLA_QK
_R_ALOW = _R_GV + GLA_V
_R_SQ = _R_ALOW + GLA_RANK
_R_SK = _R_SQ + SB_W
_R_SV = _R_SK + SB_W
_R_GA = _R_SV + SB_W
_R_GB = _R_GA + GLA_V
_R_END = _R_GB + SB_W

P_GQ = 0
P_GK = P_GQ + GLA_QK
P_GV = P_GK + GLA_QK
P_SQ = P_GV + GLA_V
P_SK = P_SQ + SB_W
P_SV = P_SK + SB_W
P_GA = P_SV + SB_W
P_GB = P_GA + GLA_V
P_W = P_GB + SB_W

LANE = 128
SUBLANE = 8
TM = 512
TN = 2048
TF = 1024
GLA_ROWS = 256
SB_TILE = 256
SB_MINI = 32
SB_STRIDE = SB_MINI // SUBLANE
SB_CUTOFF = -104.0
LOG2_E = 1.4426950408889634
SB_SCALE_LOG2 = SB_HD ** -0.5 * LOG2_E
SB_CUTOFF_LOG2 = SB_CUTOFF * LOG2_E
VMEM_LIMIT = 56 * 1024 * 1024


def _dot(a, b):
    return jnp.dot(a, b, preferred_element_type=F32)


def _dot_nt(a, b):
    return lax.dot_general(a, b, (((1,), (1,)), ((), ())), preferred_element_type=F32)


def _dot_tn(a, b):
    return lax.dot_general(a, b, (((0,), (0,)), ((), ())), preferred_element_type=F32)


def _split2(x):
    hi = x.astype(BF16)
    lo = (x - hi.astype(F32)).astype(BF16)
    return hi, lo


def _rms(x, g):
    ms = jnp.mean(x * x, axis=-1, keepdims=True)
    return x * lax.rsqrt(ms + EPS) * g


def _log_sigmoid(x):
    return jnp.minimum(x, 0.0) - jnp.log(1.0 + jnp.exp(-jnp.abs(x)))


_PREP_ROWS = 512
_PREP_COLS = 1024


def _prep_w_kernel(n_aligned, wt_ref, edge_ref, o_ref, al_ref):
    n = pl.program_id(1)

    @pl.when(n < n_aligned)
    def _():
        o_ref[...] = wt_ref[...].T.astype(BF16)

    def shifted():
        return jnp.concatenate([wt_ref[GLA_RANK:, :], edge_ref[...]], axis=0).T

    is_sq = (n >= P_SQ // _PREP_COLS) & (n < P_SK // _PREP_COLS)

    @pl.when((n >= n_aligned) & jnp.logical_not(is_sq))
    def _():
        o_ref[...] = shifted().astype(BF16)

    @pl.when(is_sq)
    def _():
        o_ref[...] = (shifted() * SB_SCALE_LOG2).astype(BF16)

    @pl.when(n == n_aligned)
    def _():
        pad = jnp.zeros((LANE - GLA_RANK, wt_ref.shape[1]), F32)
        al_ref[...] = jnp.concatenate([wt_ref[0:GLA_RANK, :], pad], axis=0).T.astype(BF16)


def _prep_w_in(wt):
    n_aligned = _R_ALOW // _PREP_COLS
    edge_blocks = _PREP_COLS // GLA_RANK
    return pl.pallas_call(
        functools.partial(_prep_w_kernel, n_aligned),
        grid=(D_MODEL // _PREP_ROWS, P_W // _PREP_COLS),
        in_specs=[
            pl.BlockSpec((_PREP_COLS, _PREP_ROWS), lambda r, n: (n, r)),
            pl.BlockSpec((GLA_RANK, _PREP_ROWS), lambda r, n: ((jnp.maximum(n, n_aligned) + 1) * edge_blocks, r)),
        ],
        out_specs=[pl.BlockSpec((_PREP_ROWS, _PREP_COLS), lambda r, n: (r, n)),
                   pl.BlockSpec((_PREP_ROWS, LANE), lambda r, n: (r, 0))],
        out_shape=[jax.ShapeDtypeStruct((D_MODEL, P_W), BF16),
                   jax.ShapeDtypeStruct((D_MODEL, LANE), BF16)],
        compiler_params=pltpu.CompilerParams(
            dimension_semantics=("arbitrary", "arbitrary"), vmem_limit_bytes=VMEM_LIMIT),
        name="prep_w_in",
    )(wt, wt)


_N_SK0 = P_SK // TN
_N_SV0 = P_SV // TN
_N_GATE0 = P_GA // TN


assert TN == SB_W


def _permute_rows(vals, perm_ref):
    rows = vals.shape[0]
    cols = []
    for cb in range(TN // LANE):
        slab = perm_ref.at[cb % 2]
        slab[...] = vals[:, cb * LANE:(cb + 1) * LANE]
        parts = [slab[pl.ds(SB_MINI * j + c, SUBLANE, stride=SB_STRIDE), :]
                 for j in range(rows // SB_MINI) for c in range(SB_STRIDE)]
        cols.append(jnp.concatenate(parts, axis=0).astype(BF16))
    return jnp.concatenate(cols, axis=1)


def _inproj_kernel(n_prompt_tiles, batch, seq, xp_ref, xs_ref, meta_ref, g1_ref, w_ref, wal_ref,
                   p_ref, pm_ref, km_ref, vm_ref, al_ref, alm_ref,
                   kp_ref, vp_ref, ks_ref, vs_ref,
                   h_ref, perm_ref, stage_ref, mstage_ref, sem, msem):
    m = pl.program_id(0)
    n = pl.program_id(1)
    n_tiles = pl.num_programs(0)
    tiles_per_batch = seq // TM

    def native_copy(dst_ref, token):
        return pltpu.make_async_copy(stage_ref, dst_ref.at[pl.ds(token * H_SB, TM * H_SB), :], sem)

    def stage_heads(ref, rows, vals):
        for h in range(H_SB):
            ref[pl.ds(h, rows, stride=H_SB), :] = vals[:, h * SB_HD:(h + 1) * SB_HD]

    def wait_native(pending):
        @pl.when(pending)
        def _():
            native_copy(kp_ref, 0).wait()

    def write_native(main, dst_p, dst_s):
        stage_heads(stage_ref, TM, main)

        @pl.when(m < n_prompt_tiles)
        def _():
            b = m // tiles_per_batch
            token = b * (seq + N_META) + N_META + (m - b * tiles_per_batch) * TM
            native_copy(dst_p, token).start()

        @pl.when(m >= n_prompt_tiles)
        def _():
            native_copy(dst_s, (m - n_prompt_tiles) * TM).start()

    def write_native_meta(met, dst_p):
        stage_heads(mstage_ref, N_META, met)
        copies = [pltpu.make_async_copy(
            mstage_ref, dst_p.at[pl.ds(b * (seq + N_META) * H_SB, N_META * H_SB), :], msem)
            for b in range(batch)]
        for cp in copies:
            cp.start()
        for cp in copies:
            cp.wait()

    first = m == 0

    @pl.when(n == 0)
    def _():
        g = g1_ref[...]

        @pl.when(m < n_prompt_tiles)
        def _():
            h_ref[0:TM, :] = _rms(xp_ref[...], g).astype(BF16)

        @pl.when(m >= n_prompt_tiles)
        def _():
            h_ref[0:TM, :] = _rms(xs_ref[...], g).astype(BF16)

        al_ref[...] = _dot(h_ref[0:TM, :], wal_ref[...])

        @pl.when(first)
        def _():
            h_ref[TM:TM + N_META, :] = _rms(meta_ref[...], g).astype(BF16)
            alm_ref[...] = _dot(h_ref[TM:TM + N_META, :], wal_ref[...])

    is_gate = n >= _N_GATE0
    is_k = (n >= _N_SK0) & (n < _N_SV0)
    is_v = (n >= _N_SV0) & (n < _N_GATE0)

    def column_tile(kind, epilogue, meta_epilogue):
        @pl.when(kind & first)
        def _():
            acc = _dot(h_ref[...], w_ref[...])
            epilogue(acc[:TM])
            pm_ref[...] = acc[TM:].astype(BF16)
            meta_epilogue(acc[TM:])

        @pl.when(kind & jnp.logical_not(first))
        def _():
            epilogue(_dot(h_ref[0:TM, :], w_ref[...]))

    def plain(main):
        p_ref[...] = main.astype(BF16)

    def gate(main):
        p_ref[...] = jax.nn.sigmoid(main).astype(BF16)

    def kv(dst_p, dst_s):
        def epilogue(main):
            p_ref[...] = _permute_rows(main, perm_ref)
            write_native(main, dst_p, dst_s)
        return epilogue

    def kv_meta(f32_ref, dst_p):
        def epilogue(met):
            f32_ref[...] = met
            write_native_meta(met, dst_p)
        return epilogue

    @pl.when(is_k)
    def _():
        wait_native(m > 0)

    @pl.when(is_v)
    def _():
        wait_native(True)

    column_tile(jnp.logical_not(is_gate | is_k | is_v), plain, lambda met: None)
    column_tile(is_gate, gate, lambda met: None)
    column_tile(is_k, kv(kp_ref, ks_ref), kv_meta(km_ref, kp_ref))
    column_tile(is_v, kv(vp_ref, vs_ref), kv_meta(vm_ref, vp_ref))

    @pl.when((m == n_tiles - 1) & (n == pl.num_programs(1) - 1))
    def _():
        native_copy(vp_ref, 0).wait()


def _inproj(xp, xs, meta, g1, w_main, w_alow, batch, seq):
    npr, nsr = xp.shape[0], xs.shape[0]
    npt, nst = npr // TM, nsr // TM
    nt = npt + nst
    rows = npr + nsr
    nn = P_W // TN
    kcol = lambda n: jnp.clip(n - _N_SK0, 0, SB_W // TN - 1)
    vcol = lambda n: jnp.clip(n - _N_SV0, 0, SB_W // TN - 1)
    once = lambda m, col, last: jnp.where(m == 0, col, last)
    any_spec = pl.BlockSpec(memory_space=pl.ANY)
    return pl.pallas_call(
        functools.partial(_inproj_kernel, npt, batch, seq),
        grid=(nt, nn),
        in_specs=[
            pl.BlockSpec((TM, D_MODEL), lambda m, n: (jnp.minimum(m, npt - 1), 0)),
            pl.BlockSpec((TM, D_MODEL), lambda m, n: (jnp.maximum(m - npt, 0), 0),
                         pipeline_mode=pl.Buffered(1)),
            pl.BlockSpec((N_META, D_MODEL), lambda m, n: (0, 0)),
            pl.BlockSpec((1, D_MODEL), lambda m, n: (0, 0)),
            pl.BlockSpec((D_MODEL, TN), lambda m, n: (0, n)),
            pl.BlockSpec((D_MODEL, LANE), lambda m, n: (0, 0)),
        ],
        out_specs=[
            pl.BlockSpec((TM, TN), lambda m, n: (m, n)),
            pl.BlockSpec((N_META, TN), lambda m, n: (0, once(m, n, nn - 1))),
            pl.BlockSpec((N_META, TN), lambda m, n: (0, once(m, kcol(n), SB_W // TN - 1))),
            pl.BlockSpec((N_META, TN), lambda m, n: (0, once(m, vcol(n), SB_W // TN - 1))),
            pl.BlockSpec((TM, LANE), lambda m, n: (m, 0)),
            pl.BlockSpec((N_META, LANE), lambda m, n: (0, 0)),
            any_spec, any_spec, any_spec, any_spec,
        ],
        out_shape=[
            jax.ShapeDtypeStruct((rows, P_W), BF16),
            jax.ShapeDtypeStruct((N_META, P_W), BF16),
            jax.ShapeDtypeStruct((N_META, SB_W), F32),
            jax.ShapeDtypeStruct((N_META, SB_W), F32),
            jax.ShapeDtypeStruct((rows, LANE), F32),
            jax.ShapeDtypeStruct((N_META, LANE), F32),
            jax.ShapeDtypeStruct((batch * (seq + N_META) * H_SB, SB_HD), F32),
            jax.ShapeDtypeStruct((batch * (seq + N_META) * H_SB, SB_HD), F32),
            jax.ShapeDtypeStruct((nsr * H_SB, SB_HD), F32),
            jax.ShapeDtypeStruct((nsr * H_SB, SB_HD), F32),
        ],
        scratch_shapes=[
            pltpu.VMEM((TM + N_META, D_MODEL), BF16),
            pltpu.VMEM((2, TM, LANE), F32),
            pltpu.VMEM((TM * H_SB, SB_HD), F32),
            pltpu.VMEM((N_META * H_SB, SB_HD), F32),
            pltpu.SemaphoreType.DMA(()),
            pltpu.SemaphoreType.DMA(()),
        ],
        compiler_params=pltpu.CompilerParams(
            dimension_semantics=("arbitrary", "arbitrary"), vmem_limit_bytes=VMEM_LIMIT),
        name="inproj",
    )(xp, xs, meta, g1, w_main, w_alow)


def _gla_chunk(c, q, k, v, al, wup, bias, st_ref):
    x = _dot(al.astype(BF16), wup) + bias
    g = _log_sigmoid(x) * (1.0 / GLA_TAU)
    g_hi, g_lo = _split2(g)
    row = lax.broadcasted_iota(jnp.int32, (c, c), 0)
    col = lax.broadcasted_iota(jnp.int32, (c, c), 1)
    causal = col <= row
    tri = causal.astype(BF16)
    cum = _dot(tri, g_hi) + _dot(tri, g_lo)
    outs = []
    for h in range(H_GLA):
        ch = cum[:, h * GLA_DK:(h + 1) * GLA_DK]
        last = ch[c - 1:c, :]
        qh = q[:, h * GLA_DK:(h + 1) * GLA_DK].astype(F32)
        kh = k[:, h * GLA_DK:(h + 1) * GLA_DK].astype(F32)
        vh = v[:, h * GLA_DV:(h + 1) * GLA_DV]
        qd = (qh * jnp.exp(ch) * (GLA_DK ** -0.5)).astype(BF16)
        kd = (kh * jnp.exp(-ch)).astype(BF16)
        kr = (kh * jnp.exp(last - ch)).astype(BF16)
        att = jnp.where(causal, _dot_nt(qd, kd), 0.0).astype(BF16)
        st = st_ref[h]
        outs.append(_dot(att, vh) + _dot_nt(qd, st.astype(BF16)))
        st_ref[h] = st * jnp.exp(last) + _dot_tn(vh, kr)
    return outs


def _gla_finish(o, gn, sa):
    ms = jnp.mean(o * o, axis=-1, keepdims=True)
    return (o * lax.rsqrt(ms + EPS) * gn * sa.astype(F32)).astype(BF16)


def _gla_prompt_kernel(q_ref, k_ref, v_ref, sa_ref, al_ref, km_ref, vm_ref, alm_ref,
                       wup_ref, b_ref, gn_ref, og_ref, so_ref, st_ref):
    c_idx = pl.program_id(1)
    wup = wup_ref[...]
    bias = b_ref[...]

    @pl.when(c_idx == 0)
    def _():
        st_ref[...] = jnp.zeros_like(st_ref)
        zq = jnp.zeros((N_META, GLA_QK), BF16)
        _gla_chunk(N_META, zq, km_ref[...], vm_ref[...], alm_ref[...], wup, bias, st_ref)

    def body(i, carry):
        r0 = pl.multiple_of(i * GLA_CHUNK, GLA_CHUNK)
        rows = pl.ds(r0, GLA_CHUNK)
        outs = _gla_chunk(GLA_CHUNK, q_ref[rows, :], k_ref[rows, :], v_ref[rows, :], al_ref[rows, :],
                          wup, bias, st_ref)
        for h in range(H_GLA):
            cols = slice(h * GLA_DV, (h + 1) * GLA_DV)
            og_ref[rows, cols] = _gla_finish(outs[h], gn_ref[:, cols], sa_ref[rows, cols])
        return carry

    lax.fori_loop(0, GLA_ROWS // GLA_CHUNK, body, 0, unroll=True)

    @pl.when(c_idx == pl.num_programs(1) - 1)
    def _():
        for h in range(H_GLA):
            so_ref[0, h] = st_ref[h].T


def _gla_prompt(p_main, p_meta, alow, alow_meta, wup, bias, gn, batch, seq):
    nc = seq // GLA_ROWS
    rb = lambda b, c: b * nc + c
    const = lambda b, c: (0, 0)
    return pl.pallas_call(
        _gla_prompt_kernel,
        grid=(batch, nc),
        in_specs=[
            pl.BlockSpec((GLA_ROWS, GLA_QK), lambda b, c: (rb(b, c), P_GQ // GLA_QK)),
            pl.BlockSpec((GLA_ROWS, GLA_QK), lambda b, c: (rb(b, c), P_GK // GLA_QK)),
            pl.BlockSpec((GLA_ROWS, GLA_V), lambda b, c: (rb(b, c), P_GV // GLA_V)),
            pl.BlockSpec((GLA_ROWS, GLA_V), lambda b, c: (rb(b, c), P_GA // GLA_V)),
            pl.BlockSpec((GLA_ROWS, LANE), lambda b, c: (rb(b, c), 0)),
            pl.BlockSpec((N_META, GLA_QK), lambda b, c: (0, P_GK // GLA_QK)),
            pl.BlockSpec((N_META, GLA_V), lambda b, c: (0, P_GV // GLA_V)),
            pl.BlockSpec((N_META, LANE), const),
            pl.BlockSpec((LANE, GLA_QK), const),
            pl.BlockSpec((1, GLA_QK), const),
            pl.BlockSpec((1, GLA_V), const),
        ],
        out_specs=[
            pl.BlockSpec((GLA_ROWS, GLA_V), lambda b, c: (rb(b, c), 0)),
            pl.BlockSpec((1, H_GLA, GLA_DK, GLA_DV), lambda b, c: (b, 0, 0, 0)),
        ],
        out_shape=[
            jax.ShapeDtypeStruct((batch * seq, GLA_V), BF16),
            jax.ShapeDtypeStruct((batch, H_GLA, GLA_DK, GLA_DV), F32),
        ],
        scratch_shapes=[pltpu.VMEM((H_GLA, GLA_DV, GLA_DK), F32)],
        compiler_params=pltpu.CompilerParams(
            dimension_semantics=("arbitrary", "arbitrary"), vmem_limit_bytes=VMEM_LIMIT),
        name="gla_prompt",
    )(p_main, p_main, p_main, p_main, alow, p_meta, p_meta, alow_meta, wup, bias, gn)


def _gla_sample_kernel(t_new, q_ref, k_ref, v_ref, sa_ref, al_ref, s0_ref,
                       wup_ref, b_ref, gn_ref, og_ref, so_ref, st_ref):
    for h in range(H_GLA):
        st_ref[h] = s0_ref[0, h].T
    outs = _gla_chunk(t_new, q_ref[...], k_ref[...], v_ref[...], al_ref[...],
                      wup_ref[...], b_ref[...], st_ref)
    for h in range(H_GLA):
        cols = slice(h * GLA_DV, (h + 1) * GLA_DV)
        og_ref[:, cols] = _gla_finish(outs[h], gn_ref[:, cols], sa_ref[:, cols])
        so_ref[0, h] = st_ref[h].T


def _gla_sample(p_main, alow, state, wup, bias, gn, row0, dec_batch, t_new):
    rb0 = row0 // t_new
    const = lambda b: (0, 0)
    return pl.pallas_call(
        functools.partial(_gla_sample_kernel, t_new),
        grid=(dec_batch,),
        in_specs=[
            pl.BlockSpec((t_new, GLA_QK), lambda b: (rb0 + b, P_GQ // GLA_QK)),
            pl.BlockSpec((t_new, GLA_QK), lambda b: (rb0 + b, P_GK // GLA_QK)),
            pl.BlockSpec((t_new, GLA_V), lambda b: (rb0 + b, P_GV // GLA_V)),
            pl.BlockSpec((t_new, GLA_V), lambda b: (rb0 + b, P_GA // GLA_V)),
            pl.BlockSpec((t_new, LANE), lambda b: (rb0 + b, 0)),
            pl.BlockSpec((1, H_GLA, GLA_DK, GLA_DV), lambda b: (b, 0, 0, 0)),
            pl.BlockSpec((LANE, GLA_QK), const),
            pl.BlockSpec((1, GLA_QK), const),
            pl.BlockSpec((1, GLA_V), const),
        ],
        out_specs=[
            pl.BlockSpec((t_new, GLA_V), lambda b: (b, 0)),
            pl.BlockSpec((1, H_GLA, GLA_DK, GLA_DV), lambda b: (b, 0, 0, 0)),
        ],
        out_shape=[
            jax.ShapeDtypeStruct((dec_batch * t_new, GLA_V), BF16),
            jax.ShapeDtypeStruct((dec_batch, H_GLA, GLA_DK, GLA_DV), F32),
        ],
        scratch_shapes=[pltpu.VMEM((H_GLA, GLA_DV, GLA_DK), F32)],
        compiler_params=pltpu.CompilerParams(
            dimension_semantics=("arbitrary",), vmem_limit_bytes=VMEM_LIMIT),
        name="gla_sample",
    )(p_main, p_main, p_main, p_main, alow, state, wup, bias, gn)


def _load_perm(ref, base, nk):
    parts = []
    for j in range(nk // SB_MINI):
        for c in range(SB_STRIDE):
            parts.append(ref[pl.ds(base + SB_MINI * j + c, SUBLANE, stride=SB_STRIDE), :])
    return jnp.concatenate(parts, axis=0)


def _sb_scan(zt, carry, limit):
    nk, nq = zt.shape
    rowid = lax.broadcasted_iota(jnp.int32, (SUBLANE, nq), 0)
    w_parts = [None] * (nk // SUBLANE)
    for j in reversed(range(nk // SB_MINI)):
        ls, ss, vis = [], [], []
        for c in range(SB_STRIDE):
            i = j * SB_STRIDE + c
            z = zt[i * SUBLANE:(i + 1) * SUBLANE]
            nz = -z
            t = jnp.log2(1.0 + jnp.exp2(jnp.minimum(z, nz)))
            l = jnp.minimum(nz, 0.0) - t
            ss.append(l + z)
            if limit is not None:
                v = rowid * SB_STRIDE + (SB_MINI * j + c) < limit
                l = jnp.where(v, l, 0.0)
                vis.append(v)
            ls.append(l)
        later = [None] * SB_STRIDE
        run = ls[SB_STRIDE - 1]
        for c in range(SB_STRIDE - 2, -1, -1):
            later[c] = run
            run = run + ls[c]
        incl = run
        for sh in (1, 2, 4):
            incl = incl + jnp.where(rowid < SUBLANE - sh, pltpu.roll(incl, SUBLANE - sh, axis=0), 0.0)
        off = carry + (incl - run)
        for c in range(SB_STRIDE):
            after = off if later[c] is None else off + later[c]
            w = jnp.exp2(ss[c] + after)
            if limit is not None:
                w = jnp.where(vis[c], w, 0.0)
            w_parts[j * SB_STRIDE + c] = w
        carry = carry + incl[0:1, :]
    return jnp.concatenate(w_parts, axis=0), carry


SB_PAR = 4


def _sb_prompt_kernel(seq, q_ref, k_ref, v_ref, km_ref, vm_ref, sb_ref, o_ref, kms_ref, vms_ref):
    kms_ref[...] = jnp.zeros_like(kms_ref)
    vms_ref[...] = jnp.zeros_like(vms_ref)
    for i in range(SB_PAR):
        kms_ref[i, 0:N_META, :] = km_ref[:, i * SB_HD:(i + 1) * SB_HD]
        vms_ref[i, 0:N_META, :] = vm_ref[:, i * SB_HD:(i + 1) * SB_HD]
    heads = range(SB_PAR)
    diag_limit = lax.broadcasted_iota(jnp.int32, (SUBLANE, SB_TILE), 1)

    def tile(kp, vp, q, carry, acc, limit):
        zt = _dot_nt(kp, q)
        wt, carry = _sb_scan(zt, carry, limit)
        return carry, acc + _dot_tn(vp, wt.astype(BF16))

    def prompt_tiles(base, qs, carries, accs, limit):
        rows = pl.ds(base, SB_TILE)
        res = [tile(k_ref[rows, i * SB_HD:(i + 1) * SB_HD], v_ref[rows, i * SB_HD:(i + 1) * SB_HD],
                    qs[i], carries[i], accs[i], limit) for i in heads]
        return [r[0] for r in res], [r[1] for r in res]

    def plain_tiles(t, qs, carries, accs):
        return prompt_tiles(pl.multiple_of(t * SB_TILE, SB_TILE), qs, carries, accs, None)

    def meta_tiles(qs, carries, accs):
        res = [tile(_load_perm(kms_ref.at[i], 0, SB_MINI).astype(BF16),
                    _load_perm(vms_ref.at[i], 0, SB_MINI).astype(BF16),
                    qs[i], carries[i], accs[i], N_META) for i in heads]
        return [r[0] for r in res], [r[1] for r in res]

    def qblock(m, is_first):
        row0 = pl.multiple_of(m * SB_TILE, SB_TILE)
        rows = pl.ds(row0, SB_TILE)
        qs = [q_ref[rows, i * SB_HD:(i + 1) * SB_HD] for i in heads]
        carry0 = [jnp.zeros((1, SB_TILE), F32)] * SB_PAR
        acc0 = [jnp.zeros((SB_HD, SB_TILE), F32)] * SB_PAR
        carries, accs = prompt_tiles(row0, qs, carry0, acc0, diag_limit)
        if is_first:
            _, accs = meta_tiles(qs, carries, accs)
        else:
            carries, accs = plain_tiles(m - 1, qs, carries, accs)

            def cond(st):
                t, carries, _ = st
                worst = functools.reduce(jnp.maximum, [jnp.max(c) for c in carries])
                return (t >= -1) & (worst > SB_CUTOFF_LOG2)

            def body(st):
                t, carries, accs = st
                carries, accs = lax.cond(t >= 0,
                                         lambda: plain_tiles(t, qs, carries, accs),
                                         lambda: meta_tiles(qs, carries, accs))
                return t - 1, carries, accs

            _, _, accs = lax.while_loop(cond, body, (m - 2, carries, accs))
        for i in heads:
            cols = slice(i * SB_HD, (i + 1) * SB_HD)
            o = accs[i].T * sb_ref[rows, cols].astype(F32)
            o_ref[rows, cols] = o.astype(BF16)

    qblock(0, True)

    def loop_body(m, carry):
        qblock(m, False)
        return carry

    lax.fori_loop(1, seq // SB_TILE, loop_body, 0)


def _sb_prompt(p_main, km32, vm32, batch, seq):
    gw = SB_PAR * SB_HD
    col = lambda off: pl.BlockSpec((seq, gw), lambda b, g: (b, off // gw + g))
    return pl.pallas_call(
        functools.partial(_sb_prompt_kernel, seq),
        grid=(batch, H_SB // SB_PAR),
        in_specs=[col(P_SQ), col(P_SK), col(P_SV),
                  pl.BlockSpec((N_META, gw), lambda b, g: (0, g)),
                  pl.BlockSpec((N_META, gw), lambda b, g: (0, g)),
                  col(P_GB)],
        out_specs=pl.BlockSpec((seq, gw), lambda b, g: (b, g)),
        out_shape=jax.ShapeDtypeStruct((batch * seq, SB_W), BF16),
        scratch_shapes=[pltpu.VMEM((SB_PAR, SB_MINI, SB_HD), F32), pltpu.VMEM((SB_PAR, SB_MINI, SB_HD), F32)],
        compiler_params=pltpu.CompilerParams(
            dimension_semantics=("arbitrary", "arbitrary"), vmem_limit_bytes=VMEM_LIMIT),
        name="sb_prompt",
    )(p_main, p_main, p_main, km32, vm32, p_main)


SB_GROUP = 4


def _load_perm_native(ref, key0, nk, head):
    parts = []
    for j in range(nk // SB_MINI):
        for c in range(SB_STRIDE):
            start = (key0 + SB_MINI * j + c) * H_SB + head
            parts.append(ref[pl.ds(start, SUBLANE, stride=SB_STRIDE * H_SB), :])
    return jnp.concatenate(parts, axis=0)


def _sb_sample_kernel(t_new, past, q_ref, kn_ref, vn_ref, ck_ref, cv_ref, sb_ref, o_ref,
                      kbuf, vbuf, kfar, vfar, sems, far_sems):
    b = pl.program_id(0)
    nb = pl.num_programs(0)
    gw = SB_GROUP * SB_HD
    nq = SB_GROUP * t_new
    ngroups = H_SB // SB_GROUP
    n_tiles = past // SB_TILE
    tile_rows = SB_TILE * H_SB

    def near_copies(bb, slot):
        rows = pl.ds((n_tiles - 1) * tile_rows, tile_rows)
        return (pltpu.make_async_copy(ck_ref.at[bb, rows, :], kbuf.at[slot], sems.at[slot, 0]),
                pltpu.make_async_copy(cv_ref.at[bb, rows, :], vbuf.at[slot], sems.at[slot, 1]))

    @pl.when(b == 0)
    def _():
        for cp in near_copies(0, 0):
            cp.start()

    slot = b % 2

    @pl.when(b + 1 < nb)
    def _():
        for cp in near_copies(b + 1, 1 - slot):
            cp.start()

    for cp in near_copies(b, slot):
        cp.wait()

    rh = lax.broadcasted_iota(jnp.int32, (gw, nq), 0) // SB_HD
    ch = lax.broadcasted_iota(jnp.int32, (gw, nq), 1) // t_new
    head_match = rh == ch
    new_limit = lax.broadcasted_iota(jnp.int32, (SUBLANE, nq), 1) % t_new

    def tile(kref, vref, nk, g, qbd, carry, acc, limit):
        hs = range(SB_GROUP * g, SB_GROUP * (g + 1))
        kp = jnp.concatenate([_load_perm_native(kref, 0, nk, h) for h in hs], axis=1).astype(BF16)
        vp = jnp.concatenate([_load_perm_native(vref, 0, nk, h) for h in hs], axis=1).astype(BF16)
        zt = _dot(kp, qbd)
        wt, carry = _sb_scan(zt, carry, limit)
        return carry, acc + _dot_tn(vp, wt.astype(BF16))

    qbds, carries, accs = [], [], []
    for g in range(ngroups):
        q = q_ref[:, g * gw:(g + 1) * gw].astype(F32)
        qt = jnp.concatenate([q] * SB_GROUP, axis=0).T
        qbd = jnp.where(head_match, qt, 0.0).astype(BF16)
        carry = jnp.zeros((1, nq), F32)
        acc = jnp.zeros((gw, nq), F32)
        carry, acc = tile(kn_ref, vn_ref, t_new, g, qbd, carry, acc, new_limit)
        carry, acc = tile(kbuf.at[slot], vbuf.at[slot], SB_TILE, g, qbd, carry, acc, None)
        qbds.append(qbd)
        carries.append(carry)
        accs.append(acc)

    def cond(st):
        t, carries, _ = st
        worst = functools.reduce(jnp.maximum, [jnp.max(c) for c in carries])
        return (t >= 0) & (worst > SB_CUTOFF_LOG2)

    def body(st):
        t, carries, accs = st
        rows = pl.ds(pl.multiple_of(t * tile_rows, tile_rows), tile_rows)
        far = (pltpu.make_async_copy(ck_ref.at[b, rows, :], kfar, far_sems.at[0]),
               pltpu.make_async_copy(cv_ref.at[b, rows, :], vfar, far_sems.at[1]))
        for cp in far:
            cp.start()
        for cp in far:
            cp.wait()
        res = [tile(kfar, vfar, SB_TILE, g, qbds[g], carries[g], accs[g], None) for g in range(ngroups)]
        return t - 1, [r[0] for r in res], [r[1] for r in res]

    _, _, accs = lax.while_loop(cond, body, (n_tiles - 2, carries, accs))
    for g in range(ngroups):
        at = accs[g].T
        for h in range(SB_GROUP):
            cols = slice(g * gw + h * SB_HD, g * gw + (h + 1) * SB_HD)
            o = at[h * t_new:(h + 1) * t_new, h * SB_HD:(h + 1) * SB_HD] * sb_ref[:, cols].astype(F32)
            o_ref[:, cols] = o.astype(BF16)


def _sb_sample(p_main, k5s, v5s, cache_k, cache_v, row0, dec_batch, t_new, past):
    rb0 = row0 // t_new
    tile_rows = SB_TILE * H_SB
    any_spec = pl.BlockSpec(memory_space=pl.ANY)
    new_spec = pl.BlockSpec((t_new * H_SB, SB_HD), lambda b: (b, 0))
    return pl.pallas_call(
        functools.partial(_sb_sample_kernel, t_new, past),
        grid=(dec_batch,),
        in_specs=[pl.BlockSpec((t_new, SB_W), lambda b: (rb0 + b, P_SQ // SB_W)),
                  new_spec, new_spec, any_spec, any_spec,
                  pl.BlockSpec((t_new, SB_W), lambda b: (rb0 + b, P_GB // SB_W))],
        out_specs=pl.BlockSpec((t_new, SB_W), lambda b: (b, 0)),
        out_shape=jax.ShapeDtypeStruct((dec_batch * t_new, SB_W), BF16),
        scratch_shapes=[
            pltpu.VMEM((2, tile_rows, SB_HD), F32),
            pltpu.VMEM((2, tile_rows, SB_HD), F32),
            pltpu.VMEM((tile_rows, SB_HD), F32),
            pltpu.VMEM((tile_rows, SB_HD), F32),
            pltpu.SemaphoreType.DMA((2, 2)),
            pltpu.SemaphoreType.DMA((2,)),
        ],
        compiler_params=pltpu.CompilerParams(
            dimension_semantics=("arbitrary",), vmem_limit_bytes=VMEM_LIMIT),
        name="sb_sample",
    )(p_main, k5s, v5s, cache_k, cache_v, p_main)


def _merge_kernel(n_prompt_tiles, ogp_ref, ogs_ref, obp_ref, obs_ref, xp_ref, xs_ref, w_ref, o_ref):
    m = pl.program_id(0)

    def run(og_ref, ob_ref, x_ref):
        mix = (og_ref[...].astype(F32) + ob_ref[...].astype(F32)).astype(BF16)
        o_ref[...] = x_ref[...] + _dot(mix, w_ref[...])

    @pl.when(m < n_prompt_tiles)
    def _():
        run(ogp_ref, obp_ref, xp_ref)

    @pl.when(m >= n_prompt_tiles)
    def _():
        run(ogs_ref, obs_ref, xs_ref)


def _merge(og_p, og_s, ob_p, ob_s, xp, xs, w_out):
    npt, nst = xp.shape[0] // TM, xs.shape[0] // TM
    pidx = lambda m: (jnp.minimum(m, npt - 1), 0)
    sidx = lambda m: (jnp.maximum(m - npt, 0), 0)
    row = lambda idx: pl.BlockSpec((TM, D_MODEL), idx)
    return pl.pallas_call(
        functools.partial(_merge_kernel, npt),
        grid=(npt + nst,),
        in_specs=[row(pidx), row(sidx), row(pidx), row(sidx), row(pidx), row(sidx),
                  pl.BlockSpec((D_MODEL, D_MODEL), lambda m: (0, 0))],
        out_specs=pl.BlockSpec((TM, D_MODEL), lambda m: (m, 0)),
        out_shape=jax.ShapeDtypeStruct((xp.shape[0] + xs.shape[0], D_MODEL), F32),
        compiler_params=pltpu.CompilerParams(
            dimension_semantics=("arbitrary",), vmem_limit_bytes=VMEM_LIMIT),
        name="merge_outproj",
    )(og_p, og_s, ob_p, ob_s, xp, xs, w_out)


def _ffn_kernel(n_prompt_tiles, x_ref, g2_ref, wu_ref, wd_ref, gf_ref, yp_ref, ys_ref, h_ref, acc_ref):
    m = pl.program_id(0)
    f = pl.program_id(1)

    @pl.when(f == 0)
    def _():
        h_ref[...] = _rms(x_ref[...], g2_ref[...]).astype(BF16)
        acc_ref[...] = jnp.zeros_like(acc_ref)

    u = jnp.maximum(_dot(h_ref[...], wu_ref[...]), 0.0)
    acc_ref[...] += _dot((u * u).astype(BF16), wd_ref[...])

    @pl.when(f == pl.num_programs(1) - 1)
    def _():
        y = _rms(x_ref[...] + acc_ref[...], gf_ref[...])

        @pl.when(m < n_prompt_tiles)
        def _():
            yp_ref[...] = y

        @pl.when(m >= n_prompt_tiles)
        def _():
            ys_ref[...] = y


def _ffn(x1, g2, w_up, w_down, gf, npr, nsr):
    npt, nst = npr // TM, nsr // TM
    return pl.pallas_call(
        functools.partial(_ffn_kernel, npt),
        grid=(npt + nst, D_FF // TF),
        in_specs=[
            pl.BlockSpec((TM, D_MODEL), lambda m, f: (m, 0)),
            pl.BlockSpec((1, D_MODEL), lambda m, f: (0, 0)),
            pl.BlockSpec((D_MODEL, TF), lambda m, f: (0, f)),
            pl.BlockSpec((TF, D_MODEL), lambda m, f: (f, 0)),
            pl.BlockSpec((1, D_MODEL), lambda m, f: (0, 0)),
        ],
        out_specs=[
            pl.BlockSpec((TM, D_MODEL), lambda m, f: (jnp.minimum(m, npt - 1), 0)),
            pl.BlockSpec((TM, D_MODEL), lambda m, f: (jnp.maximum(m - npt, 0), 0)),
        ],
        out_shape=[
            jax.ShapeDtypeStruct((npr, D_MODEL), F32),
            jax.ShapeDtypeStruct((nsr, D_MODEL), F32),
        ],
        scratch_shapes=[pltpu.VMEM((TM, D_MODEL), BF16), pltpu.VMEM((TM, D_MODEL), F32)],
        compiler_params=pltpu.CompilerParams(
            dimension_semantics=("arbitrary", "arbitrary"), vmem_limit_bytes=VMEM_LIMIT),
        name="ffn_final",
    )(x1, g2, w_up, w_down, gf)


def kernel(x_prompt, x_sample, cache_sb_k, cache_sb_v, state_gla, meta_tokens, norm1_g, w_in,
           w_alpha_up, b_alpha, gla_norm_g, w_out, norm2_g, w_up, w_down, norm_f_g):
    batch, seq, _ = x_prompt.shape
    dec_batch, t_new, _ = x_sample.shape
    depth, _, past = cache_sb_k.shape[:3]
    assert depth == 1 and w_in.shape[2] == _R_END
    assert seq % TM == 0 and (dec_batch * t_new) % TM == 0 and past % SB_TILE == 0
    assert t_new == SB_MINI and t_new % 16 == 0
    npr, nsr = batch * seq, dec_batch * t_new

    w_main, w_alow = _prep_w_in(w_in[0].T)
    wup = jnp.pad(w_alpha_up[0], ((0, LANE - GLA_RANK), (0, 0))).astype(BF16)
    bias = b_alpha[0].reshape(1, GLA_QK)
    gn = gla_norm_g[0].reshape(1, GLA_V)
    g1 = norm1_g[0].reshape(1, D_MODEL)
    g2 = norm2_g[0].reshape(1, D_MODEL)
    gf = norm_f_g.reshape(1, D_MODEL)
    w_out_b = w_out[0].astype(BF16)
    w_up_b = w_up[0].astype(BF16)
    w_down_b = w_down[0].astype(BF16)

    xp = x_prompt.reshape(npr, D_MODEL)
    xs = x_sample.reshape(nsr, D_MODEL)
    meta = meta_tokens.astype(x_prompt.dtype)

    (p_main, p_meta, km32, vm32, alow, alow_meta,
     k5p, v5p, k5s, v5s) = _inproj(xp, xs, meta, g1, w_main, w_alow, batch, seq)

    og_p, st_p = _gla_prompt(p_main, p_meta, alow, alow_meta, wup, bias, gn, batch, seq)
    og_s, st_s = _gla_sample(p_main, alow, state_gla[0], wup, bias, gn, npr, dec_batch, t_new)
    ob_p = _sb_prompt(p_main, km32, vm32, batch, seq)
    ck = cache_sb_k[0].reshape(dec_batch, past * H_SB, SB_HD)
    cv = cache_sb_v[0].reshape(dec_batch, past * H_SB, SB_HD)
    ob_s = _sb_sample(p_main, k5s, v5s, ck, cv, npr, dec_batch, t_new, past)

    x1 = _merge(og_p, og_s, ob_p, ob_s, xp, xs, w_out_b)
    y_p, y_s = _ffn(x1, g2, w_up_b, w_down_b, gf, npr, nsr)

    return (
        y_p.reshape(batch, seq, D_MODEL),
        y_s.reshape(dec_batch, t_new, D_MODEL),
        st_p[None],
        k5p.reshape(1, batch, N_META + seq, H_SB, SB_HD),
        v5p.reshape(1, batch, N_META + seq, H_SB, SB_HD),
        st_s[None],
        k5s.reshape(1, dec_batch, t_new, H_SB, SB_HD),
        v5s.reshape(1, dec_batch, t_new, H_SB, SB_HD),
    )
```

```python
import functools

import jax
import jax.numpy as jnp
from jax import lax
from jax.experimental import pallas as pl
from jax.experimental.pallas import tpu as pltpu

F32 = jnp.float32
BF16 = jnp.bfloat16

D_MODEL = 2048
N_META = 16
H_GLA = 4
GLA_DK = 256
GLA_DV = 512
GLA_RANK = 16
GLA_TAU = 16.0
GLA_CHUNK = 64
H_SB = 16
SB_HD = 128
D_FF = 4 * D_MODEL
EPS = 1e-5
GLA_QK = H_GLA * GLA_DK
GLA_V = H_GLA * GLA_DV
SB_W = H_SB * SB_HD

_R_GK = GLA_QK
_R_GV = 2 * GLA_QK
_R_ALOW = _R_GV + GLA_V
_R_SQ = _R_ALOW + GLA_RANK
_R_SK = _R_SQ + SB_W
_R_SV = _R_SK + SB_W
_R_GA = _R_SV + SB_W
_R_GB = _R_GA + GLA_V
_R_END = _R_GB + SB_W

P_GQ = 0
P_GK = P_GQ + GLA_QK
P_GV = P_GK + GLA_QK
P_SQ = P_GV + GLA_V
P_SK = P_SQ + SB_W
P_SV = P_SK + SB_W
P_GA = P_SV + SB_W
P_GB = P_GA + GLA_V
P_W = P_GB + SB_W

LANE = 128
SUBLANE = 8
TM = 512
TN = 2048
TF = 1024
GLA_ROWS = 256
SB_TILE = 256
SB_MINI = 32
SB_STRIDE = SB_MINI // SUBLANE
SB_CUTOFF = -104.0
LOG2_E = 1.4426950408889634
SB_SCALE_LOG2 = SB_HD ** -0.5 * LOG2_E
SB_CUTOFF_LOG2 = SB_CUTOFF * LOG2_E
VMEM_LIMIT = 56 * 1024 * 1024


def _dot(a, b):
    return jnp.dot(a, b, preferred_element_type=F32)


def _dot_nt(a, b):
    return lax.dot_general(a, b, (((1,), (1,)), ((), ())), preferred_element_type=F32)


def _dot_tn(a, b):
    return lax.dot_general(a, b, (((0,), (0,)), ((), ())), preferred_element_type=F32)


def _split2(x):
    hi = x.astype(BF16)
    lo = (x - hi.astype(F32)).astype(BF16)
    return hi, lo


def _rms(x, g):
    ms = jnp.mean(x * x, axis=-1, keepdims=True)
    return x * lax.rsqrt(ms + EPS) * g


def _log_sigmoid(x):
    return jnp.minimum(x, 0.0) - jnp.log(1.0 + jnp.exp(-jnp.abs(x)))


_PREP_ROWS = 512
_PREP_COLS = 1024


def _prep_w_kernel(n_aligned, wt_ref, edge_ref, o_ref, al_ref):
    n = pl.program_id(1)

    @pl.when(n < n_aligned)
    def _():
        o_ref[...] = wt_ref[...].T.astype(BF16)

    def shifted():
        return jnp.concatenate([wt_ref[GLA_RANK:, :], edge_ref[...]], axis=0).T

    is_sq = (n >= P_SQ // _PREP_COLS) & (n < P_SK // _PREP_COLS)

    @pl.when((n >= n_aligned) & jnp.logical_not(is_sq))
    def _():
        o_ref[...] = shifted().astype(BF16)

    @pl.when(is_sq)
    def _():
        o_ref[...] = (shifted() * SB_SCALE_LOG2).astype(BF16)

    @pl.when(n == n_aligned)
    def _():
        pad = jnp.zeros((LANE - GLA_RANK, wt_ref.shape[1]), F32)
        al_ref[...] = jnp.concatenate([wt_ref[0:GLA_RANK, :], pad], axis=0).T.astype(BF16)


def _prep_w_in(wt):
    n_aligned = _R_ALOW // _PREP_COLS
    edge_blocks = _PREP_COLS // GLA_RANK
    return pl.pallas_call(
        functools.partial(_prep_w_kernel, n_aligned),
        grid=(D_MODEL // _PREP_ROWS, P_W // _PREP_COLS),
        in_specs=[
            pl.BlockSpec((_PREP_COLS, _PREP_ROWS), lambda r, n: (n, r)),
            pl.BlockSpec((GLA_RANK, _PREP_ROWS), lambda r, n: ((jnp.maximum(n, n_aligned) + 1) * edge_blocks, r)),
        ],
        out_specs=[pl.BlockSpec((_PREP_ROWS, _PREP_COLS), lambda r, n: (r, n)),
                   pl.BlockSpec((_PREP_ROWS, LANE), lambda r, n: (r, 0))],
        out_shape=[jax.ShapeDtypeStruct((D_MODEL, P_W), BF16),
                   jax.ShapeDtypeStruct((D_MODEL, LANE), BF16)],
        compiler_params=pltpu.CompilerParams(
            dimension_semantics=("arbitrary", "arbitrary"), vmem_limit_bytes=VMEM_LIMIT),
        name="prep_w_in",
    )(wt, wt)


_N_SK0 = P_SK // TN
_N_SV0 = P_SV // TN
_N_GATE0 = P_GA // TN


assert TN == SB_W


def _permute_rows(vals, perm_ref):
    rows = vals.shape[0]
    cols = []
    for cb in range(TN // LANE):
        slab = perm_ref.at[cb % 2]
        slab[...] = vals[:, cb * LANE:(cb + 1) * LANE]
        parts = [slab[pl.ds(SB_MINI * j + c, SUBLANE, stride=SB_STRIDE), :]
                 for j in range(rows // SB_MINI) for c in range(SB_STRIDE)]
        cols.append(jnp.concatenate(parts, axis=0).astype(BF16))
    return jnp.concatenate(cols, axis=1)


MAX_CAST_SLABS = 64


def _inproj_kernel(n_prompt_tiles, batch, seq, n_cast_slabs, xp_ref, xs_ref, meta_ref, g1_ref, w_ref, wal_ref,
                   wo_ref, wu_ref, wd_ref,
                   p_ref, pm_ref, km_ref, vm_ref, al_ref, alm_ref,
                   kp_ref, vp_ref, ks_ref, vs_ref, wob_ref, wub_ref, wdb_ref,
                   h_ref, perm_ref, stage_ref, mstage_ref, sem, msem):
    m = pl.program_id(0)
    n = pl.program_id(1)
    n_tiles = pl.num_programs(0)
    tiles_per_batch = seq // TM

    @pl.when(m * pl.num_programs(1) + n < n_cast_slabs)
    def _():
        wob_ref[...] = wo_ref[...].astype(BF16)
        wub_ref[...] = wu_ref[...].astype(BF16)
        wdb_ref[...] = wd_ref[...].astype(BF16)

    def native_copies(src_ref, rows, dst_ref, token, dma_sem):
        return [pltpu.make_async_copy(src_ref.at[:, pl.ds(h * SB_HD, SB_HD)],
                                      dst_ref.at[pl.ds(token, rows), h, :], dma_sem)
                for h in range(H_SB)]

    def wait_native(pending):
        @pl.when(pending)
        def _():
            for cp in native_copies(stage_ref, TM, kp_ref, 0, sem):
                cp.wait()

    def write_native(main, dst_p, dst_s):
        stage_ref[...] = main

        @pl.when(m < n_prompt_tiles)
        def _():
            b = m // tiles_per_batch
            token = b * (seq + N_META) + N_META + (m - b * tiles_per_batch) * TM
            for cp in native_copies(stage_ref, TM, dst_p, token, sem):
                cp.start()

        @pl.when(m >= n_prompt_tiles)
        def _():
            for cp in native_copies(stage_ref, TM, dst_s, (m - n_prompt_tiles) * TM, sem):
                cp.start()

    def write_native_meta(met, dst_p):
        mstage_ref[...] = met
        copies = [cp for b in range(batch)
                  for cp in native_copies(mstage_ref, N_META, dst_p, b * (seq + N_META), msem)]
        for cp in copies:
            cp.start()
        for cp in copies:
            cp.wait()

    first = m == 0

    @pl.when(n == 0)
    def _():
        g = g1_ref[...]

        @pl.when(m < n_prompt_tiles)
        def _():
            h_ref[0:TM, :] = _rms(xp_ref[...], g).astype(BF16)

        @pl.when(m >= n_prompt_tiles)
        def _():
            h_ref[0:TM, :] = _rms(xs_ref[...], g).astype(BF16)

        al_ref[...] = _dot(h_ref[0:TM, :], wal_ref[...])

        @pl.when(first)
        def _():
            h_ref[TM:TM + N_META, :] = _rms(meta_ref[...], g).astype(BF16)
            alm_ref[...] = _dot(h_ref[TM:TM + N_META, :], wal_ref[...])

    is_gate = n >= _N_GATE0
    is_k = (n >= _N_SK0) & (n < _N_SV0)
    is_v = (n >= _N_SV0) & (n < _N_GATE0)

    def column_tile(kind, epilogue, meta_epilogue):
        @pl.when(kind & first)
        def _():
            acc = _dot(h_ref[...], w_ref[...])
            epilogue(acc[:TM])
            pm_ref[...] = acc[TM:].astype(BF16)
            meta_epilogue(acc[TM:])

        @pl.when(kind & jnp.logical_not(first))
        def _():
            epilogue(_dot(h_ref[0:TM, :], w_ref[...]))

    def plain(main):
        p_ref[...] = main.astype(BF16)

    def gate(main):
        p_ref[...] = jax.nn.sigmoid(main).astype(BF16)

    def kv(dst_p, dst_s):
        def epilogue(main):
            p_ref[...] = _permute_rows(main, perm_ref)
            write_native(main, dst_p, dst_s)
        return epilogue

    def kv_meta(f32_ref, dst_p):
        def epilogue(met):
            f32_ref[...] = met
            write_native_meta(met, dst_p)
        return epilogue

    @pl.when(is_k)
    def _():
        wait_native(m > 0)

    @pl.when(is_v)
    def _():
        wait_native(True)

    column_tile(jnp.logical_not(is_gate | is_k | is_v), plain, lambda met: None)
    column_tile(is_gate, gate, lambda met: None)
    column_tile(is_k, kv(kp_ref, ks_ref), kv_meta(km_ref, kp_ref))
    column_tile(is_v, kv(vp_ref, vs_ref), kv_meta(vm_ref, vp_ref))

    @pl.when((m == n_tiles - 1) & (n == pl.num_programs(1) - 1))
    def _():
        wait_native(True)


def _inproj(xp, xs, meta, g1, w_main, w_alow, w_out, w_up, w_down, batch, seq):
    npr, nsr = xp.shape[0], xs.shape[0]
    npt, nst = npr // TM, nsr // TM
    nt = npt + nst
    rows = npr + nsr
    nn = P_W // TN
    n_slabs = min(MAX_CAST_SLABS, 1 << ((nt * nn).bit_length() - 1))
    slab = lambda m, n: (jnp.minimum(m * nn + n, n_slabs - 1), 0)
    cast_specs = [pl.BlockSpec((w.shape[0] // n_slabs, w.shape[1]), slab) for w in (w_out, w_up, w_down)]
    kcol = lambda n: jnp.clip(n - _N_SK0, 0, SB_W // TN - 1)
    vcol = lambda n: jnp.clip(n - _N_SV0, 0, SB_W // TN - 1)
    once = lambda m, col, last: jnp.where(m == 0, col, last)
    any_spec = pl.BlockSpec(memory_space=pl.ANY)
    return pl.pallas_call(
        functools.partial(_inproj_kernel, npt, batch, seq, n_slabs),
        grid=(nt, nn),
        in_specs=[
            pl.BlockSpec((TM, D_MODEL), lambda m, n: (jnp.minimum(m, npt - 1), 0)),
            pl.BlockSpec((TM, D_MODEL), lambda m, n: (jnp.maximum(m - npt, 0), 0),
                         pipeline_mode=pl.Buffered(1)),
            pl.BlockSpec((N_META, D_MODEL), lambda m, n: (0, 0)),
            pl.BlockSpec((1, D_MODEL), lambda m, n: (0, 0)),
            pl.BlockSpec((D_MODEL, TN), lambda m, n: (0, n)),
            pl.BlockSpec((D_MODEL, LANE), lambda m, n: (0, 0)),
        ] + cast_specs,
        out_specs=[
            pl.BlockSpec((TM, TN), lambda m, n: (m, n)),
            pl.BlockSpec((N_META, TN), lambda m, n: (0, once(m, n, nn - 1))),
            pl.BlockSpec((N_META, TN), lambda m, n: (0, once(m, kcol(n), SB_W // TN - 1))),
            pl.BlockSpec((N_META, TN), lambda m, n: (0, once(m, vcol(n), SB_W // TN - 1))),
            pl.BlockSpec((TM, LANE), lambda m, n: (m, 0)),
            pl.BlockSpec((N_META, LANE), lambda m, n: (0, 0)),
            any_spec, any_spec, any_spec, any_spec,
        ] + cast_specs,
        out_shape=[
            jax.ShapeDtypeStruct((rows, P_W), BF16),
            jax.ShapeDtypeStruct((N_META, P_W), BF16),
            jax.ShapeDtypeStruct((N_META, SB_W), F32),
            jax.ShapeDtypeStruct((N_META, SB_W), F32),
            jax.ShapeDtypeStruct((rows, LANE), F32),
            jax.ShapeDtypeStruct((N_META, LANE), F32),
            jax.ShapeDtypeStruct((batch * (seq + N_META), H_SB, SB_HD), F32),
            jax.ShapeDtypeStruct((batch * (seq + N_META), H_SB, SB_HD), F32),
            jax.ShapeDtypeStruct((nsr, H_SB, SB_HD), F32),
            jax.ShapeDtypeStruct((nsr, H_SB, SB_HD), F32),
        ] + [jax.ShapeDtypeStruct(w.shape, BF16) for w in (w_out, w_up, w_down)],
        scratch_shapes=[
            pltpu.VMEM((TM + N_META, D_MODEL), BF16),
            pltpu.VMEM((2, TM, LANE), F32),
            pltpu.VMEM((TM, SB_W), F32),
            pltpu.VMEM((N_META, SB_W), F32),
            pltpu.SemaphoreType.DMA(()),
            pltpu.SemaphoreType.DMA(()),
        ],
        compiler_params=pltpu.CompilerParams(
            dimension_semantics=("arbitrary", "arbitrary"), vmem_limit_bytes=VMEM_LIMIT),
        name="inproj",
    )(xp, xs, meta, g1, w_main, w_alow, w_out, w_up, w_down)


def _gla_chunk(c, q, k, v, al, wup, bias, st_ref):
    x = _dot(al.astype(BF16), wup) + bias
    g = _log_sigmoid(x) * (1.0 / GLA_TAU)
    g_hi, g_lo = _split2(g)
    row = lax.broadcasted_iota(jnp.int32, (c, c), 0)
    col = lax.broadcasted_iota(jnp.int32, (c, c), 1)
    causal = col <= row
    tri = causal.astype(BF16)
    cum = _dot(tri, g_hi) + _dot(tri, g_lo)
    outs = []
    for h in range(H_GLA):
        ch = cum[:, h * GLA_DK:(h + 1) * GLA_DK]
        last = ch[c - 1:c, :]
        qh = q[:, h * GLA_DK:(h + 1) * GLA_DK].astype(F32)
        kh = k[:, h * GLA_DK:(h + 1) * GLA_DK].astype(F32)
        vh = v[:, h * GLA_DV:(h + 1) * GLA_DV]
        qd = (qh * jnp.exp(ch) * (GLA_DK ** -0.5)).astype(BF16)
        kd = (kh * jnp.exp(-ch)).astype(BF16)
        kr = (kh * jnp.exp(last - ch)).astype(BF16)
        att = jnp.where(causal, _dot_nt(qd, kd), 0.0).astype(BF16)
        st = st_ref[h]
        outs.append(_dot(att, vh) + _dot_nt(qd, st.astype(BF16)))
        st_ref[h] = st * jnp.exp(last) + _dot_tn(vh, kr)
    return outs


def _gla_finish(o, gn, sa):
    ms = jnp.mean(o * o, axis=-1, keepdims=True)
    return (o * lax.rsqrt(ms + EPS) * gn * sa.astype(F32)).astype(BF16)


def _gla_prompt_kernel(q_ref, k_ref, v_ref, sa_ref, al_ref, km_ref, vm_ref, alm_ref,
                       wup_ref, b_ref, gn_ref, og_ref, so_ref, st_ref):
    c_idx = pl.program_id(1)
    wup = wup_ref[...]
    bias = b_ref[...]

    @pl.when(c_idx == 0)
    def _():
        st_ref[...] = jnp.zeros_like(st_ref)
        zq = jnp.zeros((N_META, GLA_QK), BF16)
        _gla_chunk(N_META, zq, km_ref[...], vm_ref[...], alm_ref[...], wup, bias, st_ref)

    def body(i, carry):
        r0 = pl.multiple_of(i * GLA_CHUNK, GLA_CHUNK)
        rows = pl.ds(r0, GLA_CHUNK)
        outs = _gla_chunk(GLA_CHUNK, q_ref[rows, :], k_ref[rows, :], v_ref[rows, :], al_ref[rows, :],
                          wup, bias, st_ref)
        for h in range(H_GLA):
            cols = slice(h * GLA_DV, (h + 1) * GLA_DV)
            og_ref[rows, cols] = _gla_finish(outs[h], gn_ref[:, cols], sa_ref[rows, cols])
        return carry

    lax.fori_loop(0, GLA_ROWS // GLA_CHUNK, body, 0, unroll=True)

    @pl.when(c_idx == pl.num_programs(1) - 1)
    def _():
        for h in range(H_GLA):
            so_ref[0, h] = st_ref[h].T


def _gla_prompt(p_main, p_meta, alow, alow_meta, wup, bias, gn, batch, seq):
    nc = seq // GLA_ROWS
    rb = lambda b, c: b * nc + c
    const = lambda b, c: (0, 0)
    return pl.pallas_call(
        _gla_prompt_kernel,
        grid=(batch, nc),
        in_specs=[
            pl.BlockSpec((GLA_ROWS, GLA_QK), lambda b, c: (rb(b, c), P_GQ // GLA_QK)),
            pl.BlockSpec((GLA_ROWS, GLA_QK), lambda b, c: (rb(b, c), P_GK // GLA_QK)),
            pl.BlockSpec((GLA_ROWS, GLA_V), lambda b, c: (rb(b, c), P_GV // GLA_V)),
            pl.BlockSpec((GLA_ROWS, GLA_V), lambda b, c: (rb(b, c), P_GA // GLA_V)),
            pl.BlockSpec((GLA_ROWS, LANE), lambda b, c: (rb(b, c), 0)),
            pl.BlockSpec((N_META, GLA_QK), lambda b, c: (0, P_GK // GLA_QK)),
            pl.BlockSpec((N_META, GLA_V), lambda b, c: (0, P_GV // GLA_V)),
            pl.BlockSpec((N_META, LANE), const),
            pl.BlockSpec((LANE, GLA_QK), const),
            pl.BlockSpec((1, GLA_QK), const),
            pl.BlockSpec((1, GLA_V), const),
        ],
        out_specs=[
            pl.BlockSpec((GLA_ROWS, GLA_V), lambda b, c: (rb(b, c), 0)),
            pl.BlockSpec((1, H_GLA, GLA_DK, GLA_DV), lambda b, c: (b, 0, 0, 0)),
        ],
        out_shape=[
            jax.ShapeDtypeStruct((batch * seq, GLA_V), BF16),
            jax.ShapeDtypeStruct((batch, H_GLA, GLA_DK, GLA_DV), F32),
        ],
        scratch_shapes=[pltpu.VMEM((H_GLA, GLA_DV, GLA_DK), F32)],
        compiler_params=pltpu.CompilerParams(
            dimension_semantics=("arbitrary", "arbitrary"), vmem_limit_bytes=VMEM_LIMIT),
        name="gla_prompt",
    )(p_main, p_main, p_main, p_main, alow, p_meta, p_meta, alow_meta, wup, bias, gn)


def _gla_sample_kernel(t_new, q_ref, k_ref, v_ref, sa_ref, al_ref, s0_ref,
                       wup_ref, b_ref, gn_ref, og_ref, so_ref, st_ref):
    for h in range(H_GLA):
        st_ref[h] = s0_ref[0, h].T
    outs = _gla_chunk(t_new, q_ref[...], k_ref[...], v_ref[...], al_ref[...],
                      wup_ref[...], b_ref[...], st_ref)
    for h in range(H_GLA):
        cols = slice(h * GLA_DV, (h + 1) * GLA_DV)
        og_ref[:, cols] = _gla_finish(outs[h], gn_ref[:, cols], sa_ref[:, cols])
        so_ref[0, h] = st_ref[h].T


def _gla_sample(p_main, alow, state, wup, bias, gn, row0, dec_batch, t_new):
    rb0 = row0 // t_new
    const = lambda b: (0, 0)
    return pl.pallas_call(
        functools.partial(_gla_sample_kernel, t_new),
        grid=(dec_batch,),
        in_specs=[
            pl.BlockSpec((t_new, GLA_QK), lambda b: (rb0 + b, P_GQ // GLA_QK)),
            pl.BlockSpec((t_new, GLA_QK), lambda b: (rb0 + b, P_GK // GLA_QK)),
            pl.BlockSpec((t_new, GLA_V), lambda b: (rb0 + b, P_GV // GLA_V)),
            pl.BlockSpec((t_new, GLA_V), lambda b: (rb0 + b, P_GA // GLA_V)),
            pl.BlockSpec((t_new, LANE), lambda b: (rb0 + b, 0)),
            pl.BlockSpec((1, H_GLA, GLA_DK, GLA_DV), lambda b: (b, 0, 0, 0)),
            pl.BlockSpec((LANE, GLA_QK), const),
            pl.BlockSpec((1, GLA_QK), const),
            pl.BlockSpec((1, GLA_V), const),
        ],
        out_specs=[
            pl.BlockSpec((t_new, GLA_V), lambda b: (b, 0)),
            pl.BlockSpec((1, H_GLA, GLA_DK, GLA_DV), lambda b: (b, 0, 0, 0)),
        ],
        out_shape=[
            jax.ShapeDtypeStruct((dec_batch * t_new, GLA_V), BF16),
            jax.ShapeDtypeStruct((dec_batch, H_GLA, GLA_DK, GLA_DV), F32),
        ],
        scratch_shapes=[pltpu.VMEM((H_GLA, GLA_DV, GLA_DK), F32)],
        compiler_params=pltpu.CompilerParams(
            dimension_semantics=("arbitrary",), vmem_limit_bytes=VMEM_LIMIT),
        name="gla_sample",
    )(p_main, p_main, p_main, p_main, alow, state, wup, bias, gn)


def _load_perm(ref, base, nk):
    parts = []
    for j in range(nk // SB_MINI):
        for c in range(SB_STRIDE):
            parts.append(ref[pl.ds(base + SB_MINI * j + c, SUBLANE, stride=SB_STRIDE), :])
    return jnp.concatenate(parts, axis=0)


def _sb_scan(zt, carry, limit):
    nk, nq = zt.shape
    rowid = lax.broadcasted_iota(jnp.int32, (SUBLANE, nq), 0)
    w_parts = [None] * (nk // SUBLANE)
    for j in reversed(range(nk // SB_MINI)):
        ls, ss, vis = [], [], []
        for c in range(SB_STRIDE):
            i = j * SB_STRIDE + c
            z = zt[i * SUBLANE:(i + 1) * SUBLANE]
            nz = -z
            t = jnp.log2(1.0 + jnp.exp2(jnp.minimum(z, nz)))
            l = jnp.minimum(nz, 0.0) - t
            ss.append(l + z)
            if limit is not None:
                v = rowid * SB_STRIDE + (SB_MINI * j + c) < limit
                l = jnp.where(v, l, 0.0)
                vis.append(v)
            ls.append(l)
        later = [None] * SB_STRIDE
        run = ls[SB_STRIDE - 1]
        for c in range(SB_STRIDE - 2, -1, -1):
            later[c] = run
            run = run + ls[c]
        incl = run
        for sh in (1, 2, 4):
            incl = incl + jnp.where(rowid < SUBLANE - sh, pltpu.roll(incl, SUBLANE - sh, axis=0), 0.0)
        off = carry + (incl - run)
        for c in range(SB_STRIDE):
            after = off if later[c] is None else off + later[c]
            w = jnp.exp2(ss[c] + after)
            if limit is not None:
                w = jnp.where(vis[c], w, 0.0)
            w_parts[j * SB_STRIDE + c] = w
        carry = carry + incl[0:1, :]
    return jnp.concatenate(w_parts, axis=0), carry


SB_PAR = 4


def _sb_prompt_kernel(seq, q_ref, k_ref, v_ref, km_ref, vm_ref, sb_ref, o_ref, kms_ref, vms_ref):
    kms_ref[...] = jnp.zeros_like(kms_ref)
    vms_ref[...] = jnp.zeros_like(vms_ref)
    for i in range(SB_PAR):
        kms_ref[i, 0:N_META, :] = km_ref[:, i * SB_HD:(i + 1) * SB_HD]
        vms_ref[i, 0:N_META, :] = vm_ref[:, i * SB_HD:(i + 1) * SB_HD]
    heads = range(SB_PAR)
    diag_limit = lax.broadcasted_iota(jnp.int32, (SUBLANE, SB_TILE), 1)

    def tile(kp, vp, q, carry, acc, limit):
        zt = _dot_nt(kp, q)
        wt, carry = _sb_scan(zt, carry, limit)
        return carry, acc + _dot_tn(vp, wt.astype(BF16))

    def prompt_tiles(base, qs, carries, accs, limit):
        rows = pl.ds(base, SB_TILE)
        res = [tile(k_ref[rows, i * SB_HD:(i + 1) * SB_HD], v_ref[rows, i * SB_HD:(i + 1) * SB_HD],
                    qs[i], carries[i], accs[i], limit) for i in heads]
        return [r[0] for r in res], [r[1] for r in res]

    def plain_tiles(t, qs, carries, accs):
        return prompt_tiles(pl.multiple_of(t * SB_TILE, SB_TILE), qs, carries, accs, None)

    def meta_tiles(qs, carries, accs):
        res = [tile(_load_perm(kms_ref.at[i], 0, SB_MINI).astype(BF16),
                    _load_perm(vms_ref.at[i], 0, SB_MINI).astype(BF16),
                    qs[i], carries[i], accs[i], N_META) for i in heads]
        return [r[0] for r in res], [r[1] for r in res]

    def qblock(m, is_first):
        row0 = pl.multiple_of(m * SB_TILE, SB_TILE)
        rows = pl.ds(row0, SB_TILE)
        qs = [q_ref[rows, i * SB_HD:(i + 1) * SB_HD] for i in heads]
        carry0 = [jnp.zeros((1, SB_TILE), F32)] * SB_PAR
        acc0 = [jnp.zeros((SB_HD, SB_TILE), F32)] * SB_PAR
        carries, accs = prompt_tiles(row0, qs, carry0, acc0, diag_limit)
        if is_first:
            _, accs = meta_tiles(qs, carries, accs)
        else:
            carries, accs = plain_tiles(m - 1, qs, carries, accs)

            def cond(st):
                t, carries, _ = st
                worst = functools.reduce(jnp.maximum, [jnp.max(c) for c in carries])
                return (t >= -1) & (worst > SB_CUTOFF_LOG2)

            def body(st):
                t, carries, accs = st
                carries, accs = lax.cond(t >= 0,
                                         lambda: plain_tiles(t, qs, carries, accs),
                                         lambda: meta_tiles(qs, carries, accs))
                return t - 1, carries, accs

            _, _, accs = lax.while_loop(cond, body, (m - 2, carries, accs))
        for i in heads:
            cols = slice(i * SB_HD, (i + 1) * SB_HD)
            o = accs[i].T * sb_ref[rows, cols].astype(F32)
            o_ref[rows, cols] = o.astype(BF16)

    qblock(0, True)

    def loop_body(m, carry):
        qblock(m, False)
        return carry

    lax.fori_loop(1, seq // SB_TILE, loop_body, 0)


def _sb_prompt(p_main, km32, vm32, batch, seq):
    gw = SB_PAR * SB_HD
    col = lambda off: pl.BlockSpec((seq, gw), lambda b, g: (b, off // gw + g))
    return pl.pallas_call(
        functools.partial(_sb_prompt_kernel, seq),
        grid=(batch, H_SB // SB_PAR),
        in_specs=[col(P_SQ), col(P_SK), col(P_SV),
                  pl.BlockSpec((N_META, gw), lambda b, g: (0, g)),
                  pl.BlockSpec((N_META, gw), lambda b, g: (0, g)),
                  col(P_GB)],
        out_specs=pl.BlockSpec((seq, gw), lambda b, g: (b, g)),
        out_shape=jax.ShapeDtypeStruct((batch * seq, SB_W), BF16),
        scratch_shapes=[pltpu.VMEM((SB_PAR, SB_MINI, SB_HD), F32), pltpu.VMEM((SB_PAR, SB_MINI, SB_HD), F32)],
        compiler_params=pltpu.CompilerParams(
            dimension_semantics=("arbitrary", "arbitrary"), vmem_limit_bytes=VMEM_LIMIT),
        name="sb_prompt",
    )(p_main, p_main, p_main, km32, vm32, p_main)


SB_GROUP = 4


def _load_perm_native(ref, key0, nk, head):
    parts = []
    for j in range(nk // SB_MINI):
        for c in range(SB_STRIDE):
            start = (key0 + SB_MINI * j + c) * H_SB + head
            parts.append(ref[pl.ds(start, SUBLANE, stride=SB_STRIDE * H_SB), :])
    return jnp.concatenate(parts, axis=0)


def _sb_sample_kernel(t_new, past, q_ref, kn_ref, vn_ref, ck_ref, cv_ref, sb_ref, o_ref,
                      kbuf, vbuf, kfar, vfar, sems, far_sems):
    b = pl.program_id(0)
    nb = pl.num_programs(0)
    gw = SB_GROUP * SB_HD
    nq = SB_GROUP * t_new
    ngroups = H_SB // SB_GROUP
    n_tiles = past // SB_TILE
    tile_rows = SB_TILE * H_SB

    def near_copies(bb, slot):
        rows = pl.ds((n_tiles - 1) * tile_rows, tile_rows)
        return (pltpu.make_async_copy(ck_ref.at[bb, rows, :], kbuf.at[slot], sems.at[slot, 0]),
                pltpu.make_async_copy(cv_ref.at[bb, rows, :], vbuf.at[slot], sems.at[slot, 1]))

    @pl.when(b == 0)
    def _():
        for cp in near_copies(0, 0):
            cp.start()

    slot = b % 2

    @pl.when(b + 1 < nb)
    def _():
        for cp in near_copies(b + 1, 1 - slot):
            cp.start()

    for cp in near_copies(b, slot):
        cp.wait()

    rh = lax.broadcasted_iota(jnp.int32, (gw, nq), 0) // SB_HD
    ch = lax.broadcasted_iota(jnp.int32, (gw, nq), 1) // t_new
    head_match = rh == ch
    new_limit = lax.broadcasted_iota(jnp.int32, (SUBLANE, nq), 1) % t_new

    def tile(kref, vref, nk, g, qbd, carry, acc, limit):
        hs = range(SB_GROUP * g, SB_GROUP * (g + 1))
        kp = jnp.concatenate([_load_perm_native(kref, 0, nk, h) for h in hs], axis=1).astype(BF16)
        vp = jnp.concatenate([_load_perm_native(vref, 0, nk, h) for h in hs], axis=1).astype(BF16)
        zt = _dot(kp, qbd)
        wt, carry = _sb_scan(zt, carry, limit)
        return carry, acc + _dot_tn(vp, wt.astype(BF16))

    qbds, carries, accs = [], [], []
    for g in range(ngroups):
        q = q_ref[:, g * gw:(g + 1) * gw].astype(F32)
        qt = jnp.concatenate([q] * SB_GROUP, axis=0).T
        qbd = jnp.where(head_match, qt, 0.0).astype(BF16)
        carry = jnp.zeros((1, nq), F32)
        acc = jnp.zeros((gw, nq), F32)
        carry, acc = tile(kn_ref, vn_ref, t_new, g, qbd, carry, acc, new_limit)
        carry, acc = tile(kbuf.at[slot], vbuf.at[slot], SB_TILE, g, qbd, carry, acc, None)
        qbds.append(qbd)
        carries.append(carry)
        accs.append(acc)

    def cond(st):
        t, carries, _ = st
        worst = functools.reduce(jnp.maximum, [jnp.max(c) for c in carries])
        return (t >= 0) & (worst > SB_CUTOFF_LOG2)

    def body(st):
        t, carries, accs = st
        rows = pl.ds(pl.multiple_of(t * tile_rows, tile_rows), tile_rows)
        far = (pltpu.make_async_copy(ck_ref.at[b, rows, :], kfar, far_sems.at[0]),
               pltpu.make_async_copy(cv_ref.at[b, rows, :], vfar, far_sems.at[1]))
        for cp in far:
            cp.start()
        for cp in far:
            cp.wait()
        res = [tile(kfar, vfar, SB_TILE, g, qbds[g], carries[g], accs[g], None) for g in range(ngroups)]
        return t - 1, [r[0] for r in res], [r[1] for r in res]

    _, _, accs = lax.while_loop(cond, body, (n_tiles - 2, carries, accs))
    for g in range(ngroups):
        at = accs[g].T
        for h in range(SB_GROUP):
            cols = slice(g * gw + h * SB_HD, g * gw + (h + 1) * SB_HD)
            o = at[h * t_new:(h + 1) * t_new, h * SB_HD:(h + 1) * SB_HD] * sb_ref[:, cols].astype(F32)
            o_ref[:, cols] = o.astype(BF16)


def _sb_sample(p_main, k5s, v5s, cache_k, cache_v, row0, dec_batch, t_new, past):
    rb0 = row0 // t_new
    tile_rows = SB_TILE * H_SB
    any_spec = pl.BlockSpec(memory_space=pl.ANY)
    new_spec = pl.BlockSpec((t_new * H_SB, SB_HD), lambda b: (b, 0))
    return pl.pallas_call(
        functools.partial(_sb_sample_kernel, t_new, past),
        grid=(dec_batch,),
        in_specs=[pl.BlockSpec((t_new, SB_W), lambda b: (rb0 + b, P_SQ // SB_W)),
                  new_spec, new_spec, any_spec, any_spec,
                  pl.BlockSpec((t_new, SB_W), lambda b: (rb0 + b, P_GB // SB_W))],
        out_specs=pl.BlockSpec((t_new, SB_W), lambda b: (b, 0)),
        out_shape=jax.ShapeDtypeStruct((dec_batch * t_new, SB_W), BF16),
        scratch_shapes=[
            pltpu.VMEM((2, tile_rows, SB_HD), F32),
            pltpu.VMEM((2, tile_rows, SB_HD), F32),
            pltpu.VMEM((tile_rows, SB_HD), F32),
            pltpu.VMEM((tile_rows, SB_HD), F32),
            pltpu.SemaphoreType.DMA((2, 2)),
            pltpu.SemaphoreType.DMA((2,)),
        ],
        compiler_params=pltpu.CompilerParams(
            dimension_semantics=("arbitrary",), vmem_limit_bytes=VMEM_LIMIT),
        name="sb_sample",
    )(p_main, k5s, v5s, cache_k, cache_v, p_main)


def _merge_kernel(n_prompt_tiles, ogp_ref, ogs_ref, obp_ref, obs_ref, xp_ref, xs_ref, w_ref, o_ref):
    m = pl.program_id(0)

    def run(og_ref, ob_ref, x_ref):
        mix = (og_ref[...].astype(F32) + ob_ref[...].astype(F32)).astype(BF16)
        o_ref[...] = x_ref[...] + _dot(mix, w_ref[...])

    @pl.when(m < n_prompt_tiles)
    def _():
        run(ogp_ref, obp_ref, xp_ref)

    @pl.when(m >= n_prompt_tiles)
    def _():
        run(ogs_ref, obs_ref, xs_ref)


def _merge(og_p, og_s, ob_p, ob_s, xp, xs, w_out):
    npt, nst = xp.shape[0] // TM, xs.shape[0] // TM
    pidx = lambda m: (jnp.minimum(m, npt - 1), 0)
    sidx = lambda m: (jnp.maximum(m - npt, 0), 0)
    row = lambda idx: pl.BlockSpec((TM, D_MODEL), idx)
    return pl.pallas_call(
        functools.partial(_merge_kernel, npt),
        grid=(npt + nst,),
        in_specs=[row(pidx), row(sidx), row(pidx), row(sidx), row(pidx), row(sidx),
                  pl.BlockSpec((D_MODEL, D_MODEL), lambda m: (0, 0))],
        out_specs=pl.BlockSpec((TM, D_MODEL), lambda m: (m, 0)),
        out_shape=jax.ShapeDtypeStruct((xp.shape[0] + xs.shape[0], D_MODEL), F32),
        compiler_params=pltpu.CompilerParams(
            dimension_semantics=("arbitrary",), vmem_limit_bytes=VMEM_LIMIT),
        name="merge_outproj",
    )(og_p, og_s, ob_p, ob_s, xp, xs, w_out)


def _ffn_kernel(n_prompt_tiles, x_ref, g2_ref, wu_ref, wd_ref, gf_ref, yp_ref, ys_ref, h_ref, acc_ref):
    m = pl.program_id(0)
    f = pl.program_id(1)

    @pl.when(f == 0)
    def _():
        h_ref[...] = _rms(x_ref[...], g2_ref[...]).astype(BF16)
        acc_ref[...] = jnp.zeros_like(acc_ref)

    u = jnp.maximum(_dot(h_ref[...], wu_ref[...]), 0.0)
    acc_ref[...] += _dot((u * u).astype(BF16), wd_ref[...])

    @pl.when(f == pl.num_programs(1) - 1)
    def _():
        y = _rms(x_ref[...] + acc_ref[...], gf_ref[...])

        @pl.when(m < n_prompt_tiles)
        def _():
            yp_ref[...] = y

        @pl.when(m >= n_prompt_tiles)
        def _():
            ys_ref[...] = y


def _ffn(x1, g2, w_up, w_down, gf, npr, nsr):
    npt, nst = npr // TM, nsr // TM
    return pl.pallas_call(
        functools.partial(_ffn_kernel, npt),
        grid=(npt + nst, D_FF // TF),
        in_specs=[
            pl.BlockSpec((TM, D_MODEL), lambda m, f: (m, 0)),
            pl.BlockSpec((1, D_MODEL), lambda m, f: (0, 0)),
            pl.BlockSpec((D_MODEL, TF), lambda m, f: (0, f)),
            pl.BlockSpec((TF, D_MODEL), lambda m, f: (f, 0)),
            pl.BlockSpec((1, D_MODEL), lambda m, f: (0, 0)),
        ],
        out_specs=[
            pl.BlockSpec((TM, D_MODEL), lambda m, f: (jnp.minimum(m, npt - 1), 0)),
            pl.BlockSpec((TM, D_MODEL), lambda m, f: (jnp.maximum(m - npt, 0), 0)),
        ],
        out_shape=[
            jax.ShapeDtypeStruct((npr, D_MODEL), F32),
            jax.ShapeDtypeStruct((nsr, D_MODEL), F32),
        ],
        scratch_shapes=[pltpu.VMEM((TM, D_MODEL), BF16), pltpu.VMEM((TM, D_MODEL), F32)],
        compiler_params=pltpu.CompilerParams(
            dimension_semantics=("arbitrary", "arbitrary"), vmem_limit_bytes=VMEM_LIMIT),
        name="ffn_final",
    )(x1, g2, w_up, w_down, gf)


def kernel(x_prompt, x_sample, cache_sb_k, cache_sb_v, state_gla, meta_tokens, norm1_g, w_in,
           w_alpha_up, b_alpha, gla_norm_g, w_out, norm2_g, w_up, w_down, norm_f_g):
    batch, seq, _ = x_prompt.shape
    dec_batch, t_new, _ = x_sample.shape
    depth, _, past = cache_sb_k.shape[:3]
    assert depth == 1 and w_in.shape[2] == _R_END
    assert seq % TM == 0 and (dec_batch * t_new) % TM == 0 and past % SB_TILE == 0
    assert t_new == SB_MINI and t_new % 16 == 0
    npr, nsr = batch * seq, dec_batch * t_new

    w_main, w_alow = _prep_w_in(w_in[0].T)
    wup = jnp.pad(w_alpha_up[0], ((0, LANE - GLA_RANK), (0, 0))).astype(BF16)
    bias = b_alpha[0].reshape(1, GLA_QK)
    gn = gla_norm_g[0].reshape(1, GLA_V)
    g1 = norm1_g[0].reshape(1, D_MODEL)
    g2 = norm2_g[0].reshape(1, D_MODEL)
    gf = norm_f_g.reshape(1, D_MODEL)

    xp = x_prompt.reshape(npr, D_MODEL)
    xs = x_sample.reshape(nsr, D_MODEL)
    meta = meta_tokens.astype(x_prompt.dtype)

    (p_main, p_meta, km32, vm32, alow, alow_meta, k5p, v5p, k5s, v5s,
     w_out_b, w_up_b, w_down_b) = _inproj(xp, xs, meta, g1, w_main, w_alow,
                                          w_out[0], w_up[0], w_down[0], batch, seq)

    og_p, st_p = _gla_prompt(p_main, p_meta, alow, alow_meta, wup, bias, gn, batch, seq)
    og_s, st_s = _gla_sample(p_main, alow, state_gla[0], wup, bias, gn, npr, dec_batch, t_new)
    ob_p = _sb_prompt(p_main, km32, vm32, batch, seq)
    ck = cache_sb_k[0].reshape(dec_batch, past * H_SB, SB_HD)
    cv = cache_sb_v[0].reshape(dec_batch, past * H_SB, SB_HD)
    ob_s = _sb_sample(p_main, k5s.reshape(nsr * H_SB, SB_HD), v5s.reshape(nsr * H_SB, SB_HD),
                      ck, cv, npr, dec_batch, t_new, past)

    x1 = _merge(og_p, og_s, ob_p, ob_s, xp, xs, w_out_b)
    y_p, y_s = _ffn(x1, g2, w_up_b, w_down_b, gf, npr, nsr)

    return (
        y_p.reshape(batch, seq, D_MODEL),
        y_s.reshape(dec_batch, t_new, D_MODEL),
        st_p[None],
        k5p.reshape(1, batch, N_META + seq, H_SB, SB_HD),
        v5p.reshape(1, batch, N_META + seq, H_SB, SB_HD),
        st_s[None],
        k5s.reshape(1, dec_batch, t_new, H_SB, SB_HD),
        v5s.reshape(1, dec_batch, t_new, H_SB, SB_HD),
    )
```

```python
import functools

import jax
import jax.numpy as jnp
from jax import lax
from jax.experimental import pallas as pl
from jax.experimental.pallas import tpu as pltpu

F32 = jnp.float32
BF16 = jnp.bfloat16

D_MODEL = 2048
N_META = 16
H_GLA = 4
GLA_DK = 256
GLA_DV = 512
GLA_RANK = 16
GLA_TAU = 16.0
GLA_CHUNK = 64
H_SB = 16
SB_HD = 128
D_FF = 4 * D_MODEL
EPS = 1e-5
GLA_QK = H_GLA * GLA_DK
GLA_V = H_GLA * GLA_DV
SB_W = H_SB * SB_HD

_R_GK = GLA_QK
_R_GV = 2 * GLA_QK
_R_ALOW = _R_GV + GLA_V
_R_SQ = _R_ALOW + GLA_RANK
_R_SK = _R_SQ + SB_W
_R_SV = _R_SK + SB_W
_R_GA = _R_SV + SB_W
_R_GB = _R_GA + GLA_V
_R_END = _R_GB + SB_W

P_GQ = 0
P_GK = P_GQ + GLA_QK
P_GV = P_GK + GLA_QK
P_SQ = P_GV + GLA_V
P_SK = P_SQ + SB_W
P_SV = P_SK + SB_W
P_GA = P_SV + SB_W
P_GB = P_GA + GLA_V
P_W = P_GB + SB_W

LANE = 128
SUBLANE = 8
TM = 512
TN = 2048
TF = 1024
GLA_ROWS = 256
SB_TILE = 256
SB_MINI = 32
SB_STRIDE = SB_MINI // SUBLANE
SB_CUTOFF = -104.0
LOG2_E = 1.4426950408889634
SB_SCALE_LOG2 = SB_HD ** -0.5 * LOG2_E
SB_CUTOFF_LOG2 = SB_CUTOFF * LOG2_E
VMEM_LIMIT = 56 * 1024 * 1024


def _dot(a, b):
    return jnp.dot(a, b, preferred_element_type=F32)


def _dot_nt(a, b):
    return lax.dot_general(a, b, (((1,), (1,)), ((), ())), preferred_element_type=F32)


def _dot_tn(a, b):
    return lax.dot_general(a, b, (((0,), (0,)), ((), ())), preferred_element_type=F32)


def _split2(x):
    hi = x.astype(BF16)
    lo = (x - hi.astype(F32)).astype(BF16)
    return hi, lo


def _rms(x, g):
    ms = jnp.mean(x * x, axis=-1, keepdims=True)
    return x * lax.rsqrt(ms + EPS) * g


def _log_sigmoid(x):
    return jnp.minimum(x, 0.0) - jnp.log(1.0 + jnp.exp(-jnp.abs(x)))


_PREP_ROWS = 1024
_PREP_COLS = 1024


def _prep_w_kernel(n_aligned, wt_ref, edge_ref, o_ref, al_ref):
    n = pl.program_id(1)

    @pl.when(n < n_aligned)
    def _():
        o_ref[...] = wt_ref[...].T.astype(BF16)

    def shifted():
        return jnp.concatenate([wt_ref[GLA_RANK:, :], edge_ref[...]], axis=0).T

    is_sq = (n >= P_SQ // _PREP_COLS) & (n < P_SK // _PREP_COLS)

    @pl.when((n >= n_aligned) & jnp.logical_not(is_sq))
    def _():
        o_ref[...] = shifted().astype(BF16)

    @pl.when(is_sq)
    def _():
        o_ref[...] = (shifted() * SB_SCALE_LOG2).astype(BF16)

    @pl.when(n == n_aligned)
    def _():
        pad = jnp.zeros((LANE - GLA_RANK, wt_ref.shape[1]), F32)
        al_ref[...] = jnp.concatenate([wt_ref[0:GLA_RANK, :], pad], axis=0).T.astype(BF16)


def _prep_w_in(wt):
    n_aligned = _R_ALOW // _PREP_COLS
    edge_blocks = _PREP_COLS // GLA_RANK
    return pl.pallas_call(
        functools.partial(_prep_w_kernel, n_aligned),
        grid=(D_MODEL // _PREP_ROWS, P_W // _PREP_COLS),
        in_specs=[
            pl.BlockSpec((_PREP_COLS, _PREP_ROWS), lambda r, n: (n, r)),
            pl.BlockSpec((GLA_RANK, _PREP_ROWS), lambda r, n: ((jnp.maximum(n, n_aligned) + 1) * edge_blocks, r)),
        ],
        out_specs=[pl.BlockSpec((_PREP_ROWS, _PREP_COLS), lambda r, n: (r, n)),
                   pl.BlockSpec((_PREP_ROWS, LANE), lambda r, n: (r, 0))],
        out_shape=[jax.ShapeDtypeStruct((D_MODEL, P_W), BF16),
                   jax.ShapeDtypeStruct((D_MODEL, LANE), BF16)],
        compiler_params=pltpu.CompilerParams(
            dimension_semantics=("arbitrary", "arbitrary"), vmem_limit_bytes=VMEM_LIMIT),
        name="prep_w_in",
    )(wt, wt)


_N_SK0 = P_SK // TN
_N_SV0 = P_SV // TN
_N_GATE0 = P_GA // TN


assert TN == SB_W


def _permute_rows(stage_ref):
    heads, rows, _ = stage_ref.shape
    cols = []
    for h in range(heads):
        parts = [stage_ref[h, pl.ds(SB_MINI * j + c, SUBLANE, stride=SB_STRIDE), :]
                 for j in range(rows // SB_MINI) for c in range(SB_STRIDE)]
        cols.append(jnp.concatenate(parts, axis=0).astype(BF16))
    return jnp.concatenate(cols, axis=1)


MAX_CAST_SLABS = 64


def _inproj_kernel(n_prompt_tiles, batch, seq, n_cast_slabs, xp_ref, xs_ref, meta_ref, g1_ref, w_ref, wal_ref,
                   wo_ref, wu_ref, wd_ref,
                   p_ref, pm_ref, km_ref, vm_ref, al_ref, alm_ref,
                   kp_ref, vp_ref, ks_ref, vs_ref, wob_ref, wub_ref, wdb_ref,
                   h_ref, stage_ref, mstage_ref, sem, msem):
    m = pl.program_id(0)
    n = pl.program_id(1)
    n_tiles = pl.num_programs(0)
    tiles_per_batch = seq // TM

    @pl.when(m * pl.num_programs(1) + n < n_cast_slabs)
    def _():
        wob_ref[...] = wo_ref[...].astype(BF16)
        wub_ref[...] = wu_ref[...].astype(BF16)
        wdb_ref[...] = wd_ref[...].astype(BF16)

    def stage_heads(ref, vals):
        for h in range(H_SB):
            ref[h] = vals[:, h * SB_HD:(h + 1) * SB_HD]

    def native_copies(src_ref, rows, dst_ref, token, dma_sem):
        return [pltpu.make_async_copy(src_ref.at[h], dst_ref.at[pl.ds(token, rows), h, :], dma_sem)
                for h in range(H_SB)]

    def wait_native(pending):
        @pl.when(pending)
        def _():
            for cp in native_copies(stage_ref, TM, kp_ref, 0, sem):
                cp.wait()

    def write_native(dst_p, dst_s):
        @pl.when(m < n_prompt_tiles)
        def _():
            b = m // tiles_per_batch
            token = b * (seq + N_META) + N_META + (m - b * tiles_per_batch) * TM
            for cp in native_copies(stage_ref, TM, dst_p, token, sem):
                cp.start()

        @pl.when(m >= n_prompt_tiles)
        def _():
            for cp in native_copies(stage_ref, TM, dst_s, (m - n_prompt_tiles) * TM, sem):
                cp.start()

    def write_native_meta(met, dst_p):
        stage_heads(mstage_ref, met)
        copies = [cp for b in range(batch)
                  for cp in native_copies(mstage_ref, N_META, dst_p, b * (seq + N_META), msem)]
        for cp in copies:
            cp.start()
        for cp in copies:
            cp.wait()

    first = m == 0

    @pl.when(n == 0)
    def _():
        g = g1_ref[...]

        @pl.when(m < n_prompt_tiles)
        def _():
            h_ref[0:TM, :] = _rms(xp_ref[...], g).astype(BF16)

        @pl.when(m >= n_prompt_tiles)
        def _():
            h_ref[0:TM, :] = _rms(xs_ref[...], g).astype(BF16)

        al_ref[...] = _dot(h_ref[0:TM, :], wal_ref[...])

        @pl.when(first)
        def _():
            h_ref[TM:TM + N_META, :] = _rms(meta_ref[...], g).astype(BF16)
            alm_ref[...] = _dot(h_ref[TM:TM + N_META, :], wal_ref[...])

    is_gate = n >= _N_GATE0
    is_k = (n >= _N_SK0) & (n < _N_SV0)
    is_v = (n >= _N_SV0) & (n < _N_GATE0)

    def column_tile(kind, epilogue, meta_epilogue):
        @pl.when(kind & first)
        def _():
            acc = _dot(h_ref[...], w_ref[...])
            epilogue(acc[:TM])
            pm_ref[...] = acc[TM:].astype(BF16)
            meta_epilogue(acc[TM:])

        @pl.when(kind & jnp.logical_not(first))
        def _():
            epilogue(_dot(h_ref[0:TM, :], w_ref[...]))

    def plain(main):
        p_ref[...] = main.astype(BF16)

    def gate(main):
        p_ref[...] = jax.nn.sigmoid(main).astype(BF16)

    def kv(dst_p, dst_s):
        def epilogue(main):
            stage_heads(stage_ref, main)
            write_native(dst_p, dst_s)
            p_ref[...] = _permute_rows(stage_ref)
        return epilogue

    def kv_meta(f32_ref, dst_p):
        def epilogue(met):
            f32_ref[...] = met
            write_native_meta(met, dst_p)
        return epilogue

    @pl.when(is_k)
    def _():
        wait_native(m > 0)

    @pl.when(is_v)
    def _():
        wait_native(True)

    column_tile(jnp.logical_not(is_gate | is_k | is_v), plain, lambda met: None)
    column_tile(is_gate, gate, lambda met: None)
    column_tile(is_k, kv(kp_ref, ks_ref), kv_meta(km_ref, kp_ref))
    column_tile(is_v, kv(vp_ref, vs_ref), kv_meta(vm_ref, vp_ref))

    @pl.when((m == n_tiles - 1) & (n == pl.num_programs(1) - 1))
    def _():
        wait_native(True)


def _inproj(xp, xs, meta, g1, w_main, w_alow, w_out, w_up, w_down, batch, seq):
    npr, nsr = xp.shape[0], xs.shape[0]
    npt, nst = npr // TM, nsr // TM
    nt = npt + nst
    rows = npr + nsr
    nn = P_W // TN
    n_slabs = min(MAX_CAST_SLABS, 1 << ((nt * nn).bit_length() - 1))
    slab = lambda m, n: (jnp.minimum(m * nn + n, n_slabs - 1), 0)
    cast_specs = [pl.BlockSpec((w.shape[0] // n_slabs, w.shape[1]), slab) for w in (w_out, w_up, w_down)]
    kcol = lambda n: jnp.clip(n - _N_SK0, 0, SB_W // TN - 1)
    vcol = lambda n: jnp.clip(n - _N_SV0, 0, SB_W // TN - 1)
    once = lambda m, col, last: jnp.where(m == 0, col, last)
    any_spec = pl.BlockSpec(memory_space=pl.ANY)
    return pl.pallas_call(
        functools.partial(_inproj_kernel, npt, batch, seq, n_slabs),
        grid=(nt, nn),
        in_specs=[
            pl.BlockSpec((TM, D_MODEL), lambda m, n: (jnp.minimum(m, npt - 1), 0)),
            pl.BlockSpec((TM, D_MODEL), lambda m, n: (jnp.maximum(m - npt, 0), 0),
                         pipeline_mode=pl.Buffered(1)),
            pl.BlockSpec((N_META, D_MODEL), lambda m, n: (0, 0)),
            pl.BlockSpec((1, D_MODEL), lambda m, n: (0, 0)),
            pl.BlockSpec((D_MODEL, TN), lambda m, n: (0, n)),
            pl.BlockSpec((D_MODEL, LANE), lambda m, n: (0, 0)),
        ] + cast_specs,
        out_specs=[
            pl.BlockSpec((TM, TN), lambda m, n: (m, n)),
            pl.BlockSpec((N_META, TN), lambda m, n: (0, once(m, n, nn - 1))),
            pl.BlockSpec((N_META, TN), lambda m, n: (0, once(m, kcol(n), SB_W // TN - 1))),
            pl.BlockSpec((N_META, TN), lambda m, n: (0, once(m, vcol(n), SB_W // TN - 1))),
            pl.BlockSpec((TM, LANE), lambda m, n: (m, 0)),
            pl.BlockSpec((N_META, LANE), lambda m, n: (0, 0)),
            any_spec, any_spec, any_spec, any_spec,
        ] + cast_specs,
        out_shape=[
            jax.ShapeDtypeStruct((rows, P_W), BF16),
            jax.ShapeDtypeStruct((N_META, P_W), BF16),
            jax.ShapeDtypeStruct((N_META, SB_W), F32),
            jax.ShapeDtypeStruct((N_META, SB_W), F32),
            jax.ShapeDtypeStruct((rows, LANE), F32),
            jax.ShapeDtypeStruct((N_META, LANE), F32),
            jax.ShapeDtypeStruct((batch * (seq + N_META), H_SB, SB_HD), F32),
            jax.ShapeDtypeStruct((batch * (seq + N_META), H_SB, SB_HD), F32),
            jax.ShapeDtypeStruct((nsr, H_SB, SB_HD), F32),
            jax.ShapeDtypeStruct((nsr, H_SB, SB_HD), F32),
        ] + [jax.ShapeDtypeStruct(w.shape, BF16) for w in (w_out, w_up, w_down)],
        scratch_shapes=[
            pltpu.VMEM((TM + N_META, D_MODEL), BF16),
            pltpu.VMEM((H_SB, TM, SB_HD), F32),
            pltpu.VMEM((H_SB, N_META, SB_HD), F32),
            pltpu.SemaphoreType.DMA(()),
            pltpu.SemaphoreType.DMA(()),
        ],
        compiler_params=pltpu.CompilerParams(
            dimension_semantics=("arbitrary", "arbitrary"), vmem_limit_bytes=VMEM_LIMIT),
        name="inproj",
    )(xp, xs, meta, g1, w_main, w_alow, w_out, w_up, w_down)


def _gla_chunk(c, q, k, v, al, wup, bias, st_ref):
    x = _dot(al.astype(BF16), wup) + bias
    g = _log_sigmoid(x) * (1.0 / GLA_TAU)
    g_hi, g_lo = _split2(g)
    row = lax.broadcasted_iota(jnp.int32, (c, c), 0)
    col = lax.broadcasted_iota(jnp.int32, (c, c), 1)
    causal = col <= row
    tri = causal.astype(BF16)
    cum = _dot(tri, g_hi) + _dot(tri, g_lo)
    outs = []
    for h in range(H_GLA):
        ch = cum[:, h * GLA_DK:(h + 1) * GLA_DK]
        last = ch[c - 1:c, :]
        qh = q[:, h * GLA_DK:(h + 1) * GLA_DK].astype(F32)
        kh = k[:, h * GLA_DK:(h + 1) * GLA_DK].astype(F32)
        vh = v[:, h * GLA_DV:(h + 1) * GLA_DV]
        qd = (qh * jnp.exp(ch) * (GLA_DK ** -0.5)).astype(BF16)
        kd = (kh * jnp.exp(-ch)).astype(BF16)
        kr = (kh * jnp.exp(last - ch)).astype(BF16)
        att = jnp.where(causal, _dot_nt(qd, kd), 0.0).astype(BF16)
        st = st_ref[h]
        outs.append(_dot(att, vh) + _dot_nt(qd, st.astype(BF16)))
        st_ref[h] = st * jnp.exp(last) + _dot_tn(vh, kr)
    return outs


def _gla_finish(o, gn, sa):
    ms = jnp.mean(o * o, axis=-1, keepdims=True)
    return (o * lax.rsqrt(ms + EPS) * gn * sa.astype(F32)).astype(BF16)


def _gla_prompt_kernel(q_ref, k_ref, v_ref, sa_ref, al_ref, km_ref, vm_ref, alm_ref,
                       wup_ref, b_ref, gn_ref, og_ref, so_ref, st_ref):
    c_idx = pl.program_id(1)
    wup = wup_ref[...]
    bias = b_ref[...]

    @pl.when(c_idx == 0)
    def _():
        st_ref[...] = jnp.zeros_like(st_ref)
        zq = jnp.zeros((N_META, GLA_QK), BF16)
        _gla_chunk(N_META, zq, km_ref[...], vm_ref[...], alm_ref[...], wup, bias, st_ref)

    def body(i, carry):
        r0 = pl.multiple_of(i * GLA_CHUNK, GLA_CHUNK)
        rows = pl.ds(r0, GLA_CHUNK)
        outs = _gla_chunk(GLA_CHUNK, q_ref[rows, :], k_ref[rows, :], v_ref[rows, :], al_ref[rows, :],
                          wup, bias, st_ref)
        for h in range(H_GLA):
            cols = slice(h * GLA_DV, (h + 1) * GLA_DV)
            og_ref[rows, cols] = _gla_finish(outs[h], gn_ref[:, cols], sa_ref[rows, cols])
        return carry

    lax.fori_loop(0, GLA_ROWS // GLA_CHUNK, body, 0, unroll=True)

    @pl.when(c_idx == pl.num_programs(1) - 1)
    def _():
        for h in range(H_GLA):
            so_ref[0, h] = st_ref[h].T


def _gla_prompt(p_main, p_meta, alow, alow_meta, wup, bias, gn, batch, seq):
    nc = seq // GLA_ROWS
    rb = lambda b, c: b * nc + c
    const = lambda b, c: (0, 0)
    return pl.pallas_call(
        _gla_prompt_kernel,
        grid=(batch, nc),
        in_specs=[
            pl.BlockSpec((GLA_ROWS, GLA_QK), lambda b, c: (rb(b, c), P_GQ // GLA_QK)),
            pl.BlockSpec((GLA_ROWS, GLA_QK), lambda b, c: (rb(b, c), P_GK // GLA_QK)),
            pl.BlockSpec((GLA_ROWS, GLA_V), lambda b, c: (rb(b, c), P_GV // GLA_V)),
            pl.BlockSpec((GLA_ROWS, GLA_V), lambda b, c: (rb(b, c), P_GA // GLA_V)),
            pl.BlockSpec((GLA_ROWS, LANE), lambda b, c: (rb(b, c), 0)),
            pl.BlockSpec((N_META, GLA_QK), lambda b, c: (0, P_GK // GLA_QK)),
            pl.BlockSpec((N_META, GLA_V), lambda b, c: (0, P_GV // GLA_V)),
            pl.BlockSpec((N_META, LANE), const),
            pl.BlockSpec((LANE, GLA_QK), const),
            pl.BlockSpec((1, GLA_QK), const),
            pl.BlockSpec((1, GLA_V), const),
        ],
        out_specs=[
            pl.BlockSpec((GLA_ROWS, GLA_V), lambda b, c: (rb(b, c), 0)),
            pl.BlockSpec((1, H_GLA, GLA_DK, GLA_DV), lambda b, c: (b, 0, 0, 0)),
        ],
        out_shape=[
            jax.ShapeDtypeStruct((batch * seq, GLA_V), BF16),
            jax.ShapeDtypeStruct((batch, H_GLA, GLA_DK, GLA_DV), F32),
        ],
        scratch_shapes=[pltpu.VMEM((H_GLA, GLA_DV, GLA_DK), F32)],
        compiler_params=pltpu.CompilerParams(
            dimension_semantics=("arbitrary", "arbitrary"), vmem_limit_bytes=VMEM_LIMIT),
        name="gla_prompt",
    )(p_main, p_main, p_main, p_main, alow, p_meta, p_meta, alow_meta, wup, bias, gn)


def _gla_sample_kernel(t_new, q_ref, k_ref, v_ref, sa_ref, al_ref, s0_ref,
                       wup_ref, b_ref, gn_ref, og_ref, so_ref, st_ref):
    for h in range(H_GLA):
        st_ref[h] = s0_ref[0, h].T
    outs = _gla_chunk(t_new, q_ref[...], k_ref[...], v_ref[...], al_ref[...],
                      wup_ref[...], b_ref[...], st_ref)
    for h in range(H_GLA):
        cols = slice(h * GLA_DV, (h + 1) * GLA_DV)
        og_ref[:, cols] = _gla_finish(outs[h], gn_ref[:, cols], sa_ref[:, cols])
        so_ref[0, h] = st_ref[h].T


def _gla_sample(p_main, alow, state, wup, bias, gn, row0, dec_batch, t_new):
    rb0 = row0 // t_new
    const = lambda b: (0, 0)
    return pl.pallas_call(
        functools.partial(_gla_sample_kernel, t_new),
        grid=(dec_batch,),
        in_specs=[
            pl.BlockSpec((t_new, GLA_QK), lambda b: (rb0 + b, P_GQ // GLA_QK)),
            pl.BlockSpec((t_new, GLA_QK), lambda b: (rb0 + b, P_GK // GLA_QK)),
            pl.BlockSpec((t_new, GLA_V), lambda b: (rb0 + b, P_GV // GLA_V)),
            pl.BlockSpec((t_new, GLA_V), lambda b: (rb0 + b, P_GA // GLA_V)),
            pl.BlockSpec((t_new, LANE), lambda b: (rb0 + b, 0)),
            pl.BlockSpec((1, H_GLA, GLA_DK, GLA_DV), lambda b: (b, 0, 0, 0)),
            pl.BlockSpec((LANE, GLA_QK), const),
            pl.BlockSpec((1, GLA_QK), const),
            pl.BlockSpec((1, GLA_V), const),
        ],
        out_specs=[
            pl.BlockSpec((t_new, GLA_V), lambda b: (b, 0)),
            pl.BlockSpec((1, H_GLA, GLA_DK, GLA_DV), lambda b: (b, 0, 0, 0)),
        ],
        out_shape=[
            jax.ShapeDtypeStruct((dec_batch * t_new, GLA_V), BF16),
            jax.ShapeDtypeStruct((dec_batch, H_GLA, GLA_DK, GLA_DV), F32),
        ],
        scratch_shapes=[pltpu.VMEM((H_GLA, GLA_DV, GLA_DK), F32)],
        compiler_params=pltpu.CompilerParams(
            dimension_semantics=("arbitrary",), vmem_limit_bytes=VMEM_LIMIT),
        name="gla_sample",
    )(p_main, p_main, p_main, p_main, alow, state, wup, bias, gn)


def _load_perm(ref, base, nk):
    parts = []
    for j in range(nk // SB_MINI):
        for c in range(SB_STRIDE):
            parts.append(ref[pl.ds(base + SB_MINI * j + c, SUBLANE, stride=SB_STRIDE), :])
    return jnp.concatenate(parts, axis=0)


def _sb_scan(zt, carry, limit):
    nk, nq = zt.shape
    rowid = lax.broadcasted_iota(jnp.int32, (SUBLANE, nq), 0)
    w_parts = [None] * (nk // SUBLANE)
    for j in reversed(range(nk // SB_MINI)):
        ls, ss, vis = [], [], []
        for c in range(SB_STRIDE):
            i = j * SB_STRIDE + c
            z = zt[i * SUBLANE:(i + 1) * SUBLANE]
            nz = -z
            t = jnp.log2(1.0 + jnp.exp2(jnp.minimum(z, nz)))
            l = jnp.minimum(nz, 0.0) - t
            ss.append(l + z)
            if limit is not None:
                v = rowid * SB_STRIDE + (SB_MINI * j + c) < limit
                l = jnp.where(v, l, 0.0)
                vis.append(v)
            ls.append(l)
        later = [None] * SB_STRIDE
        run = ls[SB_STRIDE - 1]
        for c in range(SB_STRIDE - 2, -1, -1):
            later[c] = run
            run = run + ls[c]
        incl = run
        for sh in (1, 2, 4):
            incl = incl + jnp.where(rowid < SUBLANE - sh, pltpu.roll(incl, SUBLANE - sh, axis=0), 0.0)
        off = carry + (incl - run)
        for c in range(SB_STRIDE):
            after = off if later[c] is None else off + later[c]
            w = jnp.exp2(ss[c] + after)
            if limit is not None:
                w = jnp.where(vis[c], w, 0.0)
            w_parts[j * SB_STRIDE + c] = w
        carry = carry + incl[0:1, :]
    return jnp.concatenate(w_parts, axis=0), carry


SB_PAR = 4


def _sb_prompt_kernel(seq, q_ref, k_ref, v_ref, km_ref, vm_ref, sb_ref, o_ref, kms_ref, vms_ref):
    kms_ref[...] = jnp.zeros_like(kms_ref)
    vms_ref[...] = jnp.zeros_like(vms_ref)
    for i in range(SB_PAR):
        kms_ref[i, 0:N_META, :] = km_ref[:, i * SB_HD:(i + 1) * SB_HD]
        vms_ref[i, 0:N_META, :] = vm_ref[:, i * SB_HD:(i + 1) * SB_HD]
    heads = range(SB_PAR)
    diag_limit = lax.broadcasted_iota(jnp.int32, (SUBLANE, SB_TILE), 1)

    def tile(kp, vp, q, carry, acc, limit):
        zt = _dot_nt(kp, q)
        wt, carry = _sb_scan(zt, carry, limit)
        return carry, acc + _dot_tn(vp, wt.astype(BF16))

    def prompt_tiles(base, qs, carries, accs, limit):
        rows = pl.ds(base, SB_TILE)
        res = [tile(k_ref[rows, i * SB_HD:(i + 1) * SB_HD], v_ref[rows, i * SB_HD:(i + 1) * SB_HD],
                    qs[i], carries[i], accs[i], limit) for i in heads]
        return [r[0] for r in res], [r[1] for r in res]

    def plain_tiles(t, qs, carries, accs):
        return prompt_tiles(pl.multiple_of(t * SB_TILE, SB_TILE), qs, carries, accs, None)

    def meta_tiles(qs, carries, accs):
        res = [tile(_load_perm(kms_ref.at[i], 0, SB_MINI).astype(BF16),
                    _load_perm(vms_ref.at[i], 0, SB_MINI).astype(BF16),
                    qs[i], carries[i], accs[i], N_META) for i in heads]
        return [r[0] for r in res], [r[1] for r in res]

    def qblock(m, is_first):
        row0 = pl.multiple_of(m * SB_TILE, SB_TILE)
        rows = pl.ds(row0, SB_TILE)
        qs = [q_ref[rows, i * SB_HD:(i + 1) * SB_HD] for i in heads]
        carry0 = [jnp.zeros((1, SB_TILE), F32)] * SB_PAR
        acc0 = [jnp.zeros((SB_HD, SB_TILE), F32)] * SB_PAR
        carries, accs = prompt_tiles(row0, qs, carry0, acc0, diag_limit)
        if is_first:
            _, accs = meta_tiles(qs, carries, accs)
        else:
            carries, accs = plain_tiles(m - 1, qs, carries, accs)

            def cond(st):
                t, carries, _ = st
                worst = functools.reduce(jnp.maximum, [jnp.max(c) for c in carries])
                return (t >= -1) & (worst > SB_CUTOFF_LOG2)

            def body(st):
                t, carries, accs = st
                carries, accs = lax.cond(t >= 0,
                                         lambda: plain_tiles(t, qs, carries, accs),
                                         lambda: meta_tiles(qs, carries, accs))
                return t - 1, carries, accs

            _, _, accs = lax.while_loop(cond, body, (m - 2, carries, accs))
        for i in heads:
            cols = slice(i * SB_HD, (i + 1) * SB_HD)
            o = accs[i].T * sb_ref[rows, cols].astype(F32)
            o_ref[rows, cols] = o.astype(BF16)

    qblock(0, True)

    def loop_body(m, carry):
        qblock(m, False)
        return carry

    lax.fori_loop(1, seq // SB_TILE, loop_body, 0)


def _sb_prompt(p_main, km32, vm32, batch, seq):
    gw = SB_PAR * SB_HD
    col = lambda off: pl.BlockSpec((seq, gw), lambda b, g: (b, off // gw + g))
    return pl.pallas_call(
        functools.partial(_sb_prompt_kernel, seq),
        grid=(batch, H_SB // SB_PAR),
        in_specs=[col(P_SQ), col(P_SK), col(P_SV),
                  pl.BlockSpec((N_META, gw), lambda b, g: (0, g)),
                  pl.BlockSpec((N_META, gw), lambda b, g: (0, g)),
                  col(P_GB)],
        out_specs=pl.BlockSpec((seq, gw), lambda b, g: (b, g)),
        out_shape=jax.ShapeDtypeStruct((batch * seq, SB_W), BF16),
        scratch_shapes=[pltpu.VMEM((SB_PAR, SB_MINI, SB_HD), F32), pltpu.VMEM((SB_PAR, SB_MINI, SB_HD), F32)],
        compiler_params=pltpu.CompilerParams(
            dimension_semantics=("arbitrary", "arbitrary"), vmem_limit_bytes=VMEM_LIMIT),
        name="sb_prompt",
    )(p_main, p_main, p_main, km32, vm32, p_main)


SB_GROUP = 4


def _load_perm_native(ref, key0, nk, head):
    parts = []
    for j in range(nk // SB_MINI):
        for c in range(SB_STRIDE):
            start = (key0 + SB_MINI * j + c) * H_SB + head
            parts.append(ref[pl.ds(start, SUBLANE, stride=SB_STRIDE * H_SB), :])
    return jnp.concatenate(parts, axis=0)


def _sb_sample_kernel(t_new, past, q_ref, kn_ref, vn_ref, ck_ref, cv_ref, sb_ref, o_ref,
                      kbuf, vbuf, kfar, vfar, sems, far_sems):
    b = pl.program_id(0)
    nb = pl.num_programs(0)
    gw = SB_GROUP * SB_HD
    nq = SB_GROUP * t_new
    ngroups = H_SB // SB_GROUP
    n_tiles = past // SB_TILE
    tile_rows = SB_TILE * H_SB

    def near_copies(bb, slot):
        rows = pl.ds((n_tiles - 1) * tile_rows, tile_rows)
        return (pltpu.make_async_copy(ck_ref.at[bb, rows, :], kbuf.at[slot], sems.at[slot, 0]),
                pltpu.make_async_copy(cv_ref.at[bb, rows, :], vbuf.at[slot], sems.at[slot, 1]))

    @pl.when(b == 0)
    def _():
        for cp in near_copies(0, 0):
            cp.start()

    slot = b % 2

    @pl.when(b + 1 < nb)
    def _():
        for cp in near_copies(b + 1, 1 - slot):
            cp.start()

    for cp in near_copies(b, slot):
        cp.wait()

    rh = lax.broadcasted_iota(jnp.int32, (gw, nq), 0) // SB_HD
    ch = lax.broadcasted_iota(jnp.int32, (gw, nq), 1) // t_new
    head_match = rh == ch
    new_limit = lax.broadcasted_iota(jnp.int32, (SUBLANE, nq), 1) % t_new

    def tile(kref, vref, nk, g, qbd, carry, acc, limit):
        hs = range(SB_GROUP * g, SB_GROUP * (g + 1))
        kp = jnp.concatenate([_load_perm_native(kref, 0, nk, h) for h in hs], axis=1).astype(BF16)
        vp = jnp.concatenate([_load_perm_native(vref, 0, nk, h) for h in hs], axis=1).astype(BF16)
        zt = _dot(kp, qbd)
        wt, carry = _sb_scan(zt, carry, limit)
        return carry, acc + _dot_tn(vp, wt.astype(BF16))

    qbds, carries, accs = [], [], []
    for g in range(ngroups):
        q = q_ref[:, g * gw:(g + 1) * gw].astype(F32)
        qt = jnp.concatenate([q] * SB_GROUP, axis=0).T
        qbd = jnp.where(head_match, qt, 0.0).astype(BF16)
        carry = jnp.zeros((1, nq), F32)
        acc = jnp.zeros((gw, nq), F32)
        carry, acc = tile(kn_ref, vn_ref, t_new, g, qbd, carry, acc, new_limit)
        carry, acc = tile(kbuf.at[slot], vbuf.at[slot], SB_TILE, g, qbd, carry, acc, None)
        qbds.append(qbd)
        carries.append(carry)
        accs.append(acc)

    def cond(st):
        t, carries, _ = st
        worst = functools.reduce(jnp.maximum, [jnp.max(c) for c in carries])
        return (t >= 0) & (worst > SB_CUTOFF_LOG2)

    def body(st):
        t, carries, accs = st
        rows = pl.ds(pl.multiple_of(t * tile_rows, tile_rows), tile_rows)
        far = (pltpu.make_async_copy(ck_ref.at[b, rows, :], kfar, far_sems.at[0]),
               pltpu.make_async_copy(cv_ref.at[b, rows, :], vfar, far_sems.at[1]))
        for cp in far:
            cp.start()
        for cp in far:
            cp.wait()
        res = [tile(kfar, vfar, SB_TILE, g, qbds[g], carries[g], accs[g], None) for g in range(ngroups)]
        return t - 1, [r[0] for r in res], [r[1] for r in res]

    _, _, accs = lax.while_loop(cond, body, (n_tiles - 2, carries, accs))
    for g in range(ngroups):
        at = accs[g].T
        for h in range(SB_GROUP):
            cols = slice(g * gw + h * SB_HD, g * gw + (h + 1) * SB_HD)
            o = at[h * t_new:(h + 1) * t_new, h * SB_HD:(h + 1) * SB_HD] * sb_ref[:, cols].astype(F32)
            o_ref[:, cols] = o.astype(BF16)


def _sb_sample(p_main, k5s, v5s, cache_k, cache_v, row0, dec_batch, t_new, past):
    rb0 = row0 // t_new
    tile_rows = SB_TILE * H_SB
    any_spec = pl.BlockSpec(memory_space=pl.ANY)
    new_spec = pl.BlockSpec((t_new * H_SB, SB_HD), lambda b: (b, 0))
    return pl.pallas_call(
        functools.partial(_sb_sample_kernel, t_new, past),
        grid=(dec_batch,),
        in_specs=[pl.BlockSpec((t_new, SB_W), lambda b: (rb0 + b, P_SQ // SB_W)),
                  new_spec, new_spec, any_spec, any_spec,
                  pl.BlockSpec((t_new, SB_W), lambda b: (rb0 + b, P_GB // SB_W))],
        out_specs=pl.BlockSpec((t_new, SB_W), lambda b: (b, 0)),
        out_shape=jax.ShapeDtypeStruct((dec_batch * t_new, SB_W), BF16),
        scratch_shapes=[
            pltpu.VMEM((2, tile_rows, SB_HD), F32),
            pltpu.VMEM((2, tile_rows, SB_HD), F32),
            pltpu.VMEM((tile_rows, SB_HD), F32),
            pltpu.VMEM((tile_rows, SB_HD), F32),
            pltpu.SemaphoreType.DMA((2, 2)),
            pltpu.SemaphoreType.DMA((2,)),
        ],
        compiler_params=pltpu.CompilerParams(
            dimension_semantics=("arbitrary",), vmem_limit_bytes=VMEM_LIMIT),
        name="sb_sample",
    )(p_main, k5s, v5s, cache_k, cache_v, p_main)


def _merge_kernel(n_prompt_tiles, ogp_ref, ogs_ref, obp_ref, obs_ref, xp_ref, xs_ref, w_ref, o_ref):
    m = pl.program_id(0)

    def run(og_ref, ob_ref, x_ref):
        mix = (og_ref[...].astype(F32) + ob_ref[...].astype(F32)).astype(BF16)
        o_ref[...] = x_ref[...] + _dot(mix, w_ref[...])

    @pl.when(m < n_prompt_tiles)
    def _():
        run(ogp_ref, obp_ref, xp_ref)

    @pl.when(m >= n_prompt_tiles)
    def _():
        run(ogs_ref, obs_ref, xs_ref)


def _merge(og_p, og_s, ob_p, ob_s, xp, xs, w_out):
    npt, nst = xp.shape[0] // TM, xs.shape[0] // TM
    pidx = lambda m: (jnp.minimum(m, npt - 1), 0)
    sidx = lambda m: (jnp.maximum(m - npt, 0), 0)
    row = lambda idx: pl.BlockSpec((TM, D_MODEL), idx)
    return pl.pallas_call(
        functools.partial(_merge_kernel, npt),
        grid=(npt + nst,),
        in_specs=[row(pidx), row(sidx), row(pidx), row(sidx), row(pidx), row(sidx),
                  pl.BlockSpec((D_MODEL, D_MODEL), lambda m: (0, 0))],
        out_specs=pl.BlockSpec((TM, D_MODEL), lambda m: (m, 0)),
        out_shape=jax.ShapeDtypeStruct((xp.shape[0] + xs.shape[0], D_MODEL), F32),
        compiler_params=pltpu.CompilerParams(
            dimension_semantics=("arbitrary",), vmem_limit_bytes=VMEM_LIMIT),
        name="merge_outproj",
    )(og_p, og_s, ob_p, ob_s, xp, xs, w_out)


def _ffn_kernel(n_prompt_tiles, x_ref, g2_ref, wu_ref, wd_ref, gf_ref, yp_ref, ys_ref, h_ref, acc_ref):
    m = pl.program_id(0)
    f = pl.program_id(1)

    @pl.when(f == 0)
    def _():
        h_ref[...] = _rms(x_ref[...], g2_ref[...]).astype(BF16)
        acc_ref[...] = jnp.zeros_like(acc_ref)

    u = jnp.maximum(_dot(h_ref[...], wu_ref[...]), 0.0)
    acc_ref[...] += _dot((u * u).astype(BF16), wd_ref[...])

    @pl.when(f == pl.num_programs(1) - 1)
    def _():
        y = _rms(x_ref[...] + acc_ref[...], gf_ref[...])

        @pl.when(m < n_prompt_tiles)
        def _():
            yp_ref[...] = y

        @pl.when(m >= n_prompt_tiles)
        def _():
            ys_ref[...] = y


def _ffn(x1, g2, w_up, w_down, gf, npr, nsr):
    npt, nst = npr // TM, nsr // TM
    return pl.pallas_call(
        functools.partial(_ffn_kernel, npt),
        grid=(npt + nst, D_FF // TF),
        in_specs=[
            pl.BlockSpec((TM, D_MODEL), lambda m, f: (m, 0)),
            pl.BlockSpec((1, D_MODEL), lambda m, f: (0, 0)),
            pl.BlockSpec((D_MODEL, TF), lambda m, f: (0, f)),
            pl.BlockSpec((TF, D_MODEL), lambda m, f: (f, 0)),
            pl.BlockSpec((1, D_MODEL), lambda m, f: (0, 0)),
        ],
        out_specs=[
            pl.BlockSpec((TM, D_MODEL), lambda m, f: (jnp.minimum(m, npt - 1), 0)),
            pl.BlockSpec((TM, D_MODEL), lambda m, f: (jnp.maximum(m - npt, 0), 0)),
        ],
        out_shape=[
            jax.ShapeDtypeStruct((npr, D_MODEL), F32),
            jax.ShapeDtypeStruct((nsr, D_MODEL), F32),
        ],
        scratch_shapes=[pltpu.VMEM((TM, D_MODEL), BF16), pltpu.VMEM((TM, D_MODEL), F32)],
        compiler_params=pltpu.CompilerParams(
            dimension_semantics=("arbitrary", "arbitrary"), vmem_limit_bytes=VMEM_LIMIT),
        name="ffn_final",
    )(x1, g2, w_up, w_down, gf)


def kernel(x_prompt, x_sample, cache_sb_k, cache_sb_v, state_gla, meta_tokens, norm1_g, w_in,
           w_alpha_up, b_alpha, gla_norm_g, w_out, norm2_g, w_up, w_down, norm_f_g):
    batch, seq, _ = x_prompt.shape
    dec_batch, t_new, _ = x_sample.shape
    depth, _, past = cache_sb_k.shape[:3]
    assert depth == 1 and w_in.shape[2] == _R_END
    assert seq % TM == 0 and (dec_batch * t_new) % TM == 0 and past % SB_TILE == 0
    assert t_new == SB_MINI and t_new % 16 == 0
    npr, nsr = batch * seq, dec_batch * t_new

    w_main, w_alow = _prep_w_in(w_in[0].T)
    wup = jnp.pad(w_alpha_up[0], ((0, LANE - GLA_RANK), (0, 0))).astype(BF16)
    bias = b_alpha[0].reshape(1, GLA_QK)
    gn = gla_norm_g[0].reshape(1, GLA_V)
    g1 = norm1_g[0].reshape(1, D_MODEL)
    g2 = norm2_g[0].reshape(1, D_MODEL)
    gf = norm_f_g.reshape(1, D_MODEL)

    xp = x_prompt.reshape(npr, D_MODEL)
    xs = x_sample.reshape(nsr, D_MODEL)
    meta = meta_tokens.astype(x_prompt.dtype)

    (p_main, p_meta, km32, vm32, alow, alow_meta, k5p, v5p, k5s, v5s,
     w_out_b, w_up_b, w_down_b) = _inproj(xp, xs, meta, g1, w_main, w_alow,
                                          w_out[0], w_up[0], w_down[0], batch, seq)

    og_p, st_p = _gla_prompt(p_main, p_meta, alow, alow_meta, wup, bias, gn, batch, seq)
    og_s, st_s = _gla_sample(p_main, alow, state_gla[0], wup, bias, gn, npr, dec_batch, t_new)
    ob_p = _sb_prompt(p_main, km32, vm32, batch, seq)
    ck = cache_sb_k[0].reshape(dec_batch, past * H_SB, SB_HD)
    cv = cache_sb_v[0].reshape(dec_batch, past * H_SB, SB_HD)
    ob_s = _sb_sample(p_main, k5s.reshape(nsr * H_SB, SB_HD), v5s.reshape(nsr * H_SB, SB_HD),
                      ck, cv, npr, dec_batch, t_new, past)

    x1 = _merge(og_p, og_s, ob_p, ob_s, xp, xs, w_out_b)
    y_p, y_s = _ffn(x1, g2, w_up_b, w_down_b, gf, npr, nsr)

    return (
        y_p.reshape(batch, seq, D_MODEL),
        y_s.reshape(dec_batch, t_new, D_MODEL),
        st_p[None],
        k5p.reshape(1, batch, N_META + seq, H_SB, SB_HD),
        v5p.reshape(1, batch, N_META + seq, H_SB, SB_HD),
        st_s[None],
        k5s.reshape(1, dec_batch, t_new, H_SB, SB_HD),
        v5s.reshape(1, dec_batch, t_new, H_SB, SB_HD),
    )
```

```python
import functools

import jax
import jax.numpy as jnp
from jax import lax
from jax.experimental import pallas as pl
from jax.experimental.pallas import tpu as pltpu

F32 = jnp.float32
BF16 = jnp.bfloat16

D_MODEL = 2048
N_META = 16
H_GLA = 4
GLA_DK = 256
GLA_DV = 512
GLA_RANK = 16
GLA_TAU = 16.0
GLA_CHUNK = 64
H_SB = 16
SB_HD = 128
D_FF = 4 * D_MODEL
EPS = 1e-5
GLA_QK = H_GLA * GLA_DK
GLA_V = H_GLA * GLA_DV
SB_W = H_SB * SB_HD

_R_GK = GLA_QK
_R_GV = 2 * GLA_QK
_R_ALOW = _R_GV + GLA_V
_R_SQ = _R_ALOW + GLA_RANK
_R_SK = _R_SQ + SB_W
_R_SV = _R_SK + SB_W
_R_GA = _R_SV + SB_W
_R_GB = _R_GA + GLA_V
_R_END = _R_GB + SB_W

P_GQ = 0
P_GK = P_GQ + GLA_QK
P_GV = P_GK + GLA_QK
P_SQ = P_GV + GLA_V
P_SK = P_SQ + SB_W
P_SV = P_SK + SB_W
P_GA = P_SV + SB_W
P_GB = P_GA + GLA_V
P_W = P_GB + SB_W

LANE = 128
SUBLANE = 8
TM = 512
TN = 2048
TF = 1024
GLA_ROWS = 256
SB_TILE = 256
SB_MINI = 32
SB_STRIDE = SB_MINI // SUBLANE
SB_CUTOFF = -104.0
LOG2_E = 1.4426950408889634
SB_SCALE_LOG2 = SB_HD ** -0.5 * LOG2_E
SB_CUTOFF_LOG2 = SB_CUTOFF * LOG2_E
VMEM_LIMIT = 56 * 1024 * 1024


def _dot(a, b):
    return jnp.dot(a, b, preferred_element_type=F32)


def _dot_nt(a, b):
    return lax.dot_general(a, b, (((1,), (1,)), ((), ())), preferred_element_type=F32)


def _dot_tn(a, b):
    return lax.dot_general(a, b, (((0,), (0,)), ((), ())), preferred_element_type=F32)


def _split2(x):
    hi = x.astype(BF16)
    lo = (x - hi.astype(F32)).astype(BF16)
    return hi, lo


def _rms(x, g):
    ms = jnp.mean(x * x, axis=-1, keepdims=True)
    return x * lax.rsqrt(ms + EPS) * g


def _log_sigmoid(x):
    return jnp.minimum(x, 0.0) - jnp.log(1.0 + jnp.exp(-jnp.abs(x)))


_PREP_ROWS = 1024
_PREP_COLS = 1024


def _prep_w_kernel(n_aligned, wt_ref, edge_ref, o_ref, al_ref):
    n = pl.program_id(1)

    @pl.when(n < n_aligned)
    def _():
        o_ref[...] = wt_ref[...].T.astype(BF16)

    def shifted():
        return jnp.concatenate([wt_ref[GLA_RANK:, :], edge_ref[...]], axis=0).T

    is_sq = (n >= P_SQ // _PREP_COLS) & (n < P_SK // _PREP_COLS)

    @pl.when((n >= n_aligned) & jnp.logical_not(is_sq))
    def _():
        o_ref[...] = shifted().astype(BF16)

    @pl.when(is_sq)
    def _():
        o_ref[...] = (shifted() * SB_SCALE_LOG2).astype(BF16)

    @pl.when(n == n_aligned)
    def _():
        pad = jnp.zeros((LANE - GLA_RANK, wt_ref.shape[1]), F32)
        al_ref[...] = jnp.concatenate([wt_ref[0:GLA_RANK, :], pad], axis=0).T.astype(BF16)


def _prep_w_in(wt):
    n_aligned = _R_ALOW // _PREP_COLS
    edge_blocks = _PREP_COLS // GLA_RANK
    return pl.pallas_call(
        functools.partial(_prep_w_kernel, n_aligned),
        grid=(D_MODEL // _PREP_ROWS, P_W // _PREP_COLS),
        in_specs=[
            pl.BlockSpec((_PREP_COLS, _PREP_ROWS), lambda r, n: (n, r)),
            pl.BlockSpec((GLA_RANK, _PREP_ROWS), lambda r, n: ((jnp.maximum(n, n_aligned) + 1) * edge_blocks, r)),
        ],
        out_specs=[pl.BlockSpec((_PREP_ROWS, _PREP_COLS), lambda r, n: (r, n)),
                   pl.BlockSpec((_PREP_ROWS, LANE), lambda r, n: (r, 0))],
        out_shape=[jax.ShapeDtypeStruct((D_MODEL, P_W), BF16),
                   jax.ShapeDtypeStruct((D_MODEL, LANE), BF16)],
        compiler_params=pltpu.CompilerParams(
            dimension_semantics=("arbitrary", "arbitrary"), vmem_limit_bytes=VMEM_LIMIT),
        name="prep_w_in",
    )(wt, wt)


_N_SK0 = P_SK // TN
_N_SV0 = P_SV // TN
_N_GATE0 = P_GA // TN


assert TN == SB_W


def _permute_rows(stage_ref):
    heads, rows, _ = stage_ref.shape
    cols = []
    for h in range(heads):
        parts = [stage_ref[h, pl.ds(SB_MINI * j + c, SUBLANE, stride=SB_STRIDE), :]
                 for j in range(rows // SB_MINI) for c in range(SB_STRIDE)]
        cols.append(jnp.concatenate(parts, axis=0).astype(BF16))
    return jnp.concatenate(cols, axis=1)


MAX_CAST_SLABS = 64


def _inproj_kernel(n_prompt_tiles, batch, seq, xp_ref, xs_ref, meta_ref, g1_ref, w_ref, wal_ref,
                   wo_ref, wu_ref, wd_ref,
                   p_ref, pm_ref, km_ref, vm_ref, al_ref, alm_ref,
                   kp_ref, vp_ref, ks_ref, vs_ref, wob_ref, wub_ref, wdb_ref,
                   h_ref, stage_ref, mstage_ref, sem, msem):
    m = pl.program_id(0)
    n = pl.program_id(1)
    n_tiles = pl.num_programs(0)
    tiles_per_batch = seq // TM

    def cast_slab():
        wob_ref[...] = wo_ref[...].astype(BF16)
        wub_ref[...] = wu_ref[...].astype(BF16)
        wdb_ref[...] = wd_ref[...].astype(BF16)

    def stage_heads(ref, vals):
        for h in range(H_SB):
            ref[h] = vals[:, h * SB_HD:(h + 1) * SB_HD]

    def native_copies(src_ref, rows, dst_ref, token, dma_sem):
        return [pltpu.make_async_copy(src_ref.at[h], dst_ref.at[pl.ds(token, rows), h, :], dma_sem)
                for h in range(H_SB)]

    def wait_native(pending):
        @pl.when(pending)
        def _():
            for cp in native_copies(stage_ref, TM, kp_ref, 0, sem):
                cp.wait()

    def write_native(dst_p, dst_s):
        @pl.when(m < n_prompt_tiles)
        def _():
            b = m // tiles_per_batch
            token = b * (seq + N_META) + N_META + (m - b * tiles_per_batch) * TM
            for cp in native_copies(stage_ref, TM, dst_p, token, sem):
                cp.start()

        @pl.when(m >= n_prompt_tiles)
        def _():
            for cp in native_copies(stage_ref, TM, dst_s, (m - n_prompt_tiles) * TM, sem):
                cp.start()

    def write_native_meta(met, dst_p):
        stage_heads(mstage_ref, met)
        copies = [cp for b in range(batch)
                  for cp in native_copies(mstage_ref, N_META, dst_p, b * (seq + N_META), msem)]
        for cp in copies:
            cp.start()
        for cp in copies:
            cp.wait()

    first = m == 0

    @pl.when(n == 0)
    def _():
        g = g1_ref[...]

        @pl.when(m < n_prompt_tiles)
        def _():
            h_ref[0:TM, :] = _rms(xp_ref[...], g).astype(BF16)

        @pl.when(m >= n_prompt_tiles)
        def _():
            h_ref[0:TM, :] = _rms(xs_ref[...], g).astype(BF16)

        al_ref[...] = _dot(h_ref[0:TM, :], wal_ref[...])

        @pl.when(first)
        def _():
            h_ref[TM:TM + N_META, :] = _rms(meta_ref[...], g).astype(BF16)
            alm_ref[...] = _dot(h_ref[TM:TM + N_META, :], wal_ref[...])

    is_gate = n >= _N_GATE0
    is_k = (n >= _N_SK0) & (n < _N_SV0)
    is_v = (n >= _N_SV0) & (n < _N_GATE0)

    def column_tile(kind, epilogue, meta_epilogue):
        @pl.when(kind & first)
        def _():
            cast_slab()
            acc = _dot(h_ref[...], w_ref[...])
            epilogue(acc[:TM])
            pm_ref[...] = acc[TM:].astype(BF16)
            meta_epilogue(acc[TM:])

        @pl.when(kind & jnp.logical_not(first))
        def _():
            cast_slab()
            epilogue(_dot(h_ref[0:TM, :], w_ref[...]))

    def plain(main):
        p_ref[...] = main.astype(BF16)

    def gate(main):
        p_ref[...] = jax.nn.sigmoid(main).astype(BF16)

    def kv(dst_p, dst_s):
        def epilogue(main):
            stage_heads(stage_ref, main)
            write_native(dst_p, dst_s)
            p_ref[...] = _permute_rows(stage_ref)
        return epilogue

    def kv_meta(f32_ref, dst_p):
        def epilogue(met):
            f32_ref[...] = met
            write_native_meta(met, dst_p)
        return epilogue

    @pl.when(is_k)
    def _():
        wait_native(m > 0)

    @pl.when(is_v)
    def _():
        wait_native(True)

    column_tile(jnp.logical_not(is_gate | is_k | is_v), plain, lambda met: None)
    column_tile(is_gate, gate, lambda met: None)
    column_tile(is_k, kv(kp_ref, ks_ref), kv_meta(km_ref, kp_ref))
    column_tile(is_v, kv(vp_ref, vs_ref), kv_meta(vm_ref, vp_ref))

    @pl.when((m == n_tiles - 1) & (n == pl.num_programs(1) - 1))
    def _():
        wait_native(True)


def _inproj(xp, xs, meta, g1, w_main, w_alow, w_out, w_up, w_down, batch, seq):
    npr, nsr = xp.shape[0], xs.shape[0]
    npt, nst = npr // TM, nsr // TM
    nt = npt + nst
    rows = npr + nsr
    nn = P_W // TN
    n_slabs = min(MAX_CAST_SLABS, 1 << ((nt * nn).bit_length() - 1))
    slab = lambda m, n: (jnp.minimum(m * nn + n, n_slabs - 1), 0)
    cast_specs = [pl.BlockSpec((w.shape[0] // n_slabs, w.shape[1]), slab) for w in (w_out, w_up, w_down)]
    kcol = lambda n: jnp.clip(n - _N_SK0, 0, SB_W // TN - 1)
    vcol = lambda n: jnp.clip(n - _N_SV0, 0, SB_W // TN - 1)
    once = lambda m, col, last: jnp.where(m == 0, col, last)
    any_spec = pl.BlockSpec(memory_space=pl.ANY)
    return pl.pallas_call(
        functools.partial(_inproj_kernel, npt, batch, seq),
        grid=(nt, nn),
        in_specs=[
            pl.BlockSpec((TM, D_MODEL), lambda m, n: (jnp.minimum(m, npt - 1), 0)),
            pl.BlockSpec((TM, D_MODEL), lambda m, n: (jnp.maximum(m - npt, 0), 0),
                         pipeline_mode=pl.Buffered(1)),
            pl.BlockSpec((N_META, D_MODEL), lambda m, n: (0, 0)),
            pl.BlockSpec((1, D_MODEL), lambda m, n: (0, 0)),
            pl.BlockSpec((D_MODEL, TN), lambda m, n: (0, n)),
            pl.BlockSpec((D_MODEL, LANE), lambda m, n: (0, 0)),
        ] + cast_specs,
        out_specs=[
            pl.BlockSpec((TM, TN), lambda m, n: (m, n)),
            pl.BlockSpec((N_META, TN), lambda m, n: (0, once(m, n, nn - 1))),
            pl.BlockSpec((N_META, TN), lambda m, n: (0, once(m, kcol(n), SB_W // TN - 1))),
            pl.BlockSpec((N_META, TN), lambda m, n: (0, once(m, vcol(n), SB_W // TN - 1))),
            pl.BlockSpec((TM, LANE), lambda m, n: (m, 0)),
            pl.BlockSpec((N_META, LANE), lambda m, n: (0, 0)),
            any_spec, any_spec, any_spec, any_spec,
        ] + cast_specs,
        out_shape=[
            jax.ShapeDtypeStruct((rows, P_W), BF16),
            jax.ShapeDtypeStruct((N_META, P_W), BF16),
            jax.ShapeDtypeStruct((N_META, SB_W), F32),
            jax.ShapeDtypeStruct((N_META, SB_W), F32),
            jax.ShapeDtypeStruct((rows, LANE), F32),
            jax.ShapeDtypeStruct((N_META, LANE), F32),
            jax.ShapeDtypeStruct((batch * (seq + N_META), H_SB, SB_HD), F32),
            jax.ShapeDtypeStruct((batch * (seq + N_META), H_SB, SB_HD), F32),
            jax.ShapeDtypeStruct((nsr, H_SB, SB_HD), F32),
            jax.ShapeDtypeStruct((nsr, H_SB, SB_HD), F32),
        ] + [jax.ShapeDtypeStruct(w.shape, BF16) for w in (w_out, w_up, w_down)],
        scratch_shapes=[
            pltpu.VMEM((TM + N_META, D_MODEL), BF16),
            pltpu.VMEM((H_SB, TM, SB_HD), F32),
            pltpu.VMEM((H_SB, N_META, SB_HD), F32),
            pltpu.SemaphoreType.DMA(()),
            pltpu.SemaphoreType.DMA(()),
        ],
        compiler_params=pltpu.CompilerParams(
            dimension_semantics=("arbitrary", "arbitrary"), vmem_limit_bytes=VMEM_LIMIT),
        name="inproj",
    )(xp, xs, meta, g1, w_main, w_alow, w_out, w_up, w_down)


def _gla_chunk(c, q, k, v, al, wup, bias, st_ref):
    x = _dot(al.astype(BF16), wup) + bias
    g = _log_sigmoid(x) * (1.0 / GLA_TAU)
    g_hi, g_lo = _split2(g)
    row = lax.broadcasted_iota(jnp.int32, (c, c), 0)
    col = lax.broadcasted_iota(jnp.int32, (c, c), 1)
    causal = col <= row
    tri = causal.astype(BF16)
    cum = _dot(tri, g_hi) + _dot(tri, g_lo)
    outs = []
    for h in range(H_GLA):
        ch = cum[:, h * GLA_DK:(h + 1) * GLA_DK]
        last = ch[c - 1:c, :]
        qh = q[:, h * GLA_DK:(h + 1) * GLA_DK].astype(F32)
        kh = k[:, h * GLA_DK:(h + 1) * GLA_DK].astype(F32)
        vh = v[:, h * GLA_DV:(h + 1) * GLA_DV]
        qd = (qh * jnp.exp(ch) * (GLA_DK ** -0.5)).astype(BF16)
        kd = (kh * jnp.exp(-ch)).astype(BF16)
        kr = (kh * jnp.exp(last - ch)).astype(BF16)
        att = jnp.where(causal, _dot_nt(qd, kd), 0.0).astype(BF16)
        st = st_ref[h]
        outs.append(_dot(att, vh) + _dot_nt(qd, st.astype(BF16)))
        st_ref[h] = st * jnp.exp(last) + _dot_tn(vh, kr)
    return outs


def _gla_finish(o, gn, sa):
    ms = jnp.mean(o * o, axis=-1, keepdims=True)
    return (o * lax.rsqrt(ms + EPS) * gn * sa.astype(F32)).astype(BF16)


def _gla_prompt_kernel(q_ref, k_ref, v_ref, sa_ref, al_ref, km_ref, vm_ref, alm_ref,
                       wup_ref, b_ref, gn_ref, og_ref, so_ref, st_ref):
    c_idx = pl.program_id(1)
    wup = wup_ref[...]
    bias = b_ref[...]

    @pl.when(c_idx == 0)
    def _():
        st_ref[...] = jnp.zeros_like(st_ref)
        zq = jnp.zeros((N_META, GLA_QK), BF16)
        _gla_chunk(N_META, zq, km_ref[...], vm_ref[...], alm_ref[...], wup, bias, st_ref)

    def body(i, carry):
        r0 = pl.multiple_of(i * GLA_CHUNK, GLA_CHUNK)
        rows = pl.ds(r0, GLA_CHUNK)
        outs = _gla_chunk(GLA_CHUNK, q_ref[rows, :], k_ref[rows, :], v_ref[rows, :], al_ref[rows, :],
                          wup, bias, st_ref)
        for h in range(H_GLA):
            cols = slice(h * GLA_DV, (h + 1) * GLA_DV)
            og_ref[rows, cols] = _gla_finish(outs[h], gn_ref[:, cols], sa_ref[rows, cols])
        return carry

    lax.fori_loop(0, GLA_ROWS // GLA_CHUNK, body, 0, unroll=True)

    @pl.when(c_idx == pl.num_programs(1) - 1)
    def _():
        for h in range(H_GLA):
            so_ref[0, h] = st_ref[h].T


def _gla_prompt(p_main, p_meta, alow, alow_meta, wup, bias, gn, batch, seq):
    nc = seq // GLA_ROWS
    rb = lambda b, c: b * nc + c
    const = lambda b, c: (0, 0)
    return pl.pallas_call(
        _gla_prompt_kernel,
        grid=(batch, nc),
        in_specs=[
            pl.BlockSpec((GLA_ROWS, GLA_QK), lambda b, c: (rb(b, c), P_GQ // GLA_QK)),
            pl.BlockSpec((GLA_ROWS, GLA_QK), lambda b, c: (rb(b, c), P_GK // GLA_QK)),
            pl.BlockSpec((GLA_ROWS, GLA_V), lambda b, c: (rb(b, c), P_GV // GLA_V)),
            pl.BlockSpec((GLA_ROWS, GLA_V), lambda b, c: (rb(b, c), P_GA // GLA_V)),
            pl.BlockSpec((GLA_ROWS, LANE), lambda b, c: (rb(b, c), 0)),
            pl.BlockSpec((N_META, GLA_QK), lambda b, c: (0, P_GK // GLA_QK)),
            pl.BlockSpec((N_META, GLA_V), lambda b, c: (0, P_GV // GLA_V)),
            pl.BlockSpec((N_META, LANE), const),
            pl.BlockSpec((LANE, GLA_QK), const),
            pl.BlockSpec((1, GLA_QK), const),
            pl.BlockSpec((1, GLA_V), const),
        ],
        out_specs=[
            pl.BlockSpec((GLA_ROWS, GLA_V), lambda b, c: (rb(b, c), 0)),
            pl.BlockSpec((1, H_GLA, GLA_DK, GLA_DV), lambda b, c: (b, 0, 0, 0)),
        ],
        out_shape=[
            jax.ShapeDtypeStruct((batch * seq, GLA_V), BF16),
            jax.ShapeDtypeStruct((batch, H_GLA, GLA_DK, GLA_DV), F32),
        ],
        scratch_shapes=[pltpu.VMEM((H_GLA, GLA_DV, GLA_DK), F32)],
        compiler_params=pltpu.CompilerParams(
            dimension_semantics=("arbitrary", "arbitrary"), vmem_limit_bytes=VMEM_LIMIT),
        name="gla_prompt",
    )(p_main, p_main, p_main, p_main, alow, p_meta, p_meta, alow_meta, wup, bias, gn)


def _gla_sample_kernel(t_new, q_ref, k_ref, v_ref, sa_ref, al_ref, s0_ref,
                       wup_ref, b_ref, gn_ref, og_ref, so_ref, st_ref):
    for h in range(H_GLA):
        st_ref[h] = s0_ref[0, h].T
    outs = _gla_chunk(t_new, q_ref[...], k_ref[...], v_ref[...], al_ref[...],
                      wup_ref[...], b_ref[...], st_ref)
    for h in range(H_GLA):
        cols = slice(h * GLA_DV, (h + 1) * GLA_DV)
        og_ref[:, cols] = _gla_finish(outs[h], gn_ref[:, cols], sa_ref[:, cols])
        so_ref[0, h] = st_ref[h].T


def _gla_sample(p_main, alow, state, wup, bias, gn, row0, dec_batch, t_new):
    rb0 = row0 // t_new
    const = lambda b: (0, 0)
    return pl.pallas_call(
        functools.partial(_gla_sample_kernel, t_new),
        grid=(dec_batch,),
        in_specs=[
            pl.BlockSpec((t_new, GLA_QK), lambda b: (rb0 + b, P_GQ // GLA_QK)),
            pl.BlockSpec((t_new, GLA_QK), lambda b: (rb0 + b, P_GK // GLA_QK)),
            pl.BlockSpec((t_new, GLA_V), lambda b: (rb0 + b, P_GV // GLA_V)),
            pl.BlockSpec((t_new, GLA_V), lambda b: (rb0 + b, P_GA // GLA_V)),
            pl.BlockSpec((t_new, LANE), lambda b: (rb0 + b, 0)),
            pl.BlockSpec((1, H_GLA, GLA_DK, GLA_DV), lambda b: (b, 0, 0, 0)),
            pl.BlockSpec((LANE, GLA_QK), const),
            pl.BlockSpec((1, GLA_QK), const),
            pl.BlockSpec((1, GLA_V), const),
        ],
        out_specs=[
            pl.BlockSpec((t_new, GLA_V), lambda b: (b, 0)),
            pl.BlockSpec((1, H_GLA, GLA_DK, GLA_DV), lambda b: (b, 0, 0, 0)),
        ],
        out_shape=[
            jax.ShapeDtypeStruct((dec_batch * t_new, GLA_V), BF16),
            jax.ShapeDtypeStruct((dec_batch, H_GLA, GLA_DK, GLA_DV), F32),
        ],
        scratch_shapes=[pltpu.VMEM((H_GLA, GLA_DV, GLA_DK), F32)],
        compiler_params=pltpu.CompilerParams(
            dimension_semantics=("arbitrary",), vmem_limit_bytes=VMEM_LIMIT),
        name="gla_sample",
    )(p_main, p_main, p_main, p_main, alow, state, wup, bias, gn)


def _load_perm(ref, base, nk):
    parts = []
    for j in range(nk // SB_MINI):
        for c in range(SB_STRIDE):
            parts.append(ref[pl.ds(base + SB_MINI * j + c, SUBLANE, stride=SB_STRIDE), :])
    return jnp.concatenate(parts, axis=0)


def _sb_scan(zt, carry, limit):
    nk, nq = zt.shape
    rowid = lax.broadcasted_iota(jnp.int32, (SUBLANE, nq), 0)
    w_parts = [None] * (nk // SUBLANE)
    for j in reversed(range(nk // SB_MINI)):
        ls, ss, vis = [], [], []
        for c in range(SB_STRIDE):
            i = j * SB_STRIDE + c
            z = zt[i * SUBLANE:(i + 1) * SUBLANE]
            nz = -z
            t = jnp.log2(1.0 + jnp.exp2(jnp.minimum(z, nz)))
            l = jnp.minimum(nz, 0.0) - t
            ss.append(l + z)
            if limit is not None:
                v = rowid * SB_STRIDE + (SB_MINI * j + c) < limit
                l = jnp.where(v, l, 0.0)
                vis.append(v)
            ls.append(l)
        later = [None] * SB_STRIDE
        run = ls[SB_STRIDE - 1]
        for c in range(SB_STRIDE - 2, -1, -1):
            later[c] = run
            run = run + ls[c]
        incl = run
        for sh in (1, 2, 4):
            incl = incl + jnp.where(rowid < SUBLANE - sh, pltpu.roll(incl, SUBLANE - sh, axis=0), 0.0)
        off = carry + (incl - run)
        for c in range(SB_STRIDE):
            after = off if later[c] is None else off + later[c]
            w = jnp.exp2(ss[c] + after)
            if limit is not None:
                w = jnp.where(vis[c], w, 0.0)
            w_parts[j * SB_STRIDE + c] = w
        carry = carry + incl[0:1, :]
    return jnp.concatenate(w_parts, axis=0), carry


SB_PAR = 4


def _sb_prompt_kernel(seq, q_ref, k_ref, v_ref, km_ref, vm_ref, sb_ref, o_ref, kms_ref, vms_ref):
    kms_ref[...] = jnp.zeros_like(kms_ref)
    vms_ref[...] = jnp.zeros_like(vms_ref)
    for i in range(SB_PAR):
        kms_ref[i, 0:N_META, :] = km_ref[:, i * SB_HD:(i + 1) * SB_HD]
        vms_ref[i, 0:N_META, :] = vm_ref[:, i * SB_HD:(i + 1) * SB_HD]
    heads = range(SB_PAR)
    diag_limit = lax.broadcasted_iota(jnp.int32, (SUBLANE, SB_TILE), 1)

    def tile(kp, vp, q, carry, acc, limit):
        zt = _dot_nt(kp, q)
        wt, carry = _sb_scan(zt, carry, limit)
        return carry, acc + _dot_tn(vp, wt.astype(BF16))

    half = SB_TILE // 2

    def diag_tile(kp, vp, q):
        zt = _dot_nt(kp, q)
        c0 = jnp.zeros((1, half), F32)
        w_lo, c_lo = _sb_scan(zt[:half, :half], c0, diag_limit[:, :half])
        w_hi, c_hi = _sb_scan(zt[:, half:], c0, diag_limit[:, half:])
        w_lo = jnp.concatenate([w_lo, jnp.zeros((half, half), F32)], axis=0)
        wt = jnp.concatenate([w_lo, w_hi], axis=1).astype(BF16)
        return jnp.concatenate([c_lo, c_hi], axis=1), _dot_tn(vp, wt)

    def diag_tiles(base, qs):
        rows = pl.ds(base, SB_TILE)
        res = [diag_tile(k_ref[rows, i * SB_HD:(i + 1) * SB_HD], v_ref[rows, i * SB_HD:(i + 1) * SB_HD], qs[i])
               for i in heads]
        return [r[0] for r in res], [r[1] for r in res]

    def plain_tiles(t, qs, carries, accs):
        rows = pl.ds(pl.multiple_of(t * SB_TILE, SB_TILE), SB_TILE)
        res = [tile(k_ref[rows, i * SB_HD:(i + 1) * SB_HD], v_ref[rows, i * SB_HD:(i + 1) * SB_HD],
                    qs[i], carries[i], accs[i], None) for i in heads]
        return [r[0] for r in res], [r[1] for r in res]

    def meta_tiles(qs, carries, accs):
        res = [tile(_load_perm(kms_ref.at[i], 0, SB_MINI).astype(BF16),
                    _load_perm(vms_ref.at[i], 0, SB_MINI).astype(BF16),
                    qs[i], carries[i], accs[i], N_META) for i in heads]
        return [r[0] for r in res], [r[1] for r in res]

    def qblock(m, is_first):
        row0 = pl.multiple_of(m * SB_TILE, SB_TILE)
        rows = pl.ds(row0, SB_TILE)
        qs = [q_ref[rows, i * SB_HD:(i + 1) * SB_HD] for i in heads]
        carries, accs = diag_tiles(row0, qs)
        if is_first:
            _, accs = meta_tiles(qs, carries, accs)
        else:
            carries, accs = plain_tiles(m - 1, qs, carries, accs)

            def cond(st):
                t, carries, _ = st
                worst = functools.reduce(jnp.maximum, [jnp.max(c) for c in carries])
                return (t >= -1) & (worst > SB_CUTOFF_LOG2)

            def body(st):
                t, carries, accs = st
                carries, accs = lax.cond(t >= 0,
                                         lambda: plain_tiles(t, qs, carries, accs),
                                         lambda: meta_tiles(qs, carries, accs))
                return t - 1, carries, accs

            _, _, accs = lax.while_loop(cond, body, (m - 2, carries, accs))
        for i in heads:
            cols = slice(i * SB_HD, (i + 1) * SB_HD)
            o = accs[i].T * sb_ref[rows, cols].astype(F32)
            o_ref[rows, cols] = o.astype(BF16)

    qblock(0, True)

    def loop_body(m, carry):
        qblock(m, False)
        return carry

    lax.fori_loop(1, seq // SB_TILE, loop_body, 0)


def _sb_prompt(p_main, km32, vm32, batch, seq):
    gw = SB_PAR * SB_HD
    col = lambda off: pl.BlockSpec((seq, gw), lambda b, g: (b, off // gw + g))
    return pl.pallas_call(
        functools.partial(_sb_prompt_kernel, seq),
        grid=(batch, H_SB // SB_PAR),
        in_specs=[col(P_SQ), col(P_SK), col(P_SV),
                  pl.BlockSpec((N_META, gw), lambda b, g: (0, g)),
                  pl.BlockSpec((N_META, gw), lambda b, g: (0, g)),
                  col(P_GB)],
        out_specs=pl.BlockSpec((seq, gw), lambda b, g: (b, g)),
        out_shape=jax.ShapeDtypeStruct((batch * seq, SB_W), BF16),
        scratch_shapes=[pltpu.VMEM((SB_PAR, SB_MINI, SB_HD), F32), pltpu.VMEM((SB_PAR, SB_MINI, SB_HD), F32)],
        compiler_params=pltpu.CompilerParams(
            dimension_semantics=("arbitrary", "arbitrary"), vmem_limit_bytes=VMEM_LIMIT),
        name="sb_prompt",
    )(p_main, p_main, p_main, km32, vm32, p_main)


SB_GROUP = 4


def _load_perm_native(ref, key0, nk, head):
    parts = []
    for j in range(nk // SB_MINI):
        for c in range(SB_STRIDE):
            start = (key0 + SB_MINI * j + c) * H_SB + head
            parts.append(ref[pl.ds(start, SUBLANE, stride=SB_STRIDE * H_SB), :])
    return jnp.concatenate(parts, axis=0)


def _sb_sample_kernel(t_new, past, q_ref, kn_ref, vn_ref, ck_ref, cv_ref, sb_ref, o_ref,
                      kbuf, vbuf, kfar, vfar, sems, far_sems):
    b = pl.program_id(0)
    nb = pl.num_programs(0)
    gw = SB_GROUP * SB_HD
    nq = SB_GROUP * t_new
    ngroups = H_SB // SB_GROUP
    n_tiles = past // SB_TILE
    tile_rows = SB_TILE * H_SB

    def near_copies(bb, slot):
        rows = pl.ds((n_tiles - 1) * tile_rows, tile_rows)
        return (pltpu.make_async_copy(ck_ref.at[bb, rows, :], kbuf.at[slot], sems.at[slot, 0]),
                pltpu.make_async_copy(cv_ref.at[bb, rows, :], vbuf.at[slot], sems.at[slot, 1]))

    @pl.when(b == 0)
    def _():
        for cp in near_copies(0, 0):
            cp.start()

    slot = b % 2

    @pl.when(b + 1 < nb)
    def _():
        for cp in near_copies(b + 1, 1 - slot):
            cp.start()

    for cp in near_copies(b, slot):
        cp.wait()

    rh = lax.broadcasted_iota(jnp.int32, (gw, nq), 0) // SB_HD
    ch = lax.broadcasted_iota(jnp.int32, (gw, nq), 1) // t_new
    head_match = rh == ch
    new_limit = lax.broadcasted_iota(jnp.int32, (SUBLANE, nq), 1) % t_new

    def tile(kref, vref, nk, g, qbd, carry, acc, limit):
        hs = range(SB_GROUP * g, SB_GROUP * (g + 1))
        kp = jnp.concatenate([_load_perm_native(kref, 0, nk, h) for h in hs], axis=1).astype(BF16)
        vp = jnp.concatenate([_load_perm_native(vref, 0, nk, h) for h in hs], axis=1).astype(BF16)
        zt = _dot(kp, qbd)
        wt, carry = _sb_scan(zt, carry, limit)
        return carry, acc + _dot_tn(vp, wt.astype(BF16))

    qbds, carries, accs = [], [], []
    for g in range(ngroups):
        q = q_ref[:, g * gw:(g + 1) * gw].astype(F32)
        qt = jnp.concatenate([q] * SB_GROUP, axis=0).T
        qbd = jnp.where(head_match, qt, 0.0).astype(BF16)
        carry = jnp.zeros((1, nq), F32)
        acc = jnp.zeros((gw, nq), F32)
        carry, acc = tile(kn_ref, vn_ref, t_new, g, qbd, carry, acc, new_limit)
        carry, acc = tile(kbuf.at[slot], vbuf.at[slot], SB_TILE, g, qbd, carry, acc, None)
        qbds.append(qbd)
        carries.append(carry)
        accs.append(acc)

    def cond(st):
        t, carries, _ = st
        worst = functools.reduce(jnp.maximum, [jnp.max(c) for c in carries])
        return (t >= 0) & (worst > SB_CUTOFF_LOG2)

    def body(st):
        t, carries, accs = st
        rows = pl.ds(pl.multiple_of(t * tile_rows, tile_rows), tile_rows)
        far = (pltpu.make_async_copy(ck_ref.at[b, rows, :], kfar, far_sems.at[0]),
               pltpu.make_async_copy(cv_ref.at[b, rows, :], vfar, far_sems.at[1]))
        for cp in far:
            cp.start()
        for cp in far:
            cp.wait()
        res = [tile(kfar, vfar, SB_TILE, g, qbds[g], carries[g], accs[g], None) for g in range(ngroups)]
        return t - 1, [r[0] for r in res], [r[1] for r in res]

    _, _, accs = lax.while_loop(cond, body, (n_tiles - 2, carries, accs))
    for g in range(ngroups):
        at = accs[g].T
        for h in range(SB_GROUP):
            cols = slice(g * gw + h * SB_HD, g * gw + (h + 1) * SB_HD)
            o = at[h * t_new:(h + 1) * t_new, h * SB_HD:(h + 1) * SB_HD] * sb_ref[:, cols].astype(F32)
            o_ref[:, cols] = o.astype(BF16)


def _sb_sample(p_main, k5s, v5s, cache_k, cache_v, row0, dec_batch, t_new, past):
    rb0 = row0 // t_new
    tile_rows = SB_TILE * H_SB
    any_spec = pl.BlockSpec(memory_space=pl.ANY)
    new_spec = pl.BlockSpec((t_new * H_SB, SB_HD), lambda b: (b, 0))
    return pl.pallas_call(
        functools.partial(_sb_sample_kernel, t_new, past),
        grid=(dec_batch,),
        in_specs=[pl.BlockSpec((t_new, SB_W), lambda b: (rb0 + b, P_SQ // SB_W)),
                  new_spec, new_spec, any_spec, any_spec,
                  pl.BlockSpec((t_new, SB_W), lambda b: (rb0 + b, P_GB // SB_W))],
        out_specs=pl.BlockSpec((t_new, SB_W), lambda b: (b, 0)),
        out_shape=jax.ShapeDtypeStruct((dec_batch * t_new, SB_W), BF16),
        scratch_shapes=[
            pltpu.VMEM((2, tile_rows, SB_HD), F32),
            pltpu.VMEM((2, tile_rows, SB_HD), F32),
            pltpu.VMEM((tile_rows, SB_HD), F32),
            pltpu.VMEM((tile_rows, SB_HD), F32),
            pltpu.SemaphoreType.DMA((2, 2)),
            pltpu.SemaphoreType.DMA((2,)),
        ],
        compiler_params=pltpu.CompilerParams(
            dimension_semantics=("arbitrary",), vmem_limit_bytes=VMEM_LIMIT),
        name="sb_sample",
    )(p_main, k5s, v5s, cache_k, cache_v, p_main)


def _merge_kernel(n_prompt_tiles, ogp_ref, ogs_ref, obp_ref, obs_ref, xp_ref, xs_ref, w_ref, o_ref):
    m = pl.program_id(0)

    def run(og_ref, ob_ref, x_ref):
        mix = (og_ref[...].astype(F32) + ob_ref[...].astype(F32)).astype(BF16)
        o_ref[...] = x_ref[...] + _dot(mix, w_ref[...])

    @pl.when(m < n_prompt_tiles)
    def _():
        run(ogp_ref, obp_ref, xp_ref)

    @pl.when(m >= n_prompt_tiles)
    def _():
        run(ogs_ref, obs_ref, xs_ref)


def _merge(og_p, og_s, ob_p, ob_s, xp, xs, w_out):
    npt, nst = xp.shape[0] // TM, xs.shape[0] // TM
    pidx = lambda m: (jnp.minimum(m, npt - 1), 0)
    sidx = lambda m: (jnp.maximum(m - npt, 0), 0)
    row = lambda idx: pl.BlockSpec((TM, D_MODEL), idx)
    return pl.pallas_call(
        functools.partial(_merge_kernel, npt),
        grid=(npt + nst,),
        in_specs=[row(pidx), row(sidx), row(pidx), row(sidx), row(pidx), row(sidx),
                  pl.BlockSpec((D_MODEL, D_MODEL), lambda m: (0, 0))],
        out_specs=pl.BlockSpec((TM, D_MODEL), lambda m: (m, 0)),
        out_shape=jax.ShapeDtypeStruct((xp.shape[0] + xs.shape[0], D_MODEL), F32),
        compiler_params=pltpu.CompilerParams(
            dimension_semantics=("arbitrary",), vmem_limit_bytes=VMEM_LIMIT),
        name="merge_outproj",
    )(og_p, og_s, ob_p, ob_s, xp, xs, w_out)


def _ffn_kernel(n_prompt_tiles, x_ref, g2_ref, wu_ref, wd_ref, gf_ref, yp_ref, ys_ref, h_ref, acc_ref):
    m = pl.program_id(0)
    f = pl.program_id(1)

    @pl.when(f == 0)
    def _():
        h_ref[...] = _rms(x_ref[...], g2_ref[...]).astype(BF16)
        acc_ref[...] = jnp.zeros_like(acc_ref)

    u = jnp.maximum(_dot(h_ref[...], wu_ref[...]), 0.0)
    acc_ref[...] += _dot((u * u).astype(BF16), wd_ref[...])

    @pl.when(f == pl.num_programs(1) - 1)
    def _():
        y = _rms(x_ref[...] + acc_ref[...], gf_ref[...])

        @pl.when(m < n_prompt_tiles)
        def _():
            yp_ref[...] = y

        @pl.when(m >= n_prompt_tiles)
        def _():
            ys_ref[...] = y


def _ffn(x1, g2, w_up, w_down, gf, npr, nsr):
    npt, nst = npr // TM, nsr // TM
    return pl.pallas_call(
        functools.partial(_ffn_kernel, npt),
        grid=(npt + nst, D_FF // TF),
        in_specs=[
            pl.BlockSpec((TM, D_MODEL), lambda m, f: (m, 0)),
            pl.BlockSpec((1, D_MODEL), lambda m, f: (0, 0)),
            pl.BlockSpec((D_MODEL, TF), lambda m, f: (0, f)),
            pl.BlockSpec((TF, D_MODEL), lambda m, f: (f, 0)),
            pl.BlockSpec((1, D_MODEL), lambda m, f: (0, 0)),
        ],
        out_specs=[
            pl.BlockSpec((TM, D_MODEL), lambda m, f: (jnp.minimum(m, npt - 1), 0)),
            pl.BlockSpec((TM, D_MODEL), lambda m, f: (jnp.maximum(m - npt, 0), 0)),
        ],
        out_shape=[
            jax.ShapeDtypeStruct((npr, D_MODEL), F32),
            jax.ShapeDtypeStruct((nsr, D_MODEL), F32),
        ],
        scratch_shapes=[pltpu.VMEM((TM, D_MODEL), BF16), pltpu.VMEM((TM, D_MODEL), F32)],
        compiler_params=pltpu.CompilerParams(
            dimension_semantics=("arbitrary", "arbitrary"), vmem_limit_bytes=VMEM_LIMIT),
        name="ffn_final",
    )(x1, g2, w_up, w_down, gf)


def kernel(x_prompt, x_sample, cache_sb_k, cache_sb_v, state_gla, meta_tokens, norm1_g, w_in,
           w_alpha_up, b_alpha, gla_norm_g, w_out, norm2_g, w_up, w_down, norm_f_g):
    batch, seq, _ = x_prompt.shape
    dec_batch, t_new, _ = x_sample.shape
    depth, _, past = cache_sb_k.shape[:3]
    assert depth == 1 and w_in.shape[2] == _R_END
    assert seq % TM == 0 and (dec_batch * t_new) % TM == 0 and past % SB_TILE == 0
    assert t_new == SB_MINI and t_new % 16 == 0
    npr, nsr = batch * seq, dec_batch * t_new

    w_main, w_alow = _prep_w_in(w_in[0].T)
    wup = jnp.pad(w_alpha_up[0], ((0, LANE - GLA_RANK), (0, 0))).astype(BF16)
    bias = b_alpha[0].reshape(1, GLA_QK)
    gn = gla_norm_g[0].reshape(1, GLA_V)
    g1 = norm1_g[0].reshape(1, D_MODEL)
    g2 = norm2_g[0].reshape(1, D_MODEL)
    gf = norm_f_g.reshape(1, D_MODEL)

    xp = x_prompt.reshape(npr, D_MODEL)
    xs = x_sample.reshape(nsr, D_MODEL)
    meta = meta_tokens.astype(x_prompt.dtype)

    (p_main, p_meta, km32, vm32, alow, alow_meta, k5p, v5p, k5s, v5s,
     w_out_b, w_up_b, w_down_b) = _inproj(xp, xs, meta, g1, w_main, w_alow,
                                          w_out[0], w_up[0], w_down[0], batch, seq)

    og_p, st_p = _gla_prompt(p_main, p_meta, alow, alow_meta, wup, bias, gn, batch, seq)
    og_s, st_s = _gla_sample(p_main, alow, state_gla[0], wup, bias, gn, npr, dec_batch, t_new)
    ob_p = _sb_prompt(p_main, km32, vm32, batch, seq)
    ck = cache_sb_k[0].reshape(dec_batch, past * H_SB, SB_HD)
    cv = cache_sb_v[0].reshape(dec_batch, past * H_SB, SB_HD)
    ob_s = _sb_sample(p_main, k5s.reshape(nsr * H_SB, SB_HD), v5s.reshape(nsr * H_SB, SB_HD),
                      ck, cv, npr, dec_batch, t_new, past)

    x1 = _merge(og_p, og_s, ob_p, ob_s, xp, xs, w_out_b)
    y_p, y_s = _ffn(x1, g2, w_up_b, w_down_b, gf, npr, nsr)

    return (
        y_p.reshape(batch, seq, D_MODEL),
        y_s.reshape(dec_batch, t_new, D_MODEL),
        st_p[None],
        k5p.reshape(1, batch, N_META + seq, H_SB, SB_HD),
        v5p.reshape(1, batch, N_META + seq, H_SB, SB_HD),
        st_s[None],
        k5s.reshape(1, dec_batch, t_new, H_SB, SB_HD),
        v5s.reshape(1, dec_batch, t_new, H_SB, SB_HD),
    )
```

```python
import functools

import jax
import jax.numpy as jnp
from jax import lax
from jax.experimental import pallas as pl
from jax.experimental.pallas import tpu as pltpu

F32 = jnp.float32
BF16 = jnp.bfloat16

D_MODEL = 2048
N_META = 16
H_GLA = 4
GLA_DK = 256
GLA_DV = 512
GLA_RANK = 16
GLA_TAU = 16.0
GLA_CHUNK = 64
H_SB = 16
SB_HD = 128
D_FF = 4 * D_MODEL
EPS = 1e-5
GLA_QK = H_GLA * GLA_DK
GLA_V = H_GLA * GLA_DV
SB_W = H_SB * SB_HD

_R_GK = GLA_QK
_R_GV = 2 * GLA_QK
_R_ALOW = _R_GV + GLA_V
_R_SQ = _R_ALOW + GLA_RANK
_R_SK = _R_SQ + SB_W
_R_SV = _R_SK + SB_W
_R_GA = _R_SV + SB_W
_R_GB = _R_GA + GLA_V
_R_END = _R_GB + SB_W

P_GQ = 0
P_GK = P_GQ + GLA_QK
P_GV = P_GK + GLA_QK
P_SQ = P_GV + GLA_V
P_SK = P_SQ + SB_W
P_SV = P_SK + SB_W
P_GA = P_SV + SB_W
P_GB = P_GA + GLA_V
P_W = P_GB + SB_W

LANE = 128
SUBLANE = 8
TM = 512
TN = 2048
TF = 1024
GLA_ROWS = 256
SB_TILE = 256
SB_MINI = 32
SB_STRIDE = SB_MINI // SUBLANE
SB_CUTOFF = -104.0
LOG2_E = 1.4426950408889634
SB_SCALE_LOG2 = SB_HD ** -0.5 * LOG2_E
SB_CUTOFF_LOG2 = SB_CUTOFF * LOG2_E
VMEM_LIMIT = 56 * 1024 * 1024


def _dot(a, b):
    return jnp.dot(a, b, preferred_element_type=F32)


def _dot_nt(a, b):
    return lax.dot_general(a, b, (((1,), (1,)), ((), ())), preferred_element_type=F32)


def _dot_tn(a, b):
    return lax.dot_general(a, b, (((0,), (0,)), ((), ())), preferred_element_type=F32)


def _split2(x):
    hi = x.astype(BF16)
    lo = (x - hi.astype(F32)).astype(BF16)
    return hi, lo


def _rms(x, g):
    ms = jnp.mean(x * x, axis=-1, keepdims=True)
    return x * lax.rsqrt(ms + EPS) * g


def _log_sigmoid(x):
    return jnp.minimum(x, 0.0) - jnp.log(1.0 + jnp.exp(-jnp.abs(x)))


_PREP_ROWS = 1024
_PREP_COLS = 1024


def _prep_w_kernel(n_aligned, wt_ref, edge_ref, o_ref, al_ref):
    n = pl.program_id(1)

    @pl.when(n < n_aligned)
    def _():
        o_ref[...] = wt_ref[...].T.astype(BF16)

    def shifted():
        return jnp.concatenate([wt_ref[GLA_RANK:, :], edge_ref[...]], axis=0).T

    is_sq = (n >= P_SQ // _PREP_COLS) & (n < P_SK // _PREP_COLS)

    @pl.when((n >= n_aligned) & jnp.logical_not(is_sq))
    def _():
        o_ref[...] = shifted().astype(BF16)

    @pl.when(is_sq)
    def _():
        o_ref[...] = (shifted() * SB_SCALE_LOG2).astype(BF16)

    @pl.when(n == n_aligned)
    def _():
        pad = jnp.zeros((LANE - GLA_RANK, wt_ref.shape[1]), F32)
        al_ref[...] = jnp.concatenate([wt_ref[0:GLA_RANK, :], pad], axis=0).T.astype(BF16)


def _prep_w_in(wt):
    n_aligned = _R_ALOW // _PREP_COLS
    edge_blocks = _PREP_COLS // GLA_RANK
    return pl.pallas_call(
        functools.partial(_prep_w_kernel, n_aligned),
        grid=(D_MODEL // _PREP_ROWS, P_W // _PREP_COLS),
        in_specs=[
            pl.BlockSpec((_PREP_COLS, _PREP_ROWS), lambda r, n: (n, r)),
            pl.BlockSpec((GLA_RANK, _PREP_ROWS), lambda r, n: ((jnp.maximum(n, n_aligned) + 1) * edge_blocks, r)),
        ],
        out_specs=[pl.BlockSpec((_PREP_ROWS, _PREP_COLS), lambda r, n: (r, n)),
                   pl.BlockSpec((_PREP_ROWS, LANE), lambda r, n: (r, 0))],
        out_shape=[jax.ShapeDtypeStruct((D_MODEL, P_W), BF16),
                   jax.ShapeDtypeStruct((D_MODEL, LANE), BF16)],
        compiler_params=pltpu.CompilerParams(
            dimension_semantics=("arbitrary", "arbitrary"), vmem_limit_bytes=VMEM_LIMIT),
        name="prep_w_in",
    )(wt, wt)


_N_SK0 = P_SK // TN
_N_SV0 = P_SV // TN
_N_GATE0 = P_GA // TN


assert TN == SB_W


def _permute_rows(stage_ref):
    heads, rows, _ = stage_ref.shape
    cols = []
    for h in range(heads):
        parts = [stage_ref[h, pl.ds(SB_MINI * j + c, SUBLANE, stride=SB_STRIDE), :]
                 for j in range(rows // SB_MINI) for c in range(SB_STRIDE)]
        cols.append(jnp.concatenate(parts, axis=0).astype(BF16))
    return jnp.concatenate(cols, axis=1)


MAX_CAST_SLABS = 64


def _inproj_kernel(n_prompt_tiles, batch, seq, n_cast_slabs, xp_ref, xs_ref, meta_ref, g1_ref, w_ref, wal_ref,
                   wo_ref, wu_ref, wd_ref,
                   p_ref, pm_ref, km_ref, vm_ref, al_ref, alm_ref,
                   kp_ref, vp_ref, ks_ref, vs_ref, wob_ref, wub_ref, wdb_ref,
                   h_ref, kstage_ref, vstage_ref, mstage_ref, sems, msem):
    m = pl.program_id(0)
    n = pl.program_id(1)
    n_tiles = pl.num_programs(0)
    tiles_per_batch = seq // TM

    @pl.when(m * pl.num_programs(1) + n < n_cast_slabs)
    def _():
        wob_ref[...] = wo_ref[...].astype(BF16)
        wub_ref[...] = wu_ref[...].astype(BF16)
        wdb_ref[...] = wd_ref[...].astype(BF16)

    def stage_heads(ref, vals):
        for h in range(H_SB):
            ref[h] = vals[:, h * SB_HD:(h + 1) * SB_HD]

    def native_copies(src_ref, rows, dst_ref, token, dma_sem):
        return [pltpu.make_async_copy(src_ref.at[h], dst_ref.at[pl.ds(token, rows), h, :], dma_sem)
                for h in range(H_SB)]

    def wait_native(stage_ref, sem):
        for cp in native_copies(stage_ref, TM, kp_ref, 0, sem):
            cp.wait()

    def write_native(stage_ref, sem, dst_p, dst_s):
        @pl.when(m < n_prompt_tiles)
        def _():
            b = m // tiles_per_batch
            token = b * (seq + N_META) + N_META + (m - b * tiles_per_batch) * TM
            for cp in native_copies(stage_ref, TM, dst_p, token, sem):
                cp.start()

        @pl.when(m >= n_prompt_tiles)
        def _():
            for cp in native_copies(stage_ref, TM, dst_s, (m - n_prompt_tiles) * TM, sem):
                cp.start()

    def write_native_meta(met, dst_p):
        stage_heads(mstage_ref, met)
        copies = [cp for b in range(batch)
                  for cp in native_copies(mstage_ref, N_META, dst_p, b * (seq + N_META), msem)]
        for cp in copies:
            cp.start()
        for cp in copies:
            cp.wait()

    first = m == 0

    @pl.when(n == 0)
    def _():
        g = g1_ref[...]

        @pl.when(m < n_prompt_tiles)
        def _():
            h_ref[0:TM, :] = _rms(xp_ref[...], g).astype(BF16)

        @pl.when(m >= n_prompt_tiles)
        def _():
            h_ref[0:TM, :] = _rms(xs_ref[...], g).astype(BF16)

        al_ref[...] = _dot(h_ref[0:TM, :], wal_ref[...])

        @pl.when(first)
        def _():
            h_ref[TM:TM + N_META, :] = _rms(meta_ref[...], g).astype(BF16)
            alm_ref[...] = _dot(h_ref[TM:TM + N_META, :], wal_ref[...])

    is_gate = n >= _N_GATE0
    is_k = (n >= _N_SK0) & (n < _N_SV0)
    is_v = (n >= _N_SV0) & (n < _N_GATE0)

    def column_tile(kind, epilogue, meta_epilogue):
        @pl.when(kind & first)
        def _():
            acc = _dot(h_ref[...], w_ref[...])
            epilogue(acc[:TM])
            pm_ref[...] = acc[TM:].astype(BF16)
            meta_epilogue(acc[TM:])

        @pl.when(kind & jnp.logical_not(first))
        def _():
            epilogue(_dot(h_ref[0:TM, :], w_ref[...]))

    def plain(main):
        p_ref[...] = main.astype(BF16)

    def gate(main):
        p_ref[...] = jax.nn.sigmoid(main).astype(BF16)

    def kv(stage_ref, sem, dst_p, dst_s):
        def epilogue(main):
            stage_heads(stage_ref, main)
            write_native(stage_ref, sem, dst_p, dst_s)
            p_ref[...] = _permute_rows(stage_ref)
        return epilogue

    def kv_meta(f32_ref, dst_p):
        def epilogue(met):
            f32_ref[...] = met
            write_native_meta(met, dst_p)
        return epilogue

    @pl.when(is_k & (m > 0))
    def _():
        wait_native(kstage_ref, sems.at[0])

    @pl.when(is_v & (m > 0))
    def _():
        wait_native(vstage_ref, sems.at[1])

    column_tile(jnp.logical_not(is_gate | is_k | is_v), plain, lambda met: None)
    column_tile(is_gate, gate, lambda met: None)
    column_tile(is_k, kv(kstage_ref, sems.at[0], kp_ref, ks_ref), kv_meta(km_ref, kp_ref))
    column_tile(is_v, kv(vstage_ref, sems.at[1], vp_ref, vs_ref), kv_meta(vm_ref, vp_ref))

    @pl.when((m == n_tiles - 1) & (n == pl.num_programs(1) - 1))
    def _():
        wait_native(kstage_ref, sems.at[0])
        wait_native(vstage_ref, sems.at[1])


def _inproj(xp, xs, meta, g1, w_main, w_alow, w_out, w_up, w_down, batch, seq):
    npr, nsr = xp.shape[0], xs.shape[0]
    npt, nst = npr // TM, nsr // TM
    nt = npt + nst
    rows = npr + nsr
    nn = P_W // TN
    n_slabs = min(MAX_CAST_SLABS, 1 << ((nt * nn).bit_length() - 1))
    slab = lambda m, n: (jnp.minimum(m * nn + n, n_slabs - 1), 0)
    cast_specs = [pl.BlockSpec((w.shape[0] // n_slabs, w.shape[1]), slab) for w in (w_out, w_up, w_down)]
    kcol = lambda n: jnp.clip(n - _N_SK0, 0, SB_W // TN - 1)
    vcol = lambda n: jnp.clip(n - _N_SV0, 0, SB_W // TN - 1)
    once = lambda m, col, last: jnp.where(m == 0, col, last)
    any_spec = pl.BlockSpec(memory_space=pl.ANY)
    return pl.pallas_call(
        functools.partial(_inproj_kernel, npt, batch, seq, n_slabs),
        grid=(nt, nn),
        in_specs=[
            pl.BlockSpec((TM, D_MODEL), lambda m, n: (jnp.minimum(m, npt - 1), 0)),
            pl.BlockSpec((TM, D_MODEL), lambda m, n: (jnp.maximum(m - npt, 0), 0),
                         pipeline_mode=pl.Buffered(1)),
            pl.BlockSpec((N_META, D_MODEL), lambda m, n: (0, 0)),
            pl.BlockSpec((1, D_MODEL), lambda m, n: (0, 0)),
            pl.BlockSpec((D_MODEL, TN), lambda m, n: (0, n)),
            pl.BlockSpec((D_MODEL, LANE), lambda m, n: (0, 0)),
        ] + cast_specs,
        out_specs=[
            pl.BlockSpec((TM, TN), lambda m, n: (m, n)),
            pl.BlockSpec((N_META, TN), lambda m, n: (0, once(m, n, nn - 1))),
            pl.BlockSpec((N_META, TN), lambda m, n: (0, once(m, kcol(n), SB_W // TN - 1))),
            pl.BlockSpec((N_META, TN), lambda m, n: (0, once(m, vcol(n), SB_W // TN - 1))),
            pl.BlockSpec((TM, LANE), lambda m, n: (m, 0)),
            pl.BlockSpec((N_META, LANE), lambda m, n: (0, 0)),
            any_spec, any_spec, any_spec, any_spec,
        ] + cast_specs,
        out_shape=[
            jax.ShapeDtypeStruct((rows, P_W), BF16),
            jax.ShapeDtypeStruct((N_META, P_W), BF16),
            jax.ShapeDtypeStruct((N_META, SB_W), F32),
            jax.ShapeDtypeStruct((N_META, SB_W), F32),
            jax.ShapeDtypeStruct((rows, LANE), F32),
            jax.ShapeDtypeStruct((N_META, LANE), F32),
            jax.ShapeDtypeStruct((batch * (seq + N_META), H_SB, SB_HD), F32),
            jax.ShapeDtypeStruct((batch * (seq + N_META), H_SB, SB_HD), F32),
            jax.ShapeDtypeStruct((nsr, H_SB, SB_HD), F32),
            jax.ShapeDtypeStruct((nsr, H_SB, SB_HD), F32),
        ] + [jax.ShapeDtypeStruct(w.shape, BF16) for w in (w_out, w_up, w_down)],
        scratch_shapes=[
            pltpu.VMEM((TM + N_META, D_MODEL), BF16),
            pltpu.VMEM((H_SB, TM, SB_HD), F32),
            pltpu.VMEM((H_SB, TM, SB_HD), F32),
            pltpu.VMEM((H_SB, N_META, SB_HD), F32),
            pltpu.SemaphoreType.DMA((2,)),
            pltpu.SemaphoreType.DMA(()),
        ],
        compiler_params=pltpu.CompilerParams(
            dimension_semantics=("arbitrary", "arbitrary"), vmem_limit_bytes=VMEM_LIMIT),
        name="inproj",
    )(xp, xs, meta, g1, w_main, w_alow, w_out, w_up, w_down)


def _gla_chunk(c, q, k, v, al, wup, bias, st_ref):
    x = _dot(al.astype(BF16), wup) + bias
    g = _log_sigmoid(x) * (1.0 / GLA_TAU)
    g_hi, g_lo = _split2(g)
    row = lax.broadcasted_iota(jnp.int32, (c, c), 0)
    col = lax.broadcasted_iota(jnp.int32, (c, c), 1)
    causal = col <= row
    tri = causal.astype(BF16)
    cum = _dot(tri, g_hi) + _dot(tri, g_lo)
    outs = []
    for h in range(H_GLA):
        ch = cum[:, h * GLA_DK:(h + 1) * GLA_DK]
        last = ch[c - 1:c, :]
        qh = q[:, h * GLA_DK:(h + 1) * GLA_DK].astype(F32)
        kh = k[:, h * GLA_DK:(h + 1) * GLA_DK].astype(F32)
        vh = v[:, h * GLA_DV:(h + 1) * GLA_DV]
        qd = (qh * jnp.exp(ch) * (GLA_DK ** -0.5)).astype(BF16)
        kd = (kh * jnp.exp(-ch)).astype(BF16)
        kr = (kh * jnp.exp(last - ch)).astype(BF16)
        att = jnp.where(causal, _dot_nt(qd, kd), 0.0).astype(BF16)
        st = st_ref[h]
        outs.append(_dot(att, vh) + _dot_nt(qd, st.astype(BF16)))
        st_ref[h] = st * jnp.exp(last) + _dot_tn(vh, kr)
    return outs


def _gla_finish(o, gn, sa):
    ms = jnp.mean(o * o, axis=-1, keepdims=True)
    return (o * lax.rsqrt(ms + EPS) * gn * sa.astype(F32)).astype(BF16)


def _gla_prompt_kernel(q_ref, k_ref, v_ref, sa_ref, al_ref, km_ref, vm_ref, alm_ref,
                       wup_ref, b_ref, gn_ref, og_ref, so_ref, st_ref):
    c_idx = pl.program_id(1)
    wup = wup_ref[...]
    bias = b_ref[...]

    @pl.when(c_idx == 0)
    def _():
        st_ref[...] = jnp.zeros_like(st_ref)
        zq = jnp.zeros((N_META, GLA_QK), BF16)
        _gla_chunk(N_META, zq, km_ref[...], vm_ref[...], alm_ref[...], wup, bias, st_ref)

    def body(i, carry):
        r0 = pl.multiple_of(i * GLA_CHUNK, GLA_CHUNK)
        rows = pl.ds(r0, GLA_CHUNK)
        outs = _gla_chunk(GLA_CHUNK, q_ref[rows, :], k_ref[rows, :], v_ref[rows, :], al_ref[rows, :],
                          wup, bias, st_ref)
        for h in range(H_GLA):
            cols = slice(h * GLA_DV, (h + 1) * GLA_DV)
            og_ref[rows, cols] = _gla_finish(outs[h], gn_ref[:, cols], sa_ref[rows, cols])
        return carry

    lax.fori_loop(0, GLA_ROWS // GLA_CHUNK, body, 0, unroll=True)

    @pl.when(c_idx == pl.num_programs(1) - 1)
    def _():
        for h in range(H_GLA):
            so_ref[0, h] = st_ref[h].T


def _gla_prompt(p_main, p_meta, alow, alow_meta, wup, bias, gn, batch, seq):
    nc = seq // GLA_ROWS
    rb = lambda b, c: b * nc + c
    const = lambda b, c: (0, 0)
    return pl.pallas_call(
        _gla_prompt_kernel,
        grid=(batch, nc),
        in_specs=[
            pl.BlockSpec((GLA_ROWS, GLA_QK), lambda b, c: (rb(b, c), P_GQ // GLA_QK)),
            pl.BlockSpec((GLA_ROWS, GLA_QK), lambda b, c: (rb(b, c), P_GK // GLA_QK)),
            pl.BlockSpec((GLA_ROWS, GLA_V), lambda b, c: (rb(b, c), P_GV // GLA_V)),
            pl.BlockSpec((GLA_ROWS, GLA_V), lambda b, c: (rb(b, c), P_GA // GLA_V)),
            pl.BlockSpec((GLA_ROWS, LANE), lambda b, c: (rb(b, c), 0)),
            pl.BlockSpec((N_META, GLA_QK), lambda b, c: (0, P_GK // GLA_QK)),
            pl.BlockSpec((N_META, GLA_V), lambda b, c: (0, P_GV // GLA_V)),
            pl.BlockSpec((N_META, LANE), const),
            pl.BlockSpec((LANE, GLA_QK), const),
            pl.BlockSpec((1, GLA_QK), const),
            pl.BlockSpec((1, GLA_V), const),
        ],
        out_specs=[
            pl.BlockSpec((GLA_ROWS, GLA_V), lambda b, c: (rb(b, c), 0)),
            pl.BlockSpec((1, H_GLA, GLA_DK, GLA_DV), lambda b, c: (b, 0, 0, 0)),
        ],
        out_shape=[
            jax.ShapeDtypeStruct((batch * seq, GLA_V), BF16),
            jax.ShapeDtypeStruct((batch, H_GLA, GLA_DK, GLA_DV), F32),
        ],
        scratch_shapes=[pltpu.VMEM((H_GLA, GLA_DV, GLA_DK), F32)],
        compiler_params=pltpu.CompilerParams(
            dimension_semantics=("arbitrary", "arbitrary"), vmem_limit_bytes=VMEM_LIMIT),
        name="gla_prompt",
    )(p_main, p_main, p_main, p_main, alow, p_meta, p_meta, alow_meta, wup, bias, gn)


def _gla_sample_kernel(t_new, q_ref, k_ref, v_ref, sa_ref, al_ref, s0_ref,
                       wup_ref, b_ref, gn_ref, og_ref, so_ref, st_ref):
    for h in range(H_GLA):
        st_ref[h] = s0_ref[0, h].T
    outs = _gla_chunk(t_new, q_ref[...], k_ref[...], v_ref[...], al_ref[...],
                      wup_ref[...], b_ref[...], st_ref)
    for h in range(H_GLA):
        cols = slice(h * GLA_DV, (h + 1) * GLA_DV)
        og_ref[:, cols] = _gla_finish(outs[h], gn_ref[:, cols], sa_ref[:, cols])
        so_ref[0, h] = st_ref[h].T


def _gla_sample(p_main, alow, state, wup, bias, gn, row0, dec_batch, t_new):
    rb0 = row0 // t_new
    const = lambda b: (0, 0)
    return pl.pallas_call(
        functools.partial(_gla_sample_kernel, t_new),
        grid=(dec_batch,),
        in_specs=[
            pl.BlockSpec((t_new, GLA_QK), lambda b: (rb0 + b, P_GQ // GLA_QK)),
            pl.BlockSpec((t_new, GLA_QK), lambda b: (rb0 + b, P_GK // GLA_QK)),
            pl.BlockSpec((t_new, GLA_V), lambda b: (rb0 + b, P_GV // GLA_V)),
            pl.BlockSpec((t_new, GLA_V), lambda b: (rb0 + b, P_GA // GLA_V)),
            pl.BlockSpec((t_new, LANE), lambda b: (rb0 + b, 0)),
            pl.BlockSpec((1, H_GLA, GLA_DK, GLA_DV), lambda b: (b, 0, 0, 0)),
            pl.BlockSpec((LANE, GLA_QK), const),
            pl.BlockSpec((1, GLA_QK), const),
            pl.BlockSpec((1, GLA_V), const),
        ],
        out_specs=[
            pl.BlockSpec((t_new, GLA_V), lambda b: (b, 0)),
            pl.BlockSpec((1, H_GLA, GLA_DK, GLA_DV), lambda b: (b, 0, 0, 0)),
        ],
        out_shape=[
            jax.ShapeDtypeStruct((dec_batch * t_new, GLA_V), BF16),
            jax.ShapeDtypeStruct((dec_batch, H_GLA, GLA_DK, GLA_DV), F32),
        ],
        scratch_shapes=[pltpu.VMEM((H_GLA, GLA_DV, GLA_DK), F32)],
        compiler_params=pltpu.CompilerParams(
            dimension_semantics=("arbitrary",), vmem_limit_bytes=VMEM_LIMIT),
        name="gla_sample",
    )(p_main, p_main, p_main, p_main, alow, state, wup, bias, gn)


def _load_perm(ref, base, nk):
    parts = []
    for j in range(nk // SB_MINI):
        for c in range(SB_STRIDE):
            parts.append(ref[pl.ds(base + SB_MINI * j + c, SUBLANE, stride=SB_STRIDE), :])
    return jnp.concatenate(parts, axis=0)


def _sb_scan(zt, carry, limit):
    nk, nq = zt.shape
    rowid = lax.broadcasted_iota(jnp.int32, (SUBLANE, nq), 0)
    w_parts = [None] * (nk // SUBLANE)
    for j in reversed(range(nk // SB_MINI)):
        ls, ss, vis = [], [], []
        for c in range(SB_STRIDE):
            i = j * SB_STRIDE + c
            z = zt[i * SUBLANE:(i + 1) * SUBLANE]
            nz = -z
            t = jnp.log2(1.0 + jnp.exp2(jnp.minimum(z, nz)))
            l = jnp.minimum(nz, 0.0) - t
            ss.append(l + z)
            if limit is not None:
                v = rowid * SB_STRIDE + (SB_MINI * j + c) < limit
                l = jnp.where(v, l, 0.0)
                vis.append(v)
            ls.append(l)
        later = [None] * SB_STRIDE
        run = ls[SB_STRIDE - 1]
        for c in range(SB_STRIDE - 2, -1, -1):
            later[c] = run
            run = run + ls[c]
        incl = run
        for sh in (1, 2, 4):
            incl = incl + jnp.where(rowid < SUBLANE - sh, pltpu.roll(incl, SUBLANE - sh, axis=0), 0.0)
        off = carry + (incl - run)
        for c in range(SB_STRIDE):
            after = off if later[c] is None else off + later[c]
            w = jnp.exp2(ss[c] + after)
            if limit is not None:
                w = jnp.where(vis[c], w, 0.0)
            w_parts[j * SB_STRIDE + c] = w
        carry = carry + incl[0:1, :]
    return jnp.concatenate(w_parts, axis=0), carry


SB_PAR = 4


def _sb_prompt_kernel(seq, q_ref, k_ref, v_ref, km_ref, vm_ref, sb_ref, o_ref, kms_ref, vms_ref):
    kms_ref[...] = jnp.zeros_like(kms_ref)
    vms_ref[...] = jnp.zeros_like(vms_ref)
    for i in range(SB_PAR):
        kms_ref[i, 0:N_META, :] = km_ref[:, i * SB_HD:(i + 1) * SB_HD]
        vms_ref[i, 0:N_META, :] = vm_ref[:, i * SB_HD:(i + 1) * SB_HD]
    heads = range(SB_PAR)
    diag_limit = lax.broadcasted_iota(jnp.int32, (SUBLANE, SB_TILE), 1)

    def tile(kp, vp, q, carry, acc, limit):
        zt = _dot_nt(kp, q)
        wt, carry = _sb_scan(zt, carry, limit)
        return carry, acc + _dot_tn(vp, wt.astype(BF16))

    half = SB_TILE // 2

    def diag_tile(kp, vp, q):
        zt = _dot_nt(kp, q)
        c0 = jnp.zeros((1, half), F32)
        w_lo, c_lo = _sb_scan(zt[:half, :half], c0, diag_limit[:, :half])
        w_hi, c_hi = _sb_scan(zt[:, half:], c0, diag_limit[:, half:])
        w_lo = jnp.concatenate([w_lo, jnp.zeros((half, half), F32)], axis=0)
        wt = jnp.concatenate([w_lo, w_hi], axis=1).astype(BF16)
        return jnp.concatenate([c_lo, c_hi], axis=1), _dot_tn(vp, wt)

    def diag_tiles(base, qs):
        rows = pl.ds(base, SB_TILE)
        res = [diag_tile(k_ref[rows, i * SB_HD:(i + 1) * SB_HD], v_ref[rows, i * SB_HD:(i + 1) * SB_HD], qs[i])
               for i in heads]
        return [r[0] for r in res], [r[1] for r in res]

    def plain_tiles(t, qs, carries, accs):
        rows = pl.ds(pl.multiple_of(t * SB_TILE, SB_TILE), SB_TILE)
        res = [tile(k_ref[rows, i * SB_HD:(i + 1) * SB_HD], v_ref[rows, i * SB_HD:(i + 1) * SB_HD],
                    qs[i], carries[i], accs[i], None) for i in heads]
        return [r[0] for r in res], [r[1] for r in res]

    def meta_tiles(qs, carries, accs):
        res = [tile(_load_perm(kms_ref.at[i], 0, SB_MINI).astype(BF16),
                    _load_perm(vms_ref.at[i], 0, SB_MINI).astype(BF16),
                    qs[i], carries[i], accs[i], N_META) for i in heads]
        return [r[0] for r in res], [r[1] for r in res]

    def qblock(m, is_first):
        row0 = pl.multiple_of(m * SB_TILE, SB_TILE)
        rows = pl.ds(row0, SB_TILE)
        qs = [q_ref[rows, i * SB_HD:(i + 1) * SB_HD] for i in heads]
        carries, accs = diag_tiles(row0, qs)
        if is_first:
            _, accs = meta_tiles(qs, carries, accs)
        else:
            carries, accs = plain_tiles(m - 1, qs, carries, accs)

            def cond(st):
                t, carries, _ = st
                worst = functools.reduce(jnp.maximum, [jnp.max(c) for c in carries])
                return (t >= -1) & (worst > SB_CUTOFF_LOG2)

            def body(st):
                t, carries, accs = st
                carries, accs = lax.cond(t >= 0,
                                         lambda: plain_tiles(t, qs, carries, accs),
                                         lambda: meta_tiles(qs, carries, accs))
                return t - 1, carries, accs

            _, _, accs = lax.while_loop(cond, body, (m - 2, carries, accs))
        for i in heads:
            cols = slice(i * SB_HD, (i + 1) * SB_HD)
            o = accs[i].T * sb_ref[rows, cols].astype(F32)
            o_ref[rows, cols] = o.astype(BF16)

    qblock(0, True)

    def loop_body(m, carry):
        qblock(m, False)
        return carry

    lax.fori_loop(1, seq // SB_TILE, loop_body, 0)


def _sb_prompt(p_main, km32, vm32, batch, seq):
    gw = SB_PAR * SB_HD
    col = lambda off: pl.BlockSpec((seq, gw), lambda b, g: (b, off // gw + g))
    return pl.pallas_call(
        functools.partial(_sb_prompt_kernel, seq),
        grid=(batch, H_SB // SB_PAR),
        in_specs=[col(P_SQ), col(P_SK), col(P_SV),
                  pl.BlockSpec((N_META, gw), lambda b, g: (0, g)),
                  pl.BlockSpec((N_META, gw), lambda b, g: (0, g)),
                  col(P_GB)],
        out_specs=pl.BlockSpec((seq, gw), lambda b, g: (b, g)),
        out_shape=jax.ShapeDtypeStruct((batch * seq, SB_W), BF16),
        scratch_shapes=[pltpu.VMEM((SB_PAR, SB_MINI, SB_HD), F32), pltpu.VMEM((SB_PAR, SB_MINI, SB_HD), F32)],
        compiler_params=pltpu.CompilerParams(
            dimension_semantics=("arbitrary", "arbitrary"), vmem_limit_bytes=VMEM_LIMIT),
        name="sb_prompt",
    )(p_main, p_main, p_main, km32, vm32, p_main)


SB_GROUP = 4


def _load_perm_native(ref, key0, nk, head):
    parts = []
    for j in range(nk // SB_MINI):
        for c in range(SB_STRIDE):
            start = (key0 + SB_MINI * j + c) * H_SB + head
            parts.append(ref[pl.ds(start, SUBLANE, stride=SB_STRIDE * H_SB), :])
    return jnp.concatenate(parts, axis=0)


def _sb_sample_kernel(t_new, past, q_ref, kn_ref, vn_ref, ck_ref, cv_ref, sb_ref, o_ref,
                      kbuf, vbuf, kfar, vfar, sems, far_sems):
    b = pl.program_id(0)
    nb = pl.num_programs(0)
    gw = SB_GROUP * SB_HD
    nq = SB_GROUP * t_new
    ngroups = H_SB // SB_GROUP
    n_tiles = past // SB_TILE
    tile_rows = SB_TILE * H_SB

    def near_copies(bb, slot):
        rows = pl.ds((n_tiles - 1) * tile_rows, tile_rows)
        return (pltpu.make_async_copy(ck_ref.at[bb, rows, :], kbuf.at[slot], sems.at[slot, 0]),
                pltpu.make_async_copy(cv_ref.at[bb, rows, :], vbuf.at[slot], sems.at[slot, 1]))

    @pl.when(b == 0)
    def _():
        for cp in near_copies(0, 0):
            cp.start()

    slot = b % 2

    @pl.when(b + 1 < nb)
    def _():
        for cp in near_copies(b + 1, 1 - slot):
            cp.start()

    for cp in near_copies(b, slot):
        cp.wait()

    rh = lax.broadcasted_iota(jnp.int32, (gw, nq), 0) // SB_HD
    ch = lax.broadcasted_iota(jnp.int32, (gw, nq), 1) // t_new
    head_match = rh == ch
    new_limit = lax.broadcasted_iota(jnp.int32, (SUBLANE, nq), 1) % t_new

    def tile(kref, vref, nk, g, qbd, carry, acc, limit):
        hs = range(SB_GROUP * g, SB_GROUP * (g + 1))
        kp = jnp.concatenate([_load_perm_native(kref, 0, nk, h) for h in hs], axis=1).astype(BF16)
        vp = jnp.concatenate([_load_perm_native(vref, 0, nk, h) for h in hs], axis=1).astype(BF16)
        zt = _dot(kp, qbd)
        wt, carry = _sb_scan(zt, carry, limit)
        return carry, acc + _dot_tn(vp, wt.astype(BF16))

    qbds, carries, accs = [], [], []
    for g in range(ngroups):
        q = q_ref[:, g * gw:(g + 1) * gw].astype(F32)
        qt = jnp.concatenate([q] * SB_GROUP, axis=0).T
        qbd = jnp.where(head_match, qt, 0.0).astype(BF16)
        carry = jnp.zeros((1, nq), F32)
        acc = jnp.zeros((gw, nq), F32)
        carry, acc = tile(kn_ref, vn_ref, t_new, g, qbd, carry, acc, new_limit)
        carry, acc = tile(kbuf.at[slot], vbuf.at[slot], SB_TILE, g, qbd, carry, acc, None)
        qbds.append(qbd)
        carries.append(carry)
        accs.append(acc)

    def cond(st):
        t, carries, _ = st
        worst = functools.reduce(jnp.maximum, [jnp.max(c) for c in carries])
        return (t >= 0) & (worst > SB_CUTOFF_LOG2)

    def body(st):
        t, carries, accs = st
        rows = pl.ds(pl.multiple_of(t * tile_rows, tile_rows), tile_rows)
        far = (pltpu.make_async_copy(ck_ref.at[b, rows, :], kfar, far_sems.at[0]),
               pltpu.make_async_copy(cv_ref.at[b, rows, :], vfar, far_sems.at[1]))
        for cp in far:
            cp.start()
        for cp in far:
            cp.wait()
        res = [tile(kfar, vfar, SB_TILE, g, qbds[g], carries[g], accs[g], None) for g in range(ngroups)]
        return t - 1, [r[0] for r in res], [r[1] for r in res]

    _, _, accs = lax.while_loop(cond, body, (n_tiles - 2, carries, accs))
    for g in range(ngroups):
        at = accs[g].T
        for h in range(SB_GROUP):
            cols = slice(g * gw + h * SB_HD, g * gw + (h + 1) * SB_HD)
            o = at[h * t_new:(h + 1) * t_new, h * SB_HD:(h + 1) * SB_HD] * sb_ref[:, cols].astype(F32)
            o_ref[:, cols] = o.astype(BF16)


def _sb_sample(p_main, k5s, v5s, cache_k, cache_v, row0, dec_batch, t_new, past):
    rb0 = row0 // t_new
    tile_rows = SB_TILE * H_SB
    any_spec = pl.BlockSpec(memory_space=pl.ANY)
    new_spec = pl.BlockSpec((t_new * H_SB, SB_HD), lambda b: (b, 0))
    return pl.pallas_call(
        functools.partial(_sb_sample_kernel, t_new, past),
        grid=(dec_batch,),
        in_specs=[pl.BlockSpec((t_new, SB_W), lambda b: (rb0 + b, P_SQ // SB_W)),
                  new_spec, new_spec, any_spec, any_spec,
                  pl.BlockSpec((t_new, SB_W), lambda b: (rb0 + b, P_GB // SB_W))],
        out_specs=pl.BlockSpec((t_new, SB_W), lambda b: (b, 0)),
        out_shape=jax.ShapeDtypeStruct((dec_batch * t_new, SB_W), BF16),
        scratch_shapes=[
            pltpu.VMEM((2, tile_rows, SB_HD), F32),
            pltpu.VMEM((2, tile_rows, SB_HD), F32),
            pltpu.VMEM((tile_rows, SB_HD), F32),
            pltpu.VMEM((tile_rows, SB_HD), F32),
            pltpu.SemaphoreType.DMA((2, 2)),
            pltpu.SemaphoreType.DMA((2,)),
        ],
        compiler_params=pltpu.CompilerParams(
            dimension_semantics=("arbitrary",), vmem_limit_bytes=VMEM_LIMIT),
        name="sb_sample",
    )(p_main, k5s, v5s, cache_k, cache_v, p_main)


def _merge_kernel(n_prompt_tiles, ogp_ref, ogs_ref, obp_ref, obs_ref, xp_ref, xs_ref, w_ref, o_ref):
    m = pl.program_id(0)

    def run(og_ref, ob_ref, x_ref):
        mix = (og_ref[...].astype(F32) + ob_ref[...].astype(F32)).astype(BF16)
        o_ref[...] = x_ref[...] + _dot(mix, w_ref[...])

    @pl.when(m < n_prompt_tiles)
    def _():
        run(ogp_ref, obp_ref, xp_ref)

    @pl.when(m >= n_prompt_tiles)
    def _():
        run(ogs_ref, obs_ref, xs_ref)


def _merge(og_p, og_s, ob_p, ob_s, xp, xs, w_out):
    npt, nst = xp.shape[0] // TM, xs.shape[0] // TM
    pidx = lambda m: (jnp.minimum(m, npt - 1), 0)
    sidx = lambda m: (jnp.maximum(m - npt, 0), 0)
    row = lambda idx: pl.BlockSpec((TM, D_MODEL), idx)
    return pl.pallas_call(
        functools.partial(_merge_kernel, npt),
        grid=(npt + nst,),
        in_specs=[row(pidx), row(sidx), row(pidx), row(sidx), row(pidx), row(sidx),
                  pl.BlockSpec((D_MODEL, D_MODEL), lambda m: (0, 0))],
        out_specs=pl.BlockSpec((TM, D_MODEL), lambda m: (m, 0)),
        out_shape=jax.ShapeDtypeStruct((xp.shape[0] + xs.shape[0], D_MODEL), F32),
        compiler_params=pltpu.CompilerParams(
            dimension_semantics=("arbitrary",), vmem_limit_bytes=VMEM_LIMIT),
        name="merge_outproj",
    )(og_p, og_s, ob_p, ob_s, xp, xs, w_out)


def _ffn_kernel(n_prompt_tiles, x_ref, g2_ref, wu_ref, wd_ref, gf_ref, yp_ref, ys_ref, h_ref, acc_ref):
    m = pl.program_id(0)
    f = pl.program_id(1)

    @pl.when(f == 0)
    def _():
        h_ref[...] = _rms(x_ref[...], g2_ref[...]).astype(BF16)
        acc_ref[...] = jnp.zeros_like(acc_ref)

    u = jnp.maximum(_dot(h_ref[...], wu_ref[...]), 0.0)
    acc_ref[...] += _dot((u * u).astype(BF16), wd_ref[...])

    @pl.when(f == pl.num_programs(1) - 1)
    def _():
        y = _rms(x_ref[...] + acc_ref[...], gf_ref[...])

        @pl.when(m < n_prompt_tiles)
        def _():
            yp_ref[...] = y

        @pl.when(m >= n_prompt_tiles)
        def _():
            ys_ref[...] = y


def _ffn(x1, g2, w_up, w_down, gf, npr, nsr):
    npt, nst = npr // TM, nsr // TM
    return pl.pallas_call(
        functools.partial(_ffn_kernel, npt),
        grid=(npt + nst, D_FF // TF),
        in_specs=[
            pl.BlockSpec((TM, D_MODEL), lambda m, f: (m, 0)),
            pl.BlockSpec((1, D_MODEL), lambda m, f: (0, 0)),
            pl.BlockSpec((D_MODEL, TF), lambda m, f: (0, f)),
            pl.BlockSpec((TF, D_MODEL), lambda m, f: (f, 0)),
            pl.BlockSpec((1, D_MODEL), lambda m, f: (0, 0)),
        ],
        out_specs=[
            pl.BlockSpec((TM, D_MODEL), lambda m, f: (jnp.minimum(m, npt - 1), 0)),
            pl.BlockSpec((TM, D_MODEL), lambda m, f: (jnp.maximum(m - npt, 0), 0)),
        ],
        out_shape=[
            jax.ShapeDtypeStruct((npr, D_MODEL), F32),
            jax.ShapeDtypeStruct((nsr, D_MODEL), F32),
        ],
        scratch_shapes=[pltpu.VMEM((TM, D_MODEL), BF16), pltpu.VMEM((TM, D_MODEL), F32)],
        compiler_params=pltpu.CompilerParams(
            dimension_semantics=("arbitrary", "arbitrary"), vmem_limit_bytes=VMEM_LIMIT),
        name="ffn_final",
    )(x1, g2, w_up, w_down, gf)


def kernel(x_prompt, x_sample, cache_sb_k, cache_sb_v, state_gla, meta_tokens, norm1_g, w_in,
           w_alpha_up, b_alpha, gla_norm_g, w_out, norm2_g, w_up, w_down, norm_f_g):
    batch, seq, _ = x_prompt.shape
    dec_batch, t_new, _ = x_sample.shape
    depth, _, past = cache_sb_k.shape[:3]
    assert depth == 1 and w_in.shape[2] == _R_END
    assert seq % TM == 0 and (dec_batch * t_new) % TM == 0 and past % SB_TILE == 0
    assert t_new == SB_MINI and t_new % 16 == 0
    npr, nsr = batch * seq, dec_batch * t_new

    w_main, w_alow = _prep_w_in(w_in[0].T)
    wup = jnp.pad(w_alpha_up[0], ((0, LANE - GLA_RANK), (0, 0))).astype(BF16)
    bias = b_alpha[0].reshape(1, GLA_QK)
    gn = gla_norm_g[0].reshape(1, GLA_V)
    g1 = norm1_g[0].reshape(1, D_MODEL)
    g2 = norm2_g[0].reshape(1, D_MODEL)
    gf = norm_f_g.reshape(1, D_MODEL)

    xp = x_prompt.reshape(npr, D_MODEL)
    xs = x_sample.reshape(nsr, D_MODEL)
    meta = meta_tokens.astype(x_prompt.dtype)

    (p_main, p_meta, km32, vm32, alow, alow_meta, k5p, v5p, k5s, v5s,
     w_out_b, w_up_b, w_down_b) = _inproj(xp, xs, meta, g1, w_main, w_alow,
                                          w_out[0], w_up[0], w_down[0], batch, seq)

    og_p, st_p = _gla_prompt(p_main, p_meta, alow, alow_meta, wup, bias, gn, batch, seq)
    og_s, st_s = _gla_sample(p_main, alow, state_gla[0], wup, bias, gn, npr, dec_batch, t_new)
    ob_p = _sb_prompt(p_main, km32, vm32, batch, seq)
    ck = cache_sb_k[0].reshape(dec_batch, past * H_SB, SB_HD)
    cv = cache_sb_v[0].reshape(dec_batch, past * H_SB, SB_HD)
    ob_s = _sb_sample(p_main, k5s.reshape(nsr * H_SB, SB_HD), v5s.reshape(nsr * H_SB, SB_HD),
                      ck, cv, npr, dec_batch, t_new, past)

    x1 = _merge(og_p, og_s, ob_p, ob_s, xp, xs, w_out_b)
    y_p, y_s = _ffn(x1, g2, w_up_b, w_down_b, gf, npr, nsr)

    return (
        y_p.reshape(batch, seq, D_MODEL),
        y_s.reshape(dec_batch, t_new, D_MODEL),
        st_p[None],
        k5p.reshape(1, batch, N_META + seq, H_SB, SB_HD),
        v5p.reshape(1, batch, N_META + seq, H_SB, SB_HD),
        st_s[None],
        k5s.reshape(1, dec_batch, t_new, H_SB, SB_HD),
        v5s.reshape(1, dec_batch, t_new, H_SB, SB_HD),
    )
```

```python
import functools

import jax
import jax.numpy as jnp
from jax import lax
from jax.experimental import pallas as pl
from jax.experimental.pallas import tpu as pltpu

F32 = jnp.float32
BF16 = jnp.bfloat16

D_MODEL = 2048
N_META = 16
H_GLA = 4
GLA_DK = 256
GLA_DV = 512
GLA_RANK = 16
GLA_TAU = 16.0
GLA_CHUNK = 64
H_SB = 16
SB_HD = 128
D_FF = 4 * D_MODEL
EPS = 1e-5
GLA_QK = H_GLA * GLA_DK
GLA_V = H_GLA * GLA_DV
SB_W = H_SB * SB_HD

_R_GK = GLA_QK
_R_GV = 2 * GLA_QK
_R_ALOW = _R_GV + GLA_V
_R_SQ = _R_ALOW + GLA_RANK
_R_SK = _R_SQ + SB_W
_R_SV = _R_SK + SB_W
_R_GA = _R_SV + SB_W
_R_GB = _R_GA + GLA_V
_R_END = _R_GB + SB_W

P_GQ = 0
P_GK = P_GQ + GLA_QK
P_GV = P_GK + GLA_QK
P_SQ = P_GV + GLA_V
P_SK = P_SQ + SB_W
P_SV = P_SK + SB_W
P_GA = P_SV + SB_W
P_GB = P_GA + GLA_V
P_W = P_GB + SB_W

LANE = 128
SUBLANE = 8
TM = 512
TN = 2048
TF = 1024
GLA_ROWS = 256
SB_TILE = 256
SB_MINI = 32
SB_STRIDE = SB_MINI // SUBLANE
SB_CUTOFF = -104.0
LOG2_E = 1.4426950408889634
SB_SCALE_LOG2 = SB_HD ** -0.5 * LOG2_E
SB_CUTOFF_LOG2 = SB_CUTOFF * LOG2_E
VMEM_LIMIT = 56 * 1024 * 1024


def _dot(a, b):
    return jnp.dot(a, b, preferred_element_type=F32)


def _dot_nt(a, b):
    return lax.dot_general(a, b, (((1,), (1,)), ((), ())), preferred_element_type=F32)


def _dot_tn(a, b):
    return lax.dot_general(a, b, (((0,), (0,)), ((), ())), preferred_element_type=F32)


def _split2(x):
    hi = x.astype(BF16)
    lo = (x - hi.astype(F32)).astype(BF16)
    return hi, lo


def _rms(x, g):
    ms = jnp.mean(x * x, axis=-1, keepdims=True)
    return x * lax.rsqrt(ms + EPS) * g


def _log_sigmoid(x):
    return jnp.minimum(x, 0.0) - jnp.log(1.0 + jnp.exp(-jnp.abs(x)))


_PREP_ROWS = 1024
_PREP_COLS = 1024


def _prep_w_kernel(n_aligned, wt_ref, edge_ref, o_ref, al_ref):
    n = pl.program_id(1)

    @pl.when(n < n_aligned)
    def _():
        o_ref[...] = wt_ref[...].T.astype(BF16)

    def shifted():
        return jnp.concatenate([wt_ref[GLA_RANK:, :], edge_ref[...]], axis=0).T

    is_sq = (n >= P_SQ // _PREP_COLS) & (n < P_SK // _PREP_COLS)

    @pl.when((n >= n_aligned) & jnp.logical_not(is_sq))
    def _():
        o_ref[...] = shifted().astype(BF16)

    @pl.when(is_sq)
    def _():
        o_ref[...] = (shifted() * SB_SCALE_LOG2).astype(BF16)

    @pl.when(n == n_aligned)
    def _():
        pad = jnp.zeros((LANE - GLA_RANK, wt_ref.shape[1]), F32)
        al_ref[...] = jnp.concatenate([wt_ref[0:GLA_RANK, :], pad], axis=0).T.astype(BF16)


def _prep_w_in(wt):
    n_aligned = _R_ALOW // _PREP_COLS
    edge_blocks = _PREP_COLS // GLA_RANK
    return pl.pallas_call(
        functools.partial(_prep_w_kernel, n_aligned),
        grid=(D_MODEL // _PREP_ROWS, P_W // _PREP_COLS),
        in_specs=[
            pl.BlockSpec((_PREP_COLS, _PREP_ROWS), lambda r, n: (n, r)),
            pl.BlockSpec((GLA_RANK, _PREP_ROWS), lambda r, n: ((jnp.maximum(n, n_aligned) + 1) * edge_blocks, r)),
        ],
        out_specs=[pl.BlockSpec((_PREP_ROWS, _PREP_COLS), lambda r, n: (r, n)),
                   pl.BlockSpec((_PREP_ROWS, LANE), lambda r, n: (r, 0))],
        out_shape=[jax.ShapeDtypeStruct((D_MODEL, P_W), BF16),
                   jax.ShapeDtypeStruct((D_MODEL, LANE), BF16)],
        compiler_params=pltpu.CompilerParams(
            dimension_semantics=("arbitrary", "arbitrary"), vmem_limit_bytes=VMEM_LIMIT),
        name="prep_w_in",
    )(wt, wt)


_N_SK0 = P_SK // TN
_N_SV0 = P_SV // TN
_N_GATE0 = P_GA // TN


assert TN == SB_W


def _permute_rows(stage_ref):
    heads, rows, _ = stage_ref.shape
    cols = []
    for h in range(heads):
        parts = [stage_ref[h, pl.ds(SB_MINI * j + c, SUBLANE, stride=SB_STRIDE), :]
                 for j in range(rows // SB_MINI) for c in range(SB_STRIDE)]
        cols.append(jnp.concatenate(parts, axis=0).astype(BF16))
    return jnp.concatenate(cols, axis=1)


MAX_CAST_SLABS = 64


def _inproj_kernel(n_prompt_tiles, batch, seq, xp_ref, xs_ref, meta_ref, g1_ref, w_ref, wal_ref,
                   p_ref, pm_ref, km_ref, vm_ref, al_ref, alm_ref,
                   kp_ref, vp_ref, ks_ref, vs_ref,
                   h_ref, kstage_ref, vstage_ref, mstage_ref, sems, msem):
    m = pl.program_id(0)
    n = pl.program_id(1)
    n_tiles = pl.num_programs(0)
    tiles_per_batch = seq // TM

    def stage_heads(ref, vals):
        for h in range(H_SB):
            ref[h] = vals[:, h * SB_HD:(h + 1) * SB_HD]

    def native_copies(src_ref, rows, dst_ref, token, dma_sem):
        return [pltpu.make_async_copy(src_ref.at[h], dst_ref.at[pl.ds(token, rows), h, :], dma_sem)
                for h in range(H_SB)]

    def wait_native(stage_ref, sem):
        for cp in native_copies(stage_ref, TM, kp_ref, 0, sem):
            cp.wait()

    def write_native(stage_ref, sem, dst_p, dst_s):
        @pl.when(m < n_prompt_tiles)
        def _():
            b = m // tiles_per_batch
            token = b * (seq + N_META) + N_META + (m - b * tiles_per_batch) * TM
            for cp in native_copies(stage_ref, TM, dst_p, token, sem):
                cp.start()

        @pl.when(m >= n_prompt_tiles)
        def _():
            for cp in native_copies(stage_ref, TM, dst_s, (m - n_prompt_tiles) * TM, sem):
                cp.start()

    def write_native_meta(met, dst_p):
        stage_heads(mstage_ref, met)
        copies = [cp for b in range(batch)
                  for cp in native_copies(mstage_ref, N_META, dst_p, b * (seq + N_META), msem)]
        for cp in copies:
            cp.start()
        for cp in copies:
            cp.wait()

    first = m == 0

    @pl.when(n == 0)
    def _():
        g = g1_ref[...]

        @pl.when(m < n_prompt_tiles)
        def _():
            h_ref[0:TM, :] = _rms(xp_ref[...], g).astype(BF16)

        @pl.when(m >= n_prompt_tiles)
        def _():
            h_ref[0:TM, :] = _rms(xs_ref[...], g).astype(BF16)

        al_ref[...] = _dot(h_ref[0:TM, :], wal_ref[...])

        @pl.when(first)
        def _():
            h_ref[TM:TM + N_META, :] = _rms(meta_ref[...], g).astype(BF16)
            alm_ref[...] = _dot(h_ref[TM:TM + N_META, :], wal_ref[...])

    is_gate = n >= _N_GATE0
    is_k = (n >= _N_SK0) & (n < _N_SV0)
    is_v = (n >= _N_SV0) & (n < _N_GATE0)

    def column_tile(kind, epilogue, meta_epilogue):
        @pl.when(kind & first)
        def _():
            acc = _dot(h_ref[...], w_ref[...])
            epilogue(acc[:TM])
            pm_ref[...] = acc[TM:].astype(BF16)
            meta_epilogue(acc[TM:])

        @pl.when(kind & jnp.logical_not(first))
        def _():
            epilogue(_dot(h_ref[0:TM, :], w_ref[...]))

    def plain(main):
        p_ref[...] = main.astype(BF16)

    def gate(main):
        p_ref[...] = jax.nn.sigmoid(main).astype(BF16)

    def kv(stage_ref, sem, dst_p, dst_s):
        def epilogue(main):
            stage_heads(stage_ref, main)
            write_native(stage_ref, sem, dst_p, dst_s)
            p_ref[...] = _permute_rows(stage_ref)
        return epilogue

    def kv_meta(f32_ref, dst_p):
        def epilogue(met):
            f32_ref[...] = met
            write_native_meta(met, dst_p)
        return epilogue

    @pl.when(is_k & (m > 0))
    def _():
        wait_native(kstage_ref, sems.at[0])

    @pl.when(is_v & (m > 0))
    def _():
        wait_native(vstage_ref, sems.at[1])

    column_tile(jnp.logical_not(is_gate | is_k | is_v), plain, lambda met: None)
    column_tile(is_gate, gate, lambda met: None)
    column_tile(is_k, kv(kstage_ref, sems.at[0], kp_ref, ks_ref), kv_meta(km_ref, kp_ref))
    column_tile(is_v, kv(vstage_ref, sems.at[1], vp_ref, vs_ref), kv_meta(vm_ref, vp_ref))

    @pl.when((m == n_tiles - 1) & (n == pl.num_programs(1) - 1))
    def _():
        wait_native(kstage_ref, sems.at[0])
        wait_native(vstage_ref, sems.at[1])


def _inproj(xp, xs, meta, g1, w_main, w_alow, batch, seq):
    npr, nsr = xp.shape[0], xs.shape[0]
    npt, nst = npr // TM, nsr // TM
    nt = npt + nst
    rows = npr + nsr
    nn = P_W // TN
    kcol = lambda n: jnp.clip(n - _N_SK0, 0, SB_W // TN - 1)
    vcol = lambda n: jnp.clip(n - _N_SV0, 0, SB_W // TN - 1)
    once = lambda m, col, last: jnp.where(m == 0, col, last)
    any_spec = pl.BlockSpec(memory_space=pl.ANY)
    return pl.pallas_call(
        functools.partial(_inproj_kernel, npt, batch, seq),
        grid=(nt, nn),
        in_specs=[
            pl.BlockSpec((TM, D_MODEL), lambda m, n: (jnp.minimum(m, npt - 1), 0)),
            pl.BlockSpec((TM, D_MODEL), lambda m, n: (jnp.maximum(m - npt, 0), 0),
                         pipeline_mode=pl.Buffered(1)),
            pl.BlockSpec((N_META, D_MODEL), lambda m, n: (0, 0)),
            pl.BlockSpec((1, D_MODEL), lambda m, n: (0, 0)),
            pl.BlockSpec((D_MODEL, TN), lambda m, n: (0, n)),
            pl.BlockSpec((D_MODEL, LANE), lambda m, n: (0, 0)),
        ],
        out_specs=[
            pl.BlockSpec((TM, TN), lambda m, n: (m, n)),
            pl.BlockSpec((N_META, TN), lambda m, n: (0, once(m, n, nn - 1))),
            pl.BlockSpec((N_META, TN), lambda m, n: (0, once(m, kcol(n), SB_W // TN - 1))),
            pl.BlockSpec((N_META, TN), lambda m, n: (0, once(m, vcol(n), SB_W // TN - 1))),
            pl.BlockSpec((TM, LANE), lambda m, n: (m, 0)),
            pl.BlockSpec((N_META, LANE), lambda m, n: (0, 0)),
            any_spec, any_spec, any_spec, any_spec,
        ],
        out_shape=[
            jax.ShapeDtypeStruct((rows, P_W), BF16),
            jax.ShapeDtypeStruct((N_META, P_W), BF16),
            jax.ShapeDtypeStruct((N_META, SB_W), F32),
            jax.ShapeDtypeStruct((N_META, SB_W), F32),
            jax.ShapeDtypeStruct((rows, LANE), F32),
            jax.ShapeDtypeStruct((N_META, LANE), F32),
            jax.ShapeDtypeStruct((batch * (seq + N_META), H_SB, SB_HD), F32),
            jax.ShapeDtypeStruct((batch * (seq + N_META), H_SB, SB_HD), F32),
            jax.ShapeDtypeStruct((nsr, H_SB, SB_HD), F32),
            jax.ShapeDtypeStruct((nsr, H_SB, SB_HD), F32),
        ],
        scratch_shapes=[
            pltpu.VMEM((TM + N_META, D_MODEL), BF16),
            pltpu.VMEM((H_SB, TM, SB_HD), F32),
            pltpu.VMEM((H_SB, TM, SB_HD), F32),
            pltpu.VMEM((H_SB, N_META, SB_HD), F32),
            pltpu.SemaphoreType.DMA((2,)),
            pltpu.SemaphoreType.DMA(()),
        ],
        compiler_params=pltpu.CompilerParams(
            dimension_semantics=("arbitrary", "arbitrary"), vmem_limit_bytes=VMEM_LIMIT),
        name="inproj",
    )(xp, xs, meta, g1, w_main, w_alow)


def _gla_chunk(c, q, k, v, al, wup, bias, st_ref):
    x = _dot(al.astype(BF16), wup) + bias
    g = _log_sigmoid(x) * (1.0 / GLA_TAU)
    g_hi, g_lo = _split2(g)
    row = lax.broadcasted_iota(jnp.int32, (c, c), 0)
    col = lax.broadcasted_iota(jnp.int32, (c, c), 1)
    causal = col <= row
    tri = causal.astype(BF16)
    cum = _dot(tri, g_hi) + _dot(tri, g_lo)
    outs = []
    for h in range(H_GLA):
        ch = cum[:, h * GLA_DK:(h + 1) * GLA_DK]
        last = ch[c - 1:c, :]
        qh = q[:, h * GLA_DK:(h + 1) * GLA_DK].astype(F32)
        kh = k[:, h * GLA_DK:(h + 1) * GLA_DK].astype(F32)
        vh = v[:, h * GLA_DV:(h + 1) * GLA_DV]
        qd = (qh * jnp.exp(ch) * (GLA_DK ** -0.5)).astype(BF16)
        kd = (kh * jnp.exp(-ch)).astype(BF16)
        kr = (kh * jnp.exp(last - ch)).astype(BF16)
        att = jnp.where(causal, _dot_nt(qd, kd), 0.0).astype(BF16)
        st = st_ref[h]
        outs.append(_dot(att, vh) + _dot_nt(qd, st.astype(BF16)))
        st_ref[h] = st * jnp.exp(last) + _dot_tn(vh, kr)
    return outs


def _gla_finish(o, gn, sa):
    ms = jnp.mean(o * o, axis=-1, keepdims=True)
    return (o * lax.rsqrt(ms + EPS) * gn * sa.astype(F32)).astype(BF16)


def _gla_prompt_kernel(n_cast_slabs, q_ref, k_ref, v_ref, sa_ref, al_ref, km_ref, vm_ref, alm_ref,
                       wup_ref, b_ref, gn_ref, wo_ref, wu_ref, wd_ref,
                       og_ref, so_ref, wob_ref, wub_ref, wdb_ref, st_ref):
    c_idx = pl.program_id(1)
    wup = wup_ref[...]
    bias = b_ref[...]

    @pl.when(pl.program_id(0) * pl.num_programs(1) + c_idx < n_cast_slabs)
    def _():
        wob_ref[...] = wo_ref[...].astype(BF16)
        wub_ref[...] = wu_ref[...].astype(BF16)
        wdb_ref[...] = wd_ref[...].astype(BF16)

    @pl.when(c_idx == 0)
    def _():
        st_ref[...] = jnp.zeros_like(st_ref)
        zq = jnp.zeros((N_META, GLA_QK), BF16)
        _gla_chunk(N_META, zq, km_ref[...], vm_ref[...], alm_ref[...], wup, bias, st_ref)

    def body(i, carry):
        r0 = pl.multiple_of(i * GLA_CHUNK, GLA_CHUNK)
        rows = pl.ds(r0, GLA_CHUNK)
        outs = _gla_chunk(GLA_CHUNK, q_ref[rows, :], k_ref[rows, :], v_ref[rows, :], al_ref[rows, :],
                          wup, bias, st_ref)
        for h in range(H_GLA):
            cols = slice(h * GLA_DV, (h + 1) * GLA_DV)
            og_ref[rows, cols] = _gla_finish(outs[h], gn_ref[:, cols], sa_ref[rows, cols])
        return carry

    lax.fori_loop(0, GLA_ROWS // GLA_CHUNK, body, 0, unroll=True)

    @pl.when(c_idx == pl.num_programs(1) - 1)
    def _():
        for h in range(H_GLA):
            so_ref[0, h] = st_ref[h].T


def _gla_prompt(p_main, p_meta, alow, alow_meta, wup, bias, gn, w_out, w_up, w_down, batch, seq):
    nc = seq // GLA_ROWS
    rb = lambda b, c: b * nc + c
    const = lambda b, c: (0, 0)
    n_slabs = min(MAX_CAST_SLABS, 1 << ((batch * nc).bit_length() - 1))
    slab = lambda b, c: (jnp.minimum(rb(b, c), n_slabs - 1), 0)
    casts = (w_out, w_up, w_down)
    cast_specs = [pl.BlockSpec((w.shape[0] // n_slabs, w.shape[1]), slab) for w in casts]
    return pl.pallas_call(
        functools.partial(_gla_prompt_kernel, n_slabs),
        grid=(batch, nc),
        in_specs=[
            pl.BlockSpec((GLA_ROWS, GLA_QK), lambda b, c: (rb(b, c), P_GQ // GLA_QK)),
            pl.BlockSpec((GLA_ROWS, GLA_QK), lambda b, c: (rb(b, c), P_GK // GLA_QK)),
            pl.BlockSpec((GLA_ROWS, GLA_V), lambda b, c: (rb(b, c), P_GV // GLA_V)),
            pl.BlockSpec((GLA_ROWS, GLA_V), lambda b, c: (rb(b, c), P_GA // GLA_V)),
            pl.BlockSpec((GLA_ROWS, LANE), lambda b, c: (rb(b, c), 0)),
            pl.BlockSpec((N_META, GLA_QK), lambda b, c: (0, P_GK // GLA_QK)),
            pl.BlockSpec((N_META, GLA_V), lambda b, c: (0, P_GV // GLA_V)),
            pl.BlockSpec((N_META, LANE), const),
            pl.BlockSpec((LANE, GLA_QK), const),
            pl.BlockSpec((1, GLA_QK), const),
            pl.BlockSpec((1, GLA_V), const),
        ] + cast_specs,
        out_specs=[
            pl.BlockSpec((GLA_ROWS, GLA_V), lambda b, c: (rb(b, c), 0)),
            pl.BlockSpec((1, H_GLA, GLA_DK, GLA_DV), lambda b, c: (b, 0, 0, 0)),
        ] + cast_specs,
        out_shape=[
            jax.ShapeDtypeStruct((batch * seq, GLA_V), BF16),
            jax.ShapeDtypeStruct((batch, H_GLA, GLA_DK, GLA_DV), F32),
        ] + [jax.ShapeDtypeStruct(w.shape, BF16) for w in casts],
        scratch_shapes=[pltpu.VMEM((H_GLA, GLA_DV, GLA_DK), F32)],
        compiler_params=pltpu.CompilerParams(
            dimension_semantics=("arbitrary", "arbitrary"), vmem_limit_bytes=VMEM_LIMIT),
        name="gla_prompt",
    )(p_main, p_main, p_main, p_main, alow, p_meta, p_meta, alow_meta, wup, bias, gn, *casts)


def _gla_sample_kernel(t_new, q_ref, k_ref, v_ref, sa_ref, al_ref, s0_ref,
                       wup_ref, b_ref, gn_ref, og_ref, so_ref, st_ref):
    for h in range(H_GLA):
        st_ref[h] = s0_ref[0, h].T
    outs = _gla_chunk(t_new, q_ref[...], k_ref[...], v_ref[...], al_ref[...],
                      wup_ref[...], b_ref[...], st_ref)
    for h in range(H_GLA):
        cols = slice(h * GLA_DV, (h + 1) * GLA_DV)
        og_ref[:, cols] = _gla_finish(outs[h], gn_ref[:, cols], sa_ref[:, cols])
        so_ref[0, h] = st_ref[h].T


def _gla_sample(p_main, alow, state, wup, bias, gn, row0, dec_batch, t_new):
    rb0 = row0 // t_new
    const = lambda b: (0, 0)
    return pl.pallas_call(
        functools.partial(_gla_sample_kernel, t_new),
        grid=(dec_batch,),
        in_specs=[
            pl.BlockSpec((t_new, GLA_QK), lambda b: (rb0 + b, P_GQ // GLA_QK)),
            pl.BlockSpec((t_new, GLA_QK), lambda b: (rb0 + b, P_GK // GLA_QK)),
            pl.BlockSpec((t_new, GLA_V), lambda b: (rb0 + b, P_GV // GLA_V)),
            pl.BlockSpec((t_new, GLA_V), lambda b: (rb0 + b, P_GA // GLA_V)),
            pl.BlockSpec((t_new, LANE), lambda b: (rb0 + b, 0)),
            pl.BlockSpec((1, H_GLA, GLA_DK, GLA_DV), lambda b: (b, 0, 0, 0)),
            pl.BlockSpec((LANE, GLA_QK), const),
            pl.BlockSpec((1, GLA_QK), const),
            pl.BlockSpec((1, GLA_V), const),
        ],
        out_specs=[
            pl.BlockSpec((t_new, GLA_V), lambda b: (b, 0)),
            pl.BlockSpec((1, H_GLA, GLA_DK, GLA_DV), lambda b: (b, 0, 0, 0)),
        ],
        out_shape=[
            jax.ShapeDtypeStruct((dec_batch * t_new, GLA_V), BF16),
            jax.ShapeDtypeStruct((dec_batch, H_GLA, GLA_DK, GLA_DV), F32),
        ],
        scratch_shapes=[pltpu.VMEM((H_GLA, GLA_DV, GLA_DK), F32)],
        compiler_params=pltpu.CompilerParams(
            dimension_semantics=("arbitrary",), vmem_limit_bytes=VMEM_LIMIT),
        name="gla_sample",
    )(p_main, p_main, p_main, p_main, alow, state, wup, bias, gn)


def _load_perm(ref, base, nk):
    parts = []
    for j in range(nk // SB_MINI):
        for c in range(SB_STRIDE):
            parts.append(ref[pl.ds(base + SB_MINI * j + c, SUBLANE, stride=SB_STRIDE), :])
    return jnp.concatenate(parts, axis=0)


def _sb_scan(zt, carry, limit):
    nk, nq = zt.shape
    rowid = lax.broadcasted_iota(jnp.int32, (SUBLANE, nq), 0)
    w_parts = [None] * (nk // SUBLANE)
    for j in reversed(range(nk // SB_MINI)):
        ls, ss, vis = [], [], []
        for c in range(SB_STRIDE):
            i = j * SB_STRIDE + c
            z = zt[i * SUBLANE:(i + 1) * SUBLANE]
            nz = -z
            t = jnp.log2(1.0 + jnp.exp2(jnp.minimum(z, nz)))
            l = jnp.minimum(nz, 0.0) - t
            ss.append(l + z)
            if limit is not None:
                v = rowid * SB_STRIDE + (SB_MINI * j + c) < limit
                l = jnp.where(v, l, 0.0)
                vis.append(v)
            ls.append(l)
        later = [None] * SB_STRIDE
        run = ls[SB_STRIDE - 1]
        for c in range(SB_STRIDE - 2, -1, -1):
            later[c] = run
            run = run + ls[c]
        incl = run
        for sh in (1, 2, 4):
            incl = incl + jnp.where(rowid < SUBLANE - sh, pltpu.roll(incl, SUBLANE - sh, axis=0), 0.0)
        off = carry + (incl - run)
        for c in range(SB_STRIDE):
            after = off if later[c] is None else off + later[c]
            w = jnp.exp2(ss[c] + after)
            if limit is not None:
                w = jnp.where(vis[c], w, 0.0)
            w_parts[j * SB_STRIDE + c] = w
        carry = carry + incl[0:1, :]
    return jnp.concatenate(w_parts, axis=0), carry


SB_PAR = 4


def _sb_prompt_kernel(seq, q_ref, k_ref, v_ref, km_ref, vm_ref, sb_ref, o_ref, kms_ref, vms_ref):
    kms_ref[...] = jnp.zeros_like(kms_ref)
    vms_ref[...] = jnp.zeros_like(vms_ref)
    for i in range(SB_PAR):
        kms_ref[i, 0:N_META, :] = km_ref[:, i * SB_HD:(i + 1) * SB_HD]
        vms_ref[i, 0:N_META, :] = vm_ref[:, i * SB_HD:(i + 1) * SB_HD]
    heads = range(SB_PAR)
    diag_limit = lax.broadcasted_iota(jnp.int32, (SUBLANE, SB_TILE), 1)

    def tile(kp, vp, q, carry, acc, limit):
        zt = _dot_nt(kp, q)
        wt, carry = _sb_scan(zt, carry, limit)
        return carry, acc + _dot_tn(vp, wt.astype(BF16))

    half = SB_TILE // 2

    def diag_tile(kp, vp, q):
        zt = _dot_nt(kp, q)
        c0 = jnp.zeros((1, half), F32)
        w_lo, c_lo = _sb_scan(zt[:half, :half], c0, diag_limit[:, :half])
        w_hi, c_hi = _sb_scan(zt[:, half:], c0, diag_limit[:, half:])
        w_lo = jnp.concatenate([w_lo, jnp.zeros((half, half), F32)], axis=0)
        wt = jnp.concatenate([w_lo, w_hi], axis=1).astype(BF16)
        return jnp.concatenate([c_lo, c_hi], axis=1), _dot_tn(vp, wt)

    def diag_tiles(base, qs):
        rows = pl.ds(base, SB_TILE)
        res = [diag_tile(k_ref[rows, i * SB_HD:(i + 1) * SB_HD], v_ref[rows, i * SB_HD:(i + 1) * SB_HD], qs[i])
               for i in heads]
        return [r[0] for r in res], [r[1] for r in res]

    def plain_tiles(t, qs, carries, accs):
        rows = pl.ds(pl.multiple_of(t * SB_TILE, SB_TILE), SB_TILE)
        res = [tile(k_ref[rows, i * SB_HD:(i + 1) * SB_HD], v_ref[rows, i * SB_HD:(i + 1) * SB_HD],
                    qs[i], carries[i], accs[i], None) for i in heads]
        return [r[0] for r in res], [r[1] for r in res]

    def meta_tiles(qs, carries, accs):
        res = [tile(_load_perm(kms_ref.at[i], 0, SB_MINI).astype(BF16),
                    _load_perm(vms_ref.at[i], 0, SB_MINI).astype(BF16),
                    qs[i], carries[i], accs[i], N_META) for i in heads]
        return [r[0] for r in res], [r[1] for r in res]

    def qblock(m, is_first):
        row0 = pl.multiple_of(m * SB_TILE, SB_TILE)
        rows = pl.ds(row0, SB_TILE)
        qs = [q_ref[rows, i * SB_HD:(i + 1) * SB_HD] for i in heads]
        carries, accs = diag_tiles(row0, qs)
        if is_first:
            _, accs = meta_tiles(qs, carries, accs)
        else:
            carries, accs = plain_tiles(m - 1, qs, carries, accs)

            def cond(st):
                t, carries, _ = st
                worst = functools.reduce(jnp.maximum, [jnp.max(c) for c in carries])
                return (t >= -1) & (worst > SB_CUTOFF_LOG2)

            def body(st):
                t, carries, accs = st
                carries, accs = lax.cond(t >= 0,
                                         lambda: plain_tiles(t, qs, carries, accs),
                                         lambda: meta_tiles(qs, carries, accs))
                return t - 1, carries, accs

            _, _, accs = lax.while_loop(cond, body, (m - 2, carries, accs))
        for i in heads:
            cols = slice(i * SB_HD, (i + 1) * SB_HD)
            o = accs[i].T * sb_ref[rows, cols].astype(F32)
            o_ref[rows, cols] = o.astype(BF16)

    qblock(0, True)

    def loop_body(m, carry):
        qblock(m, False)
        return carry

    lax.fori_loop(1, seq // SB_TILE, loop_body, 0)


def _sb_prompt(p_main, km32, vm32, batch, seq):
    gw = SB_PAR * SB_HD
    col = lambda off: pl.BlockSpec((seq, gw), lambda b, g: (b, off // gw + g))
    return pl.pallas_call(
        functools.partial(_sb_prompt_kernel, seq),
        grid=(batch, H_SB // SB_PAR),
        in_specs=[col(P_SQ), col(P_SK), col(P_SV),
                  pl.BlockSpec((N_META, gw), lambda b, g: (0, g)),
                  pl.BlockSpec((N_META, gw), lambda b, g: (0, g)),
                  col(P_GB)],
        out_specs=pl.BlockSpec((seq, gw), lambda b, g: (b, g)),
        out_shape=jax.ShapeDtypeStruct((batch * seq, SB_W), BF16),
        scratch_shapes=[pltpu.VMEM((SB_PAR, SB_MINI, SB_HD), F32), pltpu.VMEM((SB_PAR, SB_MINI, SB_HD), F32)],
        compiler_params=pltpu.CompilerParams(
            dimension_semantics=("arbitrary", "arbitrary"), vmem_limit_bytes=VMEM_LIMIT),
        name="sb_prompt",
    )(p_main, p_main, p_main, km32, vm32, p_main)


SB_GROUP = 4


def _load_perm_native(ref, key0, nk, head):
    parts = []
    for j in range(nk // SB_MINI):
        for c in range(SB_STRIDE):
            start = (key0 + SB_MINI * j + c) * H_SB + head
            parts.append(ref[pl.ds(start, SUBLANE, stride=SB_STRIDE * H_SB), :])
    return jnp.concatenate(parts, axis=0)


def _sb_sample_kernel(t_new, past, q_ref, kn_ref, vn_ref, ck_ref, cv_ref, sb_ref, o_ref,
                      kbuf, vbuf, kfar, vfar, sems, far_sems):
    b = pl.program_id(0)
    nb = pl.num_programs(0)
    gw = SB_GROUP * SB_HD
    nq = SB_GROUP * t_new
    ngroups = H_SB // SB_GROUP
    n_tiles = past // SB_TILE
    tile_rows = SB_TILE * H_SB

    def near_copies(bb, slot):
        rows = pl.ds((n_tiles - 1) * tile_rows, tile_rows)
        return (pltpu.make_async_copy(ck_ref.at[bb, rows, :], kbuf.at[slot], sems.at[slot, 0]),
                pltpu.make_async_copy(cv_ref.at[bb, rows, :], vbuf.at[slot], sems.at[slot, 1]))

    @pl.when(b == 0)
    def _():
        for cp in near_copies(0, 0):
            cp.start()

    slot = b % 2

    @pl.when(b + 1 < nb)
    def _():
        for cp in near_copies(b + 1, 1 - slot):
            cp.start()

    for cp in near_copies(b, slot):
        cp.wait()

    rh = lax.broadcasted_iota(jnp.int32, (gw, nq), 0) // SB_HD
    ch = lax.broadcasted_iota(jnp.int32, (gw, nq), 1) // t_new
    head_match = rh == ch
    new_limit = lax.broadcasted_iota(jnp.int32, (SUBLANE, nq), 1) % t_new

    def tile(kref, vref, nk, g, qbd, carry, acc, limit):
        hs = range(SB_GROUP * g, SB_GROUP * (g + 1))
        kp = jnp.concatenate([_load_perm_native(kref, 0, nk, h) for h in hs], axis=1).astype(BF16)
        vp = jnp.concatenate([_load_perm_native(vref, 0, nk, h) for h in hs], axis=1).astype(BF16)
        zt = _dot(kp, qbd)
        wt, carry = _sb_scan(zt, carry, limit)
        return carry, acc + _dot_tn(vp, wt.astype(BF16))

    qbds, carries, accs = [], [], []
    for g in range(ngroups):
        q = q_ref[:, g * gw:(g + 1) * gw].astype(F32)
        qt = jnp.concatenate([q] * SB_GROUP, axis=0).T
        qbd = jnp.where(head_match, qt, 0.0).astype(BF16)
        carry = jnp.zeros((1, nq), F32)
        acc = jnp.zeros((gw, nq), F32)
        carry, acc = tile(kn_ref, vn_ref, t_new, g, qbd, carry, acc, new_limit)
        carry, acc = tile(kbuf.at[slot], vbuf.at[slot], SB_TILE, g, qbd, carry, acc, None)
        qbds.append(qbd)
        carries.append(carry)
        accs.append(acc)

    def cond(st):
        t, carries, _ = st
        worst = functools.reduce(jnp.maximum, [jnp.max(c) for c in carries])
        return (t >= 0) & (worst > SB_CUTOFF_LOG2)

    def body(st):
        t, carries, accs = st
        rows = pl.ds(pl.multiple_of(t * tile_rows, tile_rows), tile_rows)
        far = (pltpu.make_async_copy(ck_ref.at[b, rows, :], kfar, far_sems.at[0]),
               pltpu.make_async_copy(cv_ref.at[b, rows, :], vfar, far_sems.at[1]))
        for cp in far:
            cp.start()
        for cp in far:
            cp.wait()
        res = [tile(kfar, vfar, SB_TILE, g, qbds[g], carries[g], accs[g], None) for g in range(ngroups)]
        return t - 1, [r[0] for r in res], [r[1] for r in res]

    _, _, accs = lax.while_loop(cond, body, (n_tiles - 2, carries, accs))
    for g in range(ngroups):
        at = accs[g].T
        for h in range(SB_GROUP):
            cols = slice(g * gw + h * SB_HD, g * gw + (h + 1) * SB_HD)
            o = at[h * t_new:(h + 1) * t_new, h * SB_HD:(h + 1) * SB_HD] * sb_ref[:, cols].astype(F32)
            o_ref[:, cols] = o.astype(BF16)


def _sb_sample(p_main, k5s, v5s, cache_k, cache_v, row0, dec_batch, t_new, past):
    rb0 = row0 // t_new
    tile_rows = SB_TILE * H_SB
    any_spec = pl.BlockSpec(memory_space=pl.ANY)
    new_spec = pl.BlockSpec((t_new * H_SB, SB_HD), lambda b: (b, 0))
    return pl.pallas_call(
        functools.partial(_sb_sample_kernel, t_new, past),
        grid=(dec_batch,),
        in_specs=[pl.BlockSpec((t_new, SB_W), lambda b: (rb0 + b, P_SQ // SB_W)),
                  new_spec, new_spec, any_spec, any_spec,
                  pl.BlockSpec((t_new, SB_W), lambda b: (rb0 + b, P_GB // SB_W))],
        out_specs=pl.BlockSpec((t_new, SB_W), lambda b: (b, 0)),
        out_shape=jax.ShapeDtypeStruct((dec_batch * t_new, SB_W), BF16),
        scratch_shapes=[
            pltpu.VMEM((2, tile_rows, SB_HD), F32),
            pltpu.VMEM((2, tile_rows, SB_HD), F32),
            pltpu.VMEM((tile_rows, SB_HD), F32),
            pltpu.VMEM((tile_rows, SB_HD), F32),
            pltpu.SemaphoreType.DMA((2, 2)),
            pltpu.SemaphoreType.DMA((2,)),
        ],
        compiler_params=pltpu.CompilerParams(
            dimension_semantics=("arbitrary",), vmem_limit_bytes=VMEM_LIMIT),
        name="sb_sample",
    )(p_main, k5s, v5s, cache_k, cache_v, p_main)


def _merge_kernel(n_prompt_tiles, ogp_ref, ogs_ref, obp_ref, obs_ref, xp_ref, xs_ref, w_ref, o_ref):
    m = pl.program_id(0)

    def run(og_ref, ob_ref, x_ref):
        mix = (og_ref[...].astype(F32) + ob_ref[...].astype(F32)).astype(BF16)
        o_ref[...] = x_ref[...] + _dot(mix, w_ref[...])

    @pl.when(m < n_prompt_tiles)
    def _():
        run(ogp_ref, obp_ref, xp_ref)

    @pl.when(m >= n_prompt_tiles)
    def _():
        run(ogs_ref, obs_ref, xs_ref)


def _merge(og_p, og_s, ob_p, ob_s, xp, xs, w_out):
    npt, nst = xp.shape[0] // TM, xs.shape[0] // TM
    pidx = lambda m: (jnp.minimum(m, npt - 1), 0)
    sidx = lambda m: (jnp.maximum(m - npt, 0), 0)
    row = lambda idx: pl.BlockSpec((TM, D_MODEL), idx)
    return pl.pallas_call(
        functools.partial(_merge_kernel, npt),
        grid=(npt + nst,),
        in_specs=[row(pidx), row(sidx), row(pidx), row(sidx), row(pidx), row(sidx),
                  pl.BlockSpec((D_MODEL, D_MODEL), lambda m: (0, 0))],
        out_specs=pl.BlockSpec((TM, D_MODEL), lambda m: (m, 0)),
        out_shape=jax.ShapeDtypeStruct((xp.shape[0] + xs.shape[0], D_MODEL), F32),
        compiler_params=pltpu.CompilerParams(
            dimension_semantics=("arbitrary",), vmem_limit_bytes=VMEM_LIMIT),
        name="merge_outproj",
    )(og_p, og_s, ob_p, ob_s, xp, xs, w_out)


def _ffn_kernel(n_prompt_tiles, x_ref, g2_ref, wu_ref, wd_ref, gf_ref, yp_ref, ys_ref, h_ref, acc_ref):
    m = pl.program_id(0)
    f = pl.program_id(1)

    @pl.when(f == 0)
    def _():
        h_ref[...] = _rms(x_ref[...], g2_ref[...]).astype(BF16)
        acc_ref[...] = jnp.zeros_like(acc_ref)

    u = jnp.maximum(_dot(h_ref[...], wu_ref[...]), 0.0)
    acc_ref[...] += _dot((u * u).astype(BF16), wd_ref[...])

    @pl.when(f == pl.num_programs(1) - 1)
    def _():
        y = _rms(x_ref[...] + acc_ref[...], gf_ref[...])

        @pl.when(m < n_prompt_tiles)
        def _():
            yp_ref[...] = y

        @pl.when(m >= n_prompt_tiles)
        def _():
            ys_ref[...] = y


def _ffn(x1, g2, w_up, w_down, gf, npr, nsr):
    npt, nst = npr // TM, nsr // TM
    return pl.pallas_call(
        functools.partial(_ffn_kernel, npt),
        grid=(npt + nst, D_FF // TF),
        in_specs=[
            pl.BlockSpec((TM, D_MODEL), lambda m, f: (m, 0)),
            pl.BlockSpec((1, D_MODEL), lambda m, f: (0, 0)),
            pl.BlockSpec((D_MODEL, TF), lambda m, f: (0, f)),
            pl.BlockSpec((TF, D_MODEL), lambda m, f: (f, 0)),
            pl.BlockSpec((1, D_MODEL), lambda m, f: (0, 0)),
        ],
        out_specs=[
            pl.BlockSpec((TM, D_MODEL), lambda m, f: (jnp.minimum(m, npt - 1), 0)),
            pl.BlockSpec((TM, D_MODEL), lambda m, f: (jnp.maximum(m - npt, 0), 0)),
        ],
        out_shape=[
            jax.ShapeDtypeStruct((npr, D_MODEL), F32),
            jax.ShapeDtypeStruct((nsr, D_MODEL), F32),
        ],
        scratch_shapes=[pltpu.VMEM((TM, D_MODEL), BF16), pltpu.VMEM((TM, D_MODEL), F32)],
        compiler_params=pltpu.CompilerParams(
            dimension_semantics=("arbitrary", "arbitrary"), vmem_limit_bytes=VMEM_LIMIT),
        name="ffn_final",
    )(x1, g2, w_up, w_down, gf)


def kernel(x_prompt, x_sample, cache_sb_k, cache_sb_v, state_gla, meta_tokens, norm1_g, w_in,
           w_alpha_up, b_alpha, gla_norm_g, w_out, norm2_g, w_up, w_down, norm_f_g):
    batch, seq, _ = x_prompt.shape
    dec_batch, t_new, _ = x_sample.shape
    depth, _, past = cache_sb_k.shape[:3]
    assert depth == 1 and w_in.shape[2] == _R_END
    assert seq % TM == 0 and (dec_batch * t_new) % TM == 0 and past % SB_TILE == 0
    assert t_new == SB_MINI and t_new % 16 == 0
    npr, nsr = batch * seq, dec_batch * t_new

    w_main, w_alow = _prep_w_in(w_in[0].T)
    wup = jnp.pad(w_alpha_up[0], ((0, LANE - GLA_RANK), (0, 0))).astype(BF16)
    bias = b_alpha[0].reshape(1, GLA_QK)
    gn = gla_norm_g[0].reshape(1, GLA_V)
    g1 = norm1_g[0].reshape(1, D_MODEL)
    g2 = norm2_g[0].reshape(1, D_MODEL)
    gf = norm_f_g.reshape(1, D_MODEL)

    xp = x_prompt.reshape(npr, D_MODEL)
    xs = x_sample.reshape(nsr, D_MODEL)
    meta = meta_tokens.astype(x_prompt.dtype)

    (p_main, p_meta, km32, vm32, alow, alow_meta,
     k5p, v5p, k5s, v5s) = _inproj(xp, xs, meta, g1, w_main, w_alow, batch, seq)

    og_p, st_p, w_out_b, w_up_b, w_down_b = _gla_prompt(
        p_main, p_meta, alow, alow_meta, wup, bias, gn, w_out[0], w_up[0], w_down[0], batch, seq)
    og_s, st_s = _gla_sample(p_main, alow, state_gla[0], wup, bias, gn, npr, dec_batch, t_new)
    ob_p = _sb_prompt(p_main, km32, vm32, batch, seq)
    ck = cache_sb_k[0].reshape(dec_batch, past * H_SB, SB_HD)
    cv = cache_sb_v[0].reshape(dec_batch, past * H_SB, SB_HD)
    ob_s = _sb_sample(p_main, k5s.reshape(nsr * H_SB, SB_HD), v5s.reshape(nsr * H_SB, SB_HD),
                      ck, cv, npr, dec_batch, t_new, past)

    x1 = _merge(og_p, og_s, ob_p, ob_s, xp, xs, w_out_b)
    y_p, y_s = _ffn(x1, g2, w_up_b, w_down_b, gf, npr, nsr)

    return (
        y_p.reshape(batch, seq, D_MODEL),
        y_s.reshape(dec_batch, t_new, D_MODEL),
        st_p[None],
        k5p.reshape(1, batch, N_META + seq, H_SB, SB_HD),
        v5p.reshape(1, batch, N_META + seq, H_SB, SB_HD),
        st_s[None],
        k5s.reshape(1, dec_batch, t_new, H_SB, SB_HD),
        v5s.reshape(1, dec_batch, t_new, H_SB, SB_HD),
    )
```

```python
import functools

import jax
import jax.numpy as jnp
from jax import lax
from jax.experimental import pallas as pl
from jax.experimental.pallas import tpu as pltpu

F32 = jnp.float32
BF16 = jnp.bfloat16

D_MODEL = 2048
N_META = 16
H_GLA = 4
GLA_DK = 256
GLA_DV = 512
GLA_RANK = 16
GLA_TAU = 16.0
GLA_CHUNK = 64
H_SB = 16
SB_HD = 128
D_FF = 4 * D_MODEL
EPS = 1e-5
GLA_QK = H_GLA * GLA_DK
GLA_V = H_GLA * GLA_DV
SB_W = H_SB * SB_HD

_R_GK = GLA_QK
_R_GV = 2 * GLA_QK
_R_ALOW = _R_GV + GLA_V
_R_SQ = _R_ALOW + GLA_RANK
_R_SK = _R_SQ + SB_W
_R_SV = _R_SK + SB_W
_R_GA = _R_SV + SB_W
_R_GB = _R_GA + GLA_V
_R_END = _R_GB + SB_W

P_GQ = 0
P_GK = P_GQ + GLA_QK
P_GV = P_GK + GLA_QK
P_SQ = P_GV + GLA_V
P_SK = P_SQ + SB_W
P_SV = P_SK + SB_W
P_GA = P_SV + SB_W
P_GB = P_GA + GLA_V
P_W = P_GB + SB_W

LANE = 128
SUBLANE = 8
TM = 512
TN = 2048
TF = 1024
GLA_ROWS = 256
SB_TILE = 256
SB_MINI = 32
SB_STRIDE = SB_MINI // SUBLANE
SB_CUTOFF = -104.0
LOG2_E = 1.4426950408889634
SB_SCALE_LOG2 = SB_HD ** -0.5 * LOG2_E
SB_CUTOFF_LOG2 = SB_CUTOFF * LOG2_E
VMEM_LIMIT = 56 * 1024 * 1024


def _dot(a, b):
    return jnp.dot(a, b, preferred_element_type=F32)


def _dot_nt(a, b):
    return lax.dot_general(a, b, (((1,), (1,)), ((), ())), preferred_element_type=F32)


def _dot_tn(a, b):
    return lax.dot_general(a, b, (((0,), (0,)), ((), ())), preferred_element_type=F32)


def _split2(x):
    hi = x.astype(BF16)
    lo = (x - hi.astype(F32)).astype(BF16)
    return hi, lo


def _rms(x, g):
    ms = jnp.mean(x * x, axis=-1, keepdims=True)
    return x * lax.rsqrt(ms + EPS) * g


def _log_sigmoid(x):
    return jnp.minimum(x, 0.0) - jnp.log(1.0 + jnp.exp(-jnp.abs(x)))


_PREP_ROWS = 1024
_PREP_COLS = 1024


def _prep_w_kernel(n_aligned, wt_ref, edge_ref, o_ref, al_ref):
    n = pl.program_id(1)

    @pl.when(n < n_aligned)
    def _():
        o_ref[...] = wt_ref[...].T.astype(BF16)

    def shifted():
        return jnp.concatenate([wt_ref[GLA_RANK:, :], edge_ref[...]], axis=0).T

    is_sq = (n >= P_SQ // _PREP_COLS) & (n < P_SK // _PREP_COLS)

    @pl.when((n >= n_aligned) & jnp.logical_not(is_sq))
    def _():
        o_ref[...] = shifted().astype(BF16)

    @pl.when(is_sq)
    def _():
        o_ref[...] = (shifted() * SB_SCALE_LOG2).astype(BF16)

    @pl.when(n == n_aligned)
    def _():
        pad = jnp.zeros((LANE - GLA_RANK, wt_ref.shape[1]), F32)
        al_ref[...] = jnp.concatenate([wt_ref[0:GLA_RANK, :], pad], axis=0).T.astype(BF16)


def _prep_w_in(wt):
    n_aligned = _R_ALOW // _PREP_COLS
    edge_blocks = _PREP_COLS // GLA_RANK
    return pl.pallas_call(
        functools.partial(_prep_w_kernel, n_aligned),
        grid=(D_MODEL // _PREP_ROWS, P_W // _PREP_COLS),
        in_specs=[
            pl.BlockSpec((_PREP_COLS, _PREP_ROWS), lambda r, n: (n, r)),
            pl.BlockSpec((GLA_RANK, _PREP_ROWS), lambda r, n: ((jnp.maximum(n, n_aligned) + 1) * edge_blocks, r)),
        ],
        out_specs=[pl.BlockSpec((_PREP_ROWS, _PREP_COLS), lambda r, n: (r, n)),
                   pl.BlockSpec((_PREP_ROWS, LANE), lambda r, n: (r, 0))],
        out_shape=[jax.ShapeDtypeStruct((D_MODEL, P_W), BF16),
                   jax.ShapeDtypeStruct((D_MODEL, LANE), BF16)],
        compiler_params=pltpu.CompilerParams(
            dimension_semantics=("arbitrary", "arbitrary"), vmem_limit_bytes=VMEM_LIMIT),
        name="prep_w_in",
    )(wt, wt)


_N_SK0 = P_SK // TN
_N_SV0 = P_SV // TN
_N_GATE0 = P_GA // TN


assert TN == SB_W


def _permute_rows(stage_ref):
    heads, rows, _ = stage_ref.shape
    cols = []
    for h in range(heads):
        parts = [stage_ref[h, pl.ds(SB_MINI * j + c, SUBLANE, stride=SB_STRIDE), :]
                 for j in range(rows // SB_MINI) for c in range(SB_STRIDE)]
        cols.append(jnp.concatenate(parts, axis=0).astype(BF16))
    return jnp.concatenate(cols, axis=1)


MAX_CAST_SLABS = 64


def _inproj_kernel(n_prompt_tiles, batch, seq, xp_ref, xs_ref, meta_ref, g1_ref, w_ref, wal_ref,
                   p_ref, pm_ref, km_ref, vm_ref, al_ref, alm_ref,
                   kp_ref, vp_ref, ks_ref, vs_ref,
                   h_ref, kstage_ref, vstage_ref, mstage_ref, sems, msem):
    m = pl.program_id(0)
    n = pl.program_id(1)
    n_tiles = pl.num_programs(0)
    tiles_per_batch = seq // TM

    def stage_heads(ref, vals):
        for h in range(H_SB):
            ref[h] = vals[:, h * SB_HD:(h + 1) * SB_HD]

    def native_copies(src_ref, rows, dst_ref, token, dma_sem):
        return [pltpu.make_async_copy(src_ref.at[h], dst_ref.at[pl.ds(token, rows), h, :], dma_sem)
                for h in range(H_SB)]

    def wait_native(stage_ref, sem):
        for cp in native_copies(stage_ref, TM, kp_ref, 0, sem):
            cp.wait()

    def write_native(stage_ref, sem, dst_p, dst_s):
        @pl.when(m < n_prompt_tiles)
        def _():
            b = m // tiles_per_batch
            token = b * (seq + N_META) + N_META + (m - b * tiles_per_batch) * TM
            for cp in native_copies(stage_ref, TM, dst_p, token, sem):
                cp.start()

        @pl.when(m >= n_prompt_tiles)
        def _():
            for cp in native_copies(stage_ref, TM, dst_s, (m - n_prompt_tiles) * TM, sem):
                cp.start()

    def write_native_meta(met, dst_p):
        stage_heads(mstage_ref, met)
        copies = [cp for b in range(batch)
                  for cp in native_copies(mstage_ref, N_META, dst_p, b * (seq + N_META), msem)]
        for cp in copies:
            cp.start()
        for cp in copies:
            cp.wait()

    first = m == 0

    @pl.when(n == 0)
    def _():
        g = g1_ref[...]

        @pl.when(m < n_prompt_tiles)
        def _():
            h_ref[0:TM, :] = _rms(xp_ref[...], g).astype(BF16)

        @pl.when(m >= n_prompt_tiles)
        def _():
            h_ref[0:TM, :] = _rms(xs_ref[...], g).astype(BF16)

        al_ref[...] = _dot(h_ref[0:TM, :], wal_ref[...])

        @pl.when(first)
        def _():
            h_ref[TM:TM + N_META, :] = _rms(meta_ref[...], g).astype(BF16)
            alm_ref[...] = _dot(h_ref[TM:TM + N_META, :], wal_ref[...])

    is_gate = n >= _N_GATE0
    is_k = (n >= _N_SK0) & (n < _N_SV0)
    is_v = (n >= _N_SV0) & (n < _N_GATE0)

    def column_tile(kind, epilogue, meta_epilogue):
        @pl.when(kind & first)
        def _():
            acc = _dot(h_ref[...], w_ref[...])
            epilogue(acc[:TM])
            pm_ref[...] = acc[TM:].astype(BF16)
            meta_epilogue(acc[TM:])

        @pl.when(kind & jnp.logical_not(first))
        def _():
            epilogue(_dot(h_ref[0:TM, :], w_ref[...]))

    def plain(main):
        p_ref[...] = main.astype(BF16)

    def gate(main):
        p_ref[...] = jax.nn.sigmoid(main).astype(BF16)

    def kv(stage_ref, sem, dst_p, dst_s):
        def epilogue(main):
            stage_heads(stage_ref, main)
            write_native(stage_ref, sem, dst_p, dst_s)
            p_ref[...] = _permute_rows(stage_ref)
        return epilogue

    def kv_meta(f32_ref, dst_p):
        def epilogue(met):
            f32_ref[...] = met
            write_native_meta(met, dst_p)
        return epilogue

    @pl.when(is_k & (m > 0))
    def _():
        wait_native(kstage_ref, sems.at[0])

    @pl.when(is_v & (m > 0))
    def _():
        wait_native(vstage_ref, sems.at[1])

    column_tile(jnp.logical_not(is_gate | is_k | is_v), plain, lambda met: None)
    column_tile(is_gate, gate, lambda met: None)
    column_tile(is_k, kv(kstage_ref, sems.at[0], kp_ref, ks_ref), kv_meta(km_ref, kp_ref))
    column_tile(is_v, kv(vstage_ref, sems.at[1], vp_ref, vs_ref), kv_meta(vm_ref, vp_ref))

    @pl.when((m == n_tiles - 1) & (n == pl.num_programs(1) - 1))
    def _():
        wait_native(kstage_ref, sems.at[0])
        wait_native(vstage_ref, sems.at[1])


def _inproj(xp, xs, meta, g1, w_main, w_alow, batch, seq):
    npr, nsr = xp.shape[0], xs.shape[0]
    npt, nst = npr // TM, nsr // TM
    nt = npt + nst
    rows = npr + nsr
    nn = P_W // TN
    kcol = lambda n: jnp.clip(n - _N_SK0, 0, SB_W // TN - 1)
    vcol = lambda n: jnp.clip(n - _N_SV0, 0, SB_W // TN - 1)
    once = lambda m, col, last: jnp.where(m == 0, col, last)
    any_spec = pl.BlockSpec(memory_space=pl.ANY)
    return pl.pallas_call(
        functools.partial(_inproj_kernel, npt, batch, seq),
        grid=(nt, nn),
        in_specs=[
            pl.BlockSpec((TM, D_MODEL), lambda m, n: (jnp.minimum(m, npt - 1), 0)),
            pl.BlockSpec((TM, D_MODEL), lambda m, n: (jnp.maximum(m - npt, 0), 0),
                         pipeline_mode=pl.Buffered(1)),
            pl.BlockSpec((N_META, D_MODEL), lambda m, n: (0, 0)),
            pl.BlockSpec((1, D_MODEL), lambda m, n: (0, 0)),
            pl.BlockSpec((D_MODEL, TN), lambda m, n: (0, n)),
            pl.BlockSpec((D_MODEL, LANE), lambda m, n: (0, 0)),
        ],
        out_specs=[
            pl.BlockSpec((TM, TN), lambda m, n: (m, n)),
            pl.BlockSpec((N_META, TN), lambda m, n: (0, once(m, n, nn - 1))),
            pl.BlockSpec((N_META, TN), lambda m, n: (0, once(m, kcol(n), SB_W // TN - 1))),
            pl.BlockSpec((N_META, TN), lambda m, n: (0, once(m, vcol(n), SB_W // TN - 1))),
            pl.BlockSpec((TM, LANE), lambda m, n: (m, 0)),
            pl.BlockSpec((N_META, LANE), lambda m, n: (0, 0)),
            any_spec, any_spec, any_spec, any_spec,
        ],
        out_shape=[
            jax.ShapeDtypeStruct((rows, P_W), BF16),
            jax.ShapeDtypeStruct((N_META, P_W), BF16),
            jax.ShapeDtypeStruct((N_META, SB_W), F32),
            jax.ShapeDtypeStruct((N_META, SB_W), F32),
            jax.ShapeDtypeStruct((rows, LANE), F32),
            jax.ShapeDtypeStruct((N_META, LANE), F32),
            jax.ShapeDtypeStruct((batch * (seq + N_META), H_SB, SB_HD), F32),
            jax.ShapeDtypeStruct((batch * (seq + N_META), H_SB, SB_HD), F32),
            jax.ShapeDtypeStruct((nsr, H_SB, SB_HD), F32),
            jax.ShapeDtypeStruct((nsr, H_SB, SB_HD), F32),
        ],
        scratch_shapes=[
            pltpu.VMEM((TM + N_META, D_MODEL), BF16),
            pltpu.VMEM((H_SB, TM, SB_HD), F32),
            pltpu.VMEM((H_SB, TM, SB_HD), F32),
            pltpu.VMEM((H_SB, N_META, SB_HD), F32),
            pltpu.SemaphoreType.DMA((2,)),
            pltpu.SemaphoreType.DMA(()),
        ],
        compiler_params=pltpu.CompilerParams(
            dimension_semantics=("arbitrary", "arbitrary"), vmem_limit_bytes=VMEM_LIMIT),
        name="inproj",
    )(xp, xs, meta, g1, w_main, w_alow)


def _gla_chunk(c, q, k, v, al, wup, bias, st_ref):
    x = _dot(al.astype(BF16), wup) + bias
    g = _log_sigmoid(x) * (1.0 / GLA_TAU)
    g_hi, g_lo = _split2(g)
    row = lax.broadcasted_iota(jnp.int32, (c, c), 0)
    col = lax.broadcasted_iota(jnp.int32, (c, c), 1)
    causal = col <= row
    tri = causal.astype(BF16)
    cum = _dot(tri, g_hi) + _dot(tri, g_lo)
    outs = []
    for h in range(H_GLA):
        ch = cum[:, h * GLA_DK:(h + 1) * GLA_DK]
        last = ch[c - 1:c, :]
        qh = q[:, h * GLA_DK:(h + 1) * GLA_DK].astype(F32)
        kh = k[:, h * GLA_DK:(h + 1) * GLA_DK].astype(F32)
        vh = v[:, h * GLA_DV:(h + 1) * GLA_DV]
        qd = (qh * jnp.exp(ch) * (GLA_DK ** -0.5)).astype(BF16)
        kd = (kh * jnp.exp(-ch)).astype(BF16)
        kr = (kh * jnp.exp(last - ch)).astype(BF16)
        att = jnp.where(causal, _dot_nt(qd, kd), 0.0).astype(BF16)
        st = st_ref[h]
        outs.append(_dot(att, vh) + _dot_nt(qd, st.astype(BF16)))
        st_ref[h] = st * jnp.exp(last) + _dot_tn(vh, kr)
    return outs


def _gla_finish(o, gn, sa):
    ms = jnp.mean(o * o, axis=-1, keepdims=True)
    return (o * lax.rsqrt(ms + EPS) * gn * sa.astype(F32)).astype(BF16)


def _gla_prompt_kernel(n_cast_slabs, n_steps, q_ref, k_ref, v_ref, sa_ref, al_ref, km_ref, vm_ref, alm_ref,
                       wup_ref, b_ref, gn_ref, wo_ref, wu_ref, wd_ref,
                       og_ref, so_ref, wob_ref, wub_ref, wdb_ref, st_ref):
    c_idx = pl.program_id(1)
    wup = wup_ref[...]
    bias = b_ref[...]

    @pl.when(c_idx == 0)
    def _():
        st_ref[...] = jnp.zeros_like(st_ref)
        zq = jnp.zeros((N_META, GLA_QK), BF16)
        _gla_chunk(N_META, zq, km_ref[...], vm_ref[...], alm_ref[...], wup, bias, st_ref)

    def cast_slab():
        wob_ref[...] = wo_ref[...].astype(BF16)
        wub_ref[...] = wu_ref[...].astype(BF16)
        wdb_ref[...] = wd_ref[...].astype(BF16)

    if n_cast_slabs == n_steps:
        cast_slab()
    else:
        pl.when(pl.program_id(0) * pl.num_programs(1) + c_idx < n_cast_slabs)(cast_slab)

    def body(i, carry):
        r0 = pl.multiple_of(i * GLA_CHUNK, GLA_CHUNK)
        rows = pl.ds(r0, GLA_CHUNK)
        outs = _gla_chunk(GLA_CHUNK, q_ref[rows, :], k_ref[rows, :], v_ref[rows, :], al_ref[rows, :],
                          wup, bias, st_ref)
        for h in range(H_GLA):
            cols = slice(h * GLA_DV, (h + 1) * GLA_DV)
            og_ref[rows, cols] = _gla_finish(outs[h], gn_ref[:, cols], sa_ref[rows, cols])
        return carry

    lax.fori_loop(0, GLA_ROWS // GLA_CHUNK, body, 0, unroll=True)

    @pl.when(c_idx == pl.num_programs(1) - 1)
    def _():
        for h in range(H_GLA):
            so_ref[0, h] = st_ref[h].T


def _gla_prompt(p_main, p_meta, alow, alow_meta, wup, bias, gn, w_out, w_up, w_down, batch, seq):
    nc = seq // GLA_ROWS
    rb = lambda b, c: b * nc + c
    const = lambda b, c: (0, 0)
    n_slabs = min(MAX_CAST_SLABS, 1 << ((batch * nc).bit_length() - 1))
    slab = lambda b, c: (jnp.minimum(rb(b, c), n_slabs - 1), 0)
    casts = (w_out, w_up, w_down)
    cast_specs = [pl.BlockSpec((w.shape[0] // n_slabs, w.shape[1]), slab) for w in casts]
    return pl.pallas_call(
        functools.partial(_gla_prompt_kernel, n_slabs, batch * nc),
        grid=(batch, nc),
        in_specs=[
            pl.BlockSpec((GLA_ROWS, GLA_QK), lambda b, c: (rb(b, c), P_GQ // GLA_QK)),
            pl.BlockSpec((GLA_ROWS, GLA_QK), lambda b, c: (rb(b, c), P_GK // GLA_QK)),
            pl.BlockSpec((GLA_ROWS, GLA_V), lambda b, c: (rb(b, c), P_GV // GLA_V)),
            pl.BlockSpec((GLA_ROWS, GLA_V), lambda b, c: (rb(b, c), P_GA // GLA_V)),
            pl.BlockSpec((GLA_ROWS, LANE), lambda b, c: (rb(b, c), 0)),
            pl.BlockSpec((N_META, GLA_QK), lambda b, c: (0, P_GK // GLA_QK)),
            pl.BlockSpec((N_META, GLA_V), lambda b, c: (0, P_GV // GLA_V)),
            pl.BlockSpec((N_META, LANE), const),
            pl.BlockSpec((LANE, GLA_QK), const),
            pl.BlockSpec((1, GLA_QK), const),
            pl.BlockSpec((1, GLA_V), const),
        ] + cast_specs,
        out_specs=[
            pl.BlockSpec((GLA_ROWS, GLA_V), lambda b, c: (rb(b, c), 0)),
            pl.BlockSpec((1, H_GLA, GLA_DK, GLA_DV), lambda b, c: (b, 0, 0, 0)),
        ] + cast_specs,
        out_shape=[
            jax.ShapeDtypeStruct((batch * seq, GLA_V), BF16),
            jax.ShapeDtypeStruct((batch, H_GLA, GLA_DK, GLA_DV), F32),
        ] + [jax.ShapeDtypeStruct(w.shape, BF16) for w in casts],
        scratch_shapes=[pltpu.VMEM((H_GLA, GLA_DV, GLA_DK), F32)],
        compiler_params=pltpu.CompilerParams(
            dimension_semantics=("arbitrary", "arbitrary"), vmem_limit_bytes=VMEM_LIMIT),
        name="gla_prompt",
    )(p_main, p_main, p_main, p_main, alow, p_meta, p_meta, alow_meta, wup, bias, gn, *casts)


def _gla_sample_kernel(t_new, q_ref, k_ref, v_ref, sa_ref, al_ref, s0_ref,
                       wup_ref, b_ref, gn_ref, og_ref, so_ref, st_ref):
    for h in range(H_GLA):
        st_ref[h] = s0_ref[0, h].T
    outs = _gla_chunk(t_new, q_ref[...], k_ref[...], v_ref[...], al_ref[...],
                      wup_ref[...], b_ref[...], st_ref)
    for h in range(H_GLA):
        cols = slice(h * GLA_DV, (h + 1) * GLA_DV)
        og_ref[:, cols] = _gla_finish(outs[h], gn_ref[:, cols], sa_ref[:, cols])
        so_ref[0, h] = st_ref[h].T


def _gla_sample(p_main, alow, state, wup, bias, gn, row0, dec_batch, t_new):
    rb0 = row0 // t_new
    const = lambda b: (0, 0)
    return pl.pallas_call(
        functools.partial(_gla_sample_kernel, t_new),
        grid=(dec_batch,),
        in_specs=[
            pl.BlockSpec((t_new, GLA_QK), lambda b: (rb0 + b, P_GQ // GLA_QK)),
            pl.BlockSpec((t_new, GLA_QK), lambda b: (rb0 + b, P_GK // GLA_QK)),
            pl.BlockSpec((t_new, GLA_V), lambda b: (rb0 + b, P_GV // GLA_V)),
            pl.BlockSpec((t_new, GLA_V), lambda b: (rb0 + b, P_GA // GLA_V)),
            pl.BlockSpec((t_new, LANE), lambda b: (rb0 + b, 0)),
            pl.BlockSpec((1, H_GLA, GLA_DK, GLA_DV), lambda b: (b, 0, 0, 0)),
            pl.BlockSpec((LANE, GLA_QK), const),
            pl.BlockSpec((1, GLA_QK), const),
            pl.BlockSpec((1, GLA_V), const),
        ],
        out_specs=[
            pl.BlockSpec((t_new, GLA_V), lambda b: (b, 0)),
            pl.BlockSpec((1, H_GLA, GLA_DK, GLA_DV), lambda b: (b, 0, 0, 0)),
        ],
        out_shape=[
            jax.ShapeDtypeStruct((dec_batch * t_new, GLA_V), BF16),
            jax.ShapeDtypeStruct((dec_batch, H_GLA, GLA_DK, GLA_DV), F32),
        ],
        scratch_shapes=[pltpu.VMEM((H_GLA, GLA_DV, GLA_DK), F32)],
        compiler_params=pltpu.CompilerParams(
            dimension_semantics=("arbitrary",), vmem_limit_bytes=VMEM_LIMIT),
        name="gla_sample",
    )(p_main, p_main, p_main, p_main, alow, state, wup, bias, gn)


def _load_perm(ref, base, nk):
    parts = []
    for j in range(nk // SB_MINI):
        for c in range(SB_STRIDE):
            parts.append(ref[pl.ds(base + SB_MINI * j + c, SUBLANE, stride=SB_STRIDE), :])
    return jnp.concatenate(parts, axis=0)


def _sb_scan(zt, carry, limit):
    nk, nq = zt.shape
    rowid = lax.broadcasted_iota(jnp.int32, (SUBLANE, nq), 0)
    w_parts = [None] * (nk // SUBLANE)
    for j in reversed(range(nk // SB_MINI)):
        ls, ss, vis = [], [], []
        for c in range(SB_STRIDE):
            i = j * SB_STRIDE + c
            z = zt[i * SUBLANE:(i + 1) * SUBLANE]
            nz = -z
            t = jnp.log2(1.0 + jnp.exp2(jnp.minimum(z, nz)))
            l = jnp.minimum(nz, 0.0) - t
            ss.append(l + z)
            if limit is not None:
                v = rowid * SB_STRIDE + (SB_MINI * j + c) < limit
                l = jnp.where(v, l, 0.0)
                vis.append(v)
            ls.append(l)
        later = [None] * SB_STRIDE
        run = ls[SB_STRIDE - 1]
        for c in range(SB_STRIDE - 2, -1, -1):
            later[c] = run
            run = run + ls[c]
        incl = run
        for sh in (1, 2, 4):
            incl = incl + jnp.where(rowid < SUBLANE - sh, pltpu.roll(incl, SUBLANE - sh, axis=0), 0.0)
        off = carry + (incl - run)
        for c in range(SB_STRIDE):
            after = off if later[c] is None else off + later[c]
            w = jnp.exp2(ss[c] + after)
            if limit is not None:
                w = jnp.where(vis[c], w, 0.0)
            w_parts[j * SB_STRIDE + c] = w
        carry = carry + incl[0:1, :]
    return jnp.concatenate(w_parts, axis=0), carry


SB_PAR = 4


def _sb_prompt_kernel(seq, q_ref, k_ref, v_ref, km_ref, vm_ref, sb_ref, o_ref, kms_ref, vms_ref):
    kms_ref[...] = jnp.zeros_like(kms_ref)
    vms_ref[...] = jnp.zeros_like(vms_ref)
    for i in range(SB_PAR):
        kms_ref[i, 0:N_META, :] = km_ref[:, i * SB_HD:(i + 1) * SB_HD]
        vms_ref[i, 0:N_META, :] = vm_ref[:, i * SB_HD:(i + 1) * SB_HD]
    heads = range(SB_PAR)
    diag_limit = lax.broadcasted_iota(jnp.int32, (SUBLANE, SB_TILE), 1)

    def tile(kp, vp, q, carry, acc, limit):
        zt = _dot_nt(kp, q)
        wt, carry = _sb_scan(zt, carry, limit)
        return carry, acc + _dot_tn(vp, wt.astype(BF16))

    half = SB_TILE // 2

    def diag_tile(kp, vp, q):
        zt = _dot_nt(kp, q)
        c0 = jnp.zeros((1, half), F32)
        w_lo, c_lo = _sb_scan(zt[:half, :half], c0, diag_limit[:, :half])
        w_hi, c_hi = _sb_scan(zt[:, half:], c0, diag_limit[:, half:])
        w_lo = jnp.concatenate([w_lo, jnp.zeros((half, half), F32)], axis=0)
        wt = jnp.concatenate([w_lo, w_hi], axis=1).astype(BF16)
        return jnp.concatenate([c_lo, c_hi], axis=1), _dot_tn(vp, wt)

    def diag_tiles(base, qs):
        rows = pl.ds(base, SB_TILE)
        res = [diag_tile(k_ref[rows, i * SB_HD:(i + 1) * SB_HD], v_ref[rows, i * SB_HD:(i + 1) * SB_HD], qs[i])
               for i in heads]
        return [r[0] for r in res], [r[1] for r in res]

    def plain_tiles(t, qs, carries, accs):
        rows = pl.ds(pl.multiple_of(t * SB_TILE, SB_TILE), SB_TILE)
        res = [tile(k_ref[rows, i * SB_HD:(i + 1) * SB_HD], v_ref[rows, i * SB_HD:(i + 1) * SB_HD],
                    qs[i], carries[i], accs[i], None) for i in heads]
        return [r[0] for r in res], [r[1] for r in res]

    def meta_tiles(qs, carries, accs):
        res = [tile(_load_perm(kms_ref.at[i], 0, SB_MINI).astype(BF16),
                    _load_perm(vms_ref.at[i], 0, SB_MINI).astype(BF16),
                    qs[i], carries[i], accs[i], N_META) for i in heads]
        return [r[0] for r in res], [r[1] for r in res]

    def qblock(m, is_first):
        row0 = pl.multiple_of(m * SB_TILE, SB_TILE)
        rows = pl.ds(row0, SB_TILE)
        qs = [q_ref[rows, i * SB_HD:(i + 1) * SB_HD] for i in heads]
        carries, accs = diag_tiles(row0, qs)
        if is_first:
            _, accs = meta_tiles(qs, carries, accs)
        else:
            carries, accs = plain_tiles(m - 1, qs, carries, accs)

            def cond(st):
                t, carries, _ = st
                worst = functools.reduce(jnp.maximum, [jnp.max(c) for c in carries])
                return (t >= -1) & (worst > SB_CUTOFF_LOG2)

            def body(st):
                t, carries, accs = st
                carries, accs = lax.cond(t >= 0,
                                         lambda: plain_tiles(t, qs, carries, accs),
                                         lambda: meta_tiles(qs, carries, accs))
                return t - 1, carries, accs

            _, _, accs = lax.while_loop(cond, body, (m - 2, carries, accs))
        for i in heads:
            cols = slice(i * SB_HD, (i + 1) * SB_HD)
            o = accs[i].T * sb_ref[rows, cols].astype(F32)
            o_ref[rows, cols] = o.astype(BF16)

    qblock(0, True)

    def loop_body(m, carry):
        qblock(m, False)
        return carry

    lax.fori_loop(1, seq // SB_TILE, loop_body, 0)


def _sb_prompt(p_main, km32, vm32, batch, seq):
    gw = SB_PAR * SB_HD
    col = lambda off: pl.BlockSpec((seq, gw), lambda b, g: (b, off // gw + g))
    return pl.pallas_call(
        functools.partial(_sb_prompt_kernel, seq),
        grid=(batch, H_SB // SB_PAR),
        in_specs=[col(P_SQ), col(P_SK), col(P_SV),
                  pl.BlockSpec((N_META, gw), lambda b, g: (0, g)),
                  pl.BlockSpec((N_META, gw), lambda b, g: (0, g)),
                  col(P_GB)],
        out_specs=pl.BlockSpec((seq, gw), lambda b, g: (b, g)),
        out_shape=jax.ShapeDtypeStruct((batch * seq, SB_W), BF16),
        scratch_shapes=[pltpu.VMEM((SB_PAR, SB_MINI, SB_HD), F32), pltpu.VMEM((SB_PAR, SB_MINI, SB_HD), F32)],
        compiler_params=pltpu.CompilerParams(
            dimension_semantics=("arbitrary", "arbitrary"), vmem_limit_bytes=VMEM_LIMIT),
        name="sb_prompt",
    )(p_main, p_main, p_main, km32, vm32, p_main)


SB_GROUP = 4


def _load_perm_native(ref, key0, nk, head):
    parts = []
    for j in range(nk // SB_MINI):
        for c in range(SB_STRIDE):
            start = (key0 + SB_MINI * j + c) * H_SB + head
            parts.append(ref[pl.ds(start, SUBLANE, stride=SB_STRIDE * H_SB), :])
    return jnp.concatenate(parts, axis=0)


def _sb_sample_kernel(t_new, past, q_ref, kn_ref, vn_ref, ck_ref, cv_ref, sb_ref, o_ref,
                      kbuf, vbuf, kfar, vfar, sems, far_sems):
    b = pl.program_id(0)
    nb = pl.num_programs(0)
    gw = SB_GROUP * SB_HD
    nq = SB_GROUP * t_new
    ngroups = H_SB // SB_GROUP
    n_tiles = past // SB_TILE
    tile_rows = SB_TILE * H_SB

    def near_copies(bb, slot):
        rows = pl.ds((n_tiles - 1) * tile_rows, tile_rows)
        return (pltpu.make_async_copy(ck_ref.at[bb, rows, :], kbuf.at[slot], sems.at[slot, 0]),
                pltpu.make_async_copy(cv_ref.at[bb, rows, :], vbuf.at[slot], sems.at[slot, 1]))

    @pl.when(b == 0)
    def _():
        for cp in near_copies(0, 0):
            cp.start()

    slot = b % 2

    @pl.when(b + 1 < nb)
    def _():
        for cp in near_copies(b + 1, 1 - slot):
            cp.start()

    for cp in near_copies(b, slot):
        cp.wait()

    rh = lax.broadcasted_iota(jnp.int32, (gw, nq), 0) // SB_HD
    ch = lax.broadcasted_iota(jnp.int32, (gw, nq), 1) // t_new
    head_match = rh == ch
    new_limit = lax.broadcasted_iota(jnp.int32, (SUBLANE, nq), 1) % t_new

    def tile(kref, vref, nk, g, qbd, carry, acc, limit):
        hs = range(SB_GROUP * g, SB_GROUP * (g + 1))
        kp = jnp.concatenate([_load_perm_native(kref, 0, nk, h) for h in hs], axis=1).astype(BF16)
        vp = jnp.concatenate([_load_perm_native(vref, 0, nk, h) for h in hs], axis=1).astype(BF16)
        zt = _dot(kp, qbd)
        wt, carry = _sb_scan(zt, carry, limit)
        return carry, acc + _dot_tn(vp, wt.astype(BF16))

    qbds, carries, accs = [], [], []
    for g in range(ngroups):
        q = q_ref[:, g * gw:(g + 1) * gw].astype(F32)
        qt = jnp.concatenate([q] * SB_GROUP, axis=0).T
        qbd = jnp.where(head_match, qt, 0.0).astype(BF16)
        carry = jnp.zeros((1, nq), F32)
        acc = jnp.zeros((gw, nq), F32)
        carry, acc = tile(kn_ref, vn_ref, t_new, g, qbd, carry, acc, new_limit)
        carry, acc = tile(kbuf.at[slot], vbuf.at[slot], SB_TILE, g, qbd, carry, acc, None)
        qbds.append(qbd)
        carries.append(carry)
        accs.append(acc)

    def cond(st):
        t, carries, _ = st
        worst = functools.reduce(jnp.maximum, [jnp.max(c) for c in carries])
        return (t >= 0) & (worst > SB_CUTOFF_LOG2)

    def body(st):
        t, carries, accs = st
        rows = pl.ds(pl.multiple_of(t * tile_rows, tile_rows), tile_rows)
        far = (pltpu.make_async_copy(ck_ref.at[b, rows, :], kfar, far_sems.at[0]),
               pltpu.make_async_copy(cv_ref.at[b, rows, :], vfar, far_sems.at[1]))
        for cp in far:
            cp.start()
        for cp in far:
            cp.wait()
        res = [tile(kfar, vfar, SB_TILE, g, qbds[g], carries[g], accs[g], None) for g in range(ngroups)]
        return t - 1, [r[0] for r in res], [r[1] for r in res]

    _, _, accs = lax.while_loop(cond, body, (n_tiles - 2, carries, accs))
    for g in range(ngroups):
        at = accs[g].T
        for h in range(SB_GROUP):
            cols = slice(g * gw + h * SB_HD, g * gw + (h + 1) * SB_HD)
            o = at[h * t_new:(h + 1) * t_new, h * SB_HD:(h + 1) * SB_HD] * sb_ref[:, cols].astype(F32)
            o_ref[:, cols] = o.astype(BF16)


def _sb_sample(p_main, k5s, v5s, cache_k, cache_v, row0, dec_batch, t_new, past):
    rb0 = row0 // t_new
    tile_rows = SB_TILE * H_SB
    any_spec = pl.BlockSpec(memory_space=pl.ANY)
    new_spec = pl.BlockSpec((t_new * H_SB, SB_HD), lambda b: (b, 0))
    return pl.pallas_call(
        functools.partial(_sb_sample_kernel, t_new, past),
        grid=(dec_batch,),
        in_specs=[pl.BlockSpec((t_new, SB_W), lambda b: (rb0 + b, P_SQ // SB_W)),
                  new_spec, new_spec, any_spec, any_spec,
                  pl.BlockSpec((t_new, SB_W), lambda b: (rb0 + b, P_GB // SB_W))],
        out_specs=pl.BlockSpec((t_new, SB_W), lambda b: (b, 0)),
        out_shape=jax.ShapeDtypeStruct((dec_batch * t_new, SB_W), BF16),
        scratch_shapes=[
            pltpu.VMEM((2, tile_rows, SB_HD), F32),
            pltpu.VMEM((2, tile_rows, SB_HD), F32),
            pltpu.VMEM((tile_rows, SB_HD), F32),
            pltpu.VMEM((tile_rows, SB_HD), F32),
            pltpu.SemaphoreType.DMA((2, 2)),
            pltpu.SemaphoreType.DMA((2,)),
        ],
        compiler_params=pltpu.CompilerParams(
            dimension_semantics=("arbitrary",), vmem_limit_bytes=VMEM_LIMIT),
        name="sb_sample",
    )(p_main, k5s, v5s, cache_k, cache_v, p_main)


def _merge_kernel(n_prompt_tiles, ogp_ref, ogs_ref, obp_ref, obs_ref, xp_ref, xs_ref, w_ref, o_ref):
    m = pl.program_id(0)

    def run(og_ref, ob_ref, x_ref):
        mix = (og_ref[...].astype(F32) + ob_ref[...].astype(F32)).astype(BF16)
        o_ref[...] = x_ref[...] + _dot(mix, w_ref[...])

    @pl.when(m < n_prompt_tiles)
    def _():
        run(ogp_ref, obp_ref, xp_ref)

    @pl.when(m >= n_prompt_tiles)
    def _():
        run(ogs_ref, obs_ref, xs_ref)


def _merge(og_p, og_s, ob_p, ob_s, xp, xs, w_out):
    npt, nst = xp.shape[0] // TM, xs.shape[0] // TM
    pidx = lambda m: (jnp.minimum(m, npt - 1), 0)
    sidx = lambda m: (jnp.maximum(m - npt, 0), 0)
    row = lambda idx: pl.BlockSpec((TM, D_MODEL), idx)
    return pl.pallas_call(
        functools.partial(_merge_kernel, npt),
        grid=(npt + nst,),
        in_specs=[row(pidx), row(sidx), row(pidx), row(sidx), row(pidx), row(sidx),
                  pl.BlockSpec((D_MODEL, D_MODEL), lambda m: (0, 0))],
        out_specs=pl.BlockSpec((TM, D_MODEL), lambda m: (m, 0)),
        out_shape=jax.ShapeDtypeStruct((xp.shape[0] + xs.shape[0], D_MODEL), F32),
        compiler_params=pltpu.CompilerParams(
            dimension_semantics=("arbitrary",), vmem_limit_bytes=VMEM_LIMIT),
        name="merge_outproj",
    )(og_p, og_s, ob_p, ob_s, xp, xs, w_out)


def _ffn_kernel(n_prompt_tiles, x_ref, g2_ref, wu_ref, wd_ref, gf_ref, yp_ref, ys_ref, h_ref, acc_ref):
    m = pl.program_id(0)
    f = pl.program_id(1)

    last = pl.num_programs(1) - 1

    def mlp_slice(h):
        u = jnp.maximum(_dot(h, wu_ref[...]), 0.0)
        return _dot((u * u).astype(BF16), wd_ref[...])

    @pl.when(f == 0)
    def _():
        h = _rms(x_ref[...], g2_ref[...]).astype(BF16)
        h_ref[...] = h
        acc_ref[...] = mlp_slice(h)

    @pl.when((f > 0) & (f < last))
    def _():
        acc_ref[...] += mlp_slice(h_ref[...])

    @pl.when(f == last)
    def _():
        y = _rms(x_ref[...] + (acc_ref[...] + mlp_slice(h_ref[...])), gf_ref[...])

        @pl.when(m < n_prompt_tiles)
        def _():
            yp_ref[...] = y

        @pl.when(m >= n_prompt_tiles)
        def _():
            ys_ref[...] = y


def _ffn(x1, g2, w_up, w_down, gf, npr, nsr):
    npt, nst = npr // TM, nsr // TM
    assert D_FF // TF >= 2
    return pl.pallas_call(
        functools.partial(_ffn_kernel, npt),
        grid=(npt + nst, D_FF // TF),
        in_specs=[
            pl.BlockSpec((TM, D_MODEL), lambda m, f: (m, 0)),
            pl.BlockSpec((1, D_MODEL), lambda m, f: (0, 0)),
            pl.BlockSpec((D_MODEL, TF), lambda m, f: (0, f)),
            pl.BlockSpec((TF, D_MODEL), lambda m, f: (f, 0)),
            pl.BlockSpec((1, D_MODEL), lambda m, f: (0, 0)),
        ],
        out_specs=[
            pl.BlockSpec((TM, D_MODEL), lambda m, f: (jnp.minimum(m, npt - 1), 0)),
            pl.BlockSpec((TM, D_MODEL), lambda m, f: (jnp.maximum(m - npt, 0), 0)),
        ],
        out_shape=[
            jax.ShapeDtypeStruct((npr, D_MODEL), F32),
            jax.ShapeDtypeStruct((nsr, D_MODEL), F32),
        ],
        scratch_shapes=[pltpu.VMEM((TM, D_MODEL), BF16), pltpu.VMEM((TM, D_MODEL), F32)],
        compiler_params=pltpu.CompilerParams(
            dimension_semantics=("arbitrary", "arbitrary"), vmem_limit_bytes=VMEM_LIMIT),
        name="ffn_final",
    )(x1, g2, w_up, w_down, gf)


def kernel(x_prompt, x_sample, cache_sb_k, cache_sb_v, state_gla, meta_tokens, norm1_g, w_in,
           w_alpha_up, b_alpha, gla_norm_g, w_out, norm2_g, w_up, w_down, norm_f_g):
    batch, seq, _ = x_prompt.shape
    dec_batch, t_new, _ = x_sample.shape
    depth, _, past = cache_sb_k.shape[:3]
    assert depth == 1 and w_in.shape[2] == _R_END
    assert seq % TM == 0 and (dec_batch * t_new) % TM == 0 and past % SB_TILE == 0
    assert t_new == SB_MINI and t_new % 16 == 0
    npr, nsr = batch * seq, dec_batch * t_new

    w_main, w_alow = _prep_w_in(w_in[0].T)
    wup = jnp.pad(w_alpha_up[0], ((0, LANE - GLA_RANK), (0, 0))).astype(BF16)
    bias = b_alpha[0].reshape(1, GLA_QK)
    gn = gla_norm_g[0].reshape(1, GLA_V)
    g1 = norm1_g[0].reshape(1, D_MODEL)
    g2 = norm2_g[0].reshape(1, D_MODEL)
    gf = norm_f_g.reshape(1, D_MODEL)

    xp = x_prompt.reshape(npr, D_MODEL)
    xs = x_sample.reshape(nsr, D_MODEL)
    meta = meta_tokens.astype(x_prompt.dtype)

    (p_main, p_meta, km32, vm32, alow, alow_meta,
     k5p, v5p, k5s, v5s) = _inproj(xp, xs, meta, g1, w_main, w_alow, batch, seq)

    og_p, st_p, w_out_b, w_up_b, w_down_b = _gla_prompt(
        p_main, p_meta, alow, alow_meta, wup, bias, gn, w_out[0], w_up[0], w_down[0], batch, seq)
    og_s, st_s = _gla_sample(p_main, alow, state_gla[0], wup, bias, gn, npr, dec_batch, t_new)
    ob_p = _sb_prompt(p_main, km32, vm32, batch, seq)
    ck = cache_sb_k[0].reshape(dec_batch, past * H_SB, SB_HD)
    cv = cache_sb_v[0].reshape(dec_batch, past * H_SB, SB_HD)
    ob_s = _sb_sample(p_main, k5s.reshape(nsr * H_SB, SB_HD), v5s.reshape(nsr * H_SB, SB_HD),
                      ck, cv, npr, dec_batch, t_new, past)

    x1 = _merge(og_p, og_s, ob_p, ob_s, xp, xs, w_out_b)
    y_p, y_s = _ffn(x1, g2, w_up_b, w_down_b, gf, npr, nsr)

    return (
        y_p.reshape(batch, seq, D_MODEL),
        y_s.reshape(dec_batch, t_new, D_MODEL),
        st_p[None],
        k5p.reshape(1, batch, N_META + seq, H_SB, SB_HD),
        v5p.reshape(1, batch, N_META + seq, H_SB, SB_HD),
        st_s[None],
        k5s.reshape(1, dec_batch, t_new, H_SB, SB_HD),
        v5s.reshape(1, dec_batch, t_new, H_SB, SB_HD),
    )
```
